```python
import math
import jax, jax.numpy as jnp
from jax import lax
import numpy as np

D_MODEL = 2048
BATCH = 2
SEQ = 16384
DEPTH = 2

BRANCH_WIDTH = 1024
N_BRANCHES = 3
BLOCK_Q = 128
NORM_EPS = 1e-6
DA_HEADS = 8
DA_HEAD_DIM = 64
DA_V_DIM = 2 * DA_HEAD_DIM
ML_HEADS = 4
ML_QK_DIM = 128
ML_V_DIM = 256
ML_CONV = 4
ML_CHUNK = 64
ML_QK_WIDTH = 2 * ML_HEADS * ML_QK_DIM
DSA_HEADS = 8
DSA_LATENT = 256
DSA_V_DIM = 128
IDX_HEADS = 16
IDX_DIM = 64
IDX_TOPK_MAX = 256

IN_SPLITS = (
    DA_HEADS * 2 * DA_HEAD_DIM,
    DA_HEADS * 2 * DA_HEAD_DIM,
    DA_HEADS * DA_V_DIM,
    BRANCH_WIDTH,
    ML_QK_WIDTH,
    ML_HEADS * ML_V_DIM,
    2 * ML_HEADS,
    BRANCH_WIDTH,
    BRANCH_WIDTH,
    DSA_HEADS * DSA_LATENT,
    DSA_LATENT,
    IDX_HEADS * IDX_DIM,
    IDX_DIM,
    IDX_HEADS,
    BRANCH_WIDTH,
    N_BRANCHES * D_MODEL,
)
N_IN = sum(IN_SPLITS)

kernel_name = "hybrid_diffattn_mlstm_dsa_gated"


def rmsnorm(x, g):
    x32 = x.astype(jnp.float32)
    y = x32 * lax.rsqrt(jnp.mean(x32 * x32, axis=-1, keepdims=True) + NORM_EPS)
    return (y * g.astype(jnp.float32)).astype(x.dtype)


def to_blocks(a, n):
    b, s = a.shape[:2]
    a = a.reshape((b, s // n, n) + a.shape[2:])
    return jnp.moveaxis(a, 1, 0)


def from_blocks(a):
    a = jnp.moveaxis(a, 0, 1)
    return a.reshape((a.shape[0], a.shape[1] * a.shape[2]) + a.shape[3:])


def diff_attention(q, k, v, lam_params, norm_g, lam_init):
    b, s, _ = q.shape
    q = q.reshape(b, s, DA_HEADS, 2, DA_HEAD_DIM)
    k = k.reshape(b, s, DA_HEADS, 2, DA_HEAD_DIM)
    v = v.reshape(b, s, DA_HEADS, DA_V_DIM)
    k1, k2 = k[:, :, :, 0], k[:, :, :, 1]
    lp = lam_params.astype(jnp.float32)
    lam = jnp.exp(jnp.sum(lp[0] * lp[1])) - jnp.exp(jnp.sum(lp[2] * lp[3])) + lam_init
    scale = DA_HEAD_DIM ** -0.5
    kpos = jnp.arange(s)

    def block(args):
        i, qb = args
        qpos = i * BLOCK_Q + jnp.arange(BLOCK_Q)
        mask = kpos[None, :] <= qpos[:, None]
        s1 = jnp.einsum('bqhd,bkhd->bhqk', qb[:, :, :, 0], k1).astype(jnp.float32) * scale
        s2 = jnp.einsum('bqhd,bkhd->bhqk', qb[:, :, :, 1], k2).astype(jnp.float32) * scale
        a = (jax.nn.softmax(jnp.where(mask, s1, -jnp.inf), axis=-1)
             - lam * jax.nn.softmax(jnp.where(mask, s2, -jnp.inf), axis=-1))
        return jnp.einsum('bhqk,bkhv->bqhv', a.astype(v.dtype), v)

    nb = s // BLOCK_Q
    o = from_blocks(lax.map(block, (jnp.arange(nb), to_blocks(q, BLOCK_Q))))
    o = rmsnorm(o, norm_g) * (1.0 - lam_init)
    return o.reshape(b, s, DA_HEADS * DA_V_DIM)


def causal_conv(x, w, bias):
    kw, s = w.shape[0], x.shape[1]
    xp = jnp.pad(x, ((0, 0), (kw - 1, 0), (0, 0)))
    y = xp[:, 0:s] * w[0]
    for j in range(1, kw):
        y = y + xp[:, j:j + s] * w[j]
    return y + bias


def mlstm_chunk(carry, xs):
    C, n, m = carry
    q, k, v, logf, ig = xs
    L = q.shape[2]
    bcum = jnp.cumsum(logf, axis=-1)
    causal = jnp.tril(jnp.ones((L, L), dtype=bool))
    dmat = jnp.where(causal, bcum[..., :, None] - bcum[..., None, :] + ig[..., None, :], -jnp.inf)
    inter = bcum + m[..., None]
    m_t = jnp.maximum(jnp.max(dmat, axis=-1), inter)
    w = jnp.exp(dmat - m_t[..., None]) * jnp.einsum('bhtd,bhsd->bhts', q, k)
    decay = jnp.exp(inter - m_t)
    num = jnp.einsum('bhts,bhsv->bhtv', w, v) + decay[..., None] * jnp.einsum('bhtd,bhdv->bhtv', q, C)
    den = jnp.sum(w, axis=-1) + decay * jnp.einsum('bhtd,bhd->bht', q, n)
    h = num / jnp.maximum(jnp.abs(den), jnp.exp(-m_t))[..., None]
    b_last = bcum[..., -1]
    g = b_last[..., None] - bcum + ig
    m_new = jnp.maximum(b_last + m, jnp.max(g, axis=-1))
    wk = jnp.exp(g - m_new[..., None])
    cd = jnp.exp(b_last + m - m_new)
    C_new = cd[..., None, None] * C + jnp.einsum('bhs,bhsd,bhsv->bhdv', wk, k, v)
    n_new = cd[..., None] * n + jnp.einsum('bhs,bhsd->bhd', wk, k)
    return (C_new, n_new, m_new), h


def mlstm(qk, v, gif, o, conv_w, conv_b, gate_b, norm_g):
    b, s, _ = v.shape
    out_dtype = v.dtype
    nc = s // ML_CHUNK
    qk = jax.nn.silu(causal_conv(qk, conv_w, conv_b))
    q, k = jnp.split(qk, 2, axis=-1)

    def heads(a, d):
        a = a.reshape(b, nc, ML_CHUNK, ML_HEADS, d)
        return jnp.transpose(a, (1, 0, 3, 2, 4)).astype(jnp.float32)

    qh = heads(q, ML_QK_DIM)
    kh = heads(k, ML_QK_DIM) * (ML_QK_DIM ** -0.5)
    vh = heads(v, ML_V_DIM)
    gates = gif.reshape(b, s, 2, ML_HEADS).astype(jnp.float32) + gate_b.astype(jnp.float32)
    ig = gates[:, :, 0]
    logf = jax.nn.log_sigmoid(gates[:, :, 1])

    def gheads(a):
        return jnp.transpose(a.reshape(b, nc, ML_CHUNK, ML_HEADS), (1, 0, 3, 2))

    init = (jnp.zeros((b, ML_HEADS, ML_QK_DIM, ML_V_DIM), jnp.float32),
            jnp.zeros((b, ML_HEADS, ML_QK_DIM), jnp.float32),
            jnp.zeros((b, ML_HEADS), jnp.float32))
    _, h = lax.scan(mlstm_chunk, init, (qh, kh, vh, gheads(logf), gheads(ig)))
    h = jnp.transpose(h, (1, 0, 3, 2, 4)).reshape(b, s, ML_HEADS, ML_V_DIM)
    h = rmsnorm(h, norm_g).reshape(b, s, ML_HEADS * ML_V_DIM)
    return (h * jax.nn.sigmoid(o.astype(jnp.float32))).astype(out_dtype)


def dsa_attention(q, ckv, iq, ik, iw, kv_norm_g, ik_norm_g, w_uv):
    b, s, _ = q.shape
    topk = min(IDX_TOPK_MAX, s // 4)
    q = q.reshape(b, s, DSA_HEADS, DSA_LATENT)
    c = rmsnorm(ckv, kv_norm_g)
    iq = iq.reshape(b, s, IDX_HEADS, IDX_DIM)
    ik = rmsnorm(ik, ik_norm_g)
    iw = iw.astype(jnp.float32) * (IDX_HEADS ** -0.5)
    kpos = jnp.arange(s)

    def block(args):
        i, qb, iqb, iwb = args
        qpos = i * BLOCK_Q + jnp.arange(BLOCK_Q)
        mask = kpos[None, :] <= qpos[:, None]
        isc = jnp.einsum('bqhd,bsd->bqhs', iqb, ik).astype(jnp.float32) * (IDX_DIM ** -0.5)
        isc = jnp.einsum('bqhs,bqh->bqs', jax.nn.relu(isc), iwb)
        isc = jnp.where(mask[None], isc, -jnp.inf)
        vals, idx = lax.top_k(isc, topk)
        valid = jnp.isfinite(vals)
        c_sel = jax.vmap(lambda cb, ib: cb[ib])(c, idx)
        logits = jnp.einsum('bqhc,bqkc->bqhk', qb, c_sel).astype(jnp.float32) * (DSA_LATENT ** -0.5)
        p = jax.nn.softmax(jnp.where(valid[:, :, None, :], logits, -jnp.inf), axis=-1)
        ob = jnp.einsum('bqhk,bqkc->bqhc', p.astype(c.dtype), c_sel)
        return jnp.einsum('bqhc,hcv->bqhv', ob, w_uv)

    nb = s // BLOCK_Q
    o = lax.map(block, (jnp.arange(nb), to_blocks(q, BLOCK_Q), to_blocks(iq, BLOCK_Q), to_blocks(iw, BLOCK_Q)))
    return from_blocks(o).reshape(b, s, DSA_HEADS * DSA_V_DIM)


def setup_inputs(seed: int = 0) -> dict:
    key = jax.random.key(seed)
    ks = jax.random.split(key, 16)
    nrm = jax.random.normal
    f32 = jnp.float32
    x = nrm(ks[0], (BATCH, SEQ, D_MODEL), f32)
    pre_norm_g = 1.0 + 0.02 * nrm(ks[1], (DEPTH, D_MODEL), f32)
    w_in = nrm(ks[2], (DEPTH, D_MODEL, N_IN), f32) * (D_MODEL ** -0.5)
    da_lambda = 0.1 * nrm(ks[3], (DEPTH, 4, DA_HEAD_DIM), f32)
    da_norm_g = 1.0 + 0.02 * nrm(ks[4], (DEPTH, DA_V_DIM), f32)
    ml_conv_w = nrm(ks[5], (DEPTH, ML_CONV, ML_QK_WIDTH), f32) * (ML_CONV ** -0.5)
    ml_conv_b = 0.02 * nrm(ks[6], (DEPTH, ML_QK_WIDTH), f32)
    base = jnp.stack([jnp.zeros((ML_HEADS,), f32), jnp.linspace(3.0, 6.0, ML_HEADS, dtype=f32)])
    ml_gate_b = base[None] + 0.1 * nrm(ks[7], (DEPTH, 2, ML_HEADS), f32)
    ml_norm_g = 1.0 + 0.02 * nrm(ks[8], (DEPTH, ML_V_DIM), f32)
    dsa_kv_norm_g = 1.0 + 0.02 * nrm(ks[9], (DEPTH, DSA_LATENT), f32)
    dsa_ik_norm_g = 1.0 + 0.02 * nrm(ks[10], (DEPTH, IDX_DIM), f32)
    dsa_w_uv = nrm(ks[11], (DEPTH, DSA_HEADS, DSA_LATENT, DSA_V_DIM), f32) * (DSA_LATENT ** -0.5)
    w_branch = nrm(ks[12], (DEPTH, N_BRANCHES, BRANCH_WIDTH, D_MODEL), f32) * (BRANCH_WIDTH ** -0.5)
    w_out = nrm(ks[13], (DEPTH, D_MODEL, D_MODEL), f32) * (D_MODEL ** -0.5)
    post_norm_g = 1.0 + 0.02 * nrm(ks[14], (DEPTH, D_MODEL), f32)
    return {"x": x, "pre_norm_g": pre_norm_g, "w_in": w_in, "da_lambda": da_lambda,
            "da_norm_g": da_norm_g, "ml_conv_w": ml_conv_w, "ml_conv_b": ml_conv_b,
            "ml_gate_b": ml_gate_b, "ml_norm_g": ml_norm_g, "dsa_kv_norm_g": dsa_kv_norm_g,
            "dsa_ik_norm_g": dsa_ik_norm_g, "dsa_w_uv": dsa_w_uv, "w_branch": w_branch,
            "w_out": w_out, "post_norm_g": post_norm_g}


def reference(x, pre_norm_g, w_in, da_lambda, da_norm_g, ml_conv_w, ml_conv_b, ml_gate_b,
              ml_norm_g, dsa_kv_norm_g, dsa_ik_norm_g, dsa_w_uv, w_branch, w_out, post_norm_g):
    b, s, _ = x.shape
    split_points = [int(p) for p in np.cumsum(IN_SPLITS)[:-1]]
    for l in range(DEPTH):
        h = rmsnorm(x, pre_norm_g[l])
        proj = jnp.einsum('bsd,dn->bsn', h, w_in[l])
        (da_q, da_k, da_v, da_z, ml_qk, ml_v, ml_if, ml_o, ml_z,
         dsa_q, dsa_kv, idx_q, idx_k, idx_w, dsa_z, gates) = jnp.split(proj, split_points, axis=-1)
        lam_init = 0.8 - 0.6 * math.exp(-0.3 * l)
        y_a = diff_attention(da_q, da_k, da_v, da_lambda[l], da_norm_g[l], lam_init) * jax.nn.silu(da_z)
        y_b = mlstm(ml_qk, ml_v, ml_if, ml_o, ml_conv_w[l], ml_conv_b[l], ml_gate_b[l], ml_norm_g[l]) * jax.nn.silu(ml_z)
        y_c = dsa_attention(dsa_q, dsa_kv, idx_q, idx_k, idx_w, dsa_kv_norm_g[l], dsa_ik_norm_g[l], dsa_w_uv[l]) * jax.nn.silu(dsa_z)
        g = jax.nn.sigmoid(gates.reshape(b, s, N_BRANCHES, D_MODEL))
        mixed = (g[:, :, 0] * jnp.einsum('bsw,wd->bsd', y_a, w_branch[l, 0])
                 + g[:, :, 1] * jnp.einsum('bsw,wd->bsd', y_b, w_branch[l, 1])
                 + g[:, :, 2] * jnp.einsum('bsw,wd->bsd', y_c, w_branch[l, 2]))
        out = jnp.einsum('bsd,de->bse', mixed, w_out[l])
        x = x + rmsnorm(out, post_norm_g[l])
    return x
```

```python
import functools
import math

import jax
import jax.numpy as jnp
from jax import lax
from jax.experimental import pallas as pl
from jax.experimental.pallas import tpu as pltpu

F32 = jnp.float32
BF16 = jnp.bfloat16
I32 = jnp.int32

D_MODEL = 2048
DEPTH = 2
BRANCH_WIDTH = 1024
N_BRANCHES = 3
NORM_EPS = 1e-6
DA_HEADS = 8
DA_HEAD_DIM = 64
DA_V_DIM = 128
ML_HEADS = 4
ML_QK_DIM = 128
ML_V_DIM = 256
ML_CONV = 4
ML_QK_WIDTH = 2 * ML_HEADS * ML_QK_DIM
DSA_HEADS = 8
DSA_LATENT = 256
DSA_V_DIM = 128
IDX_HEADS = 16
IDX_DIM = 64
IDX_TOPK_MAX = 256

IN_SPLITS = (
    1024, 1024, 1024, 1024,
    ML_QK_WIDTH, 1024, 2 * ML_HEADS,
    1024, 1024,
    DSA_HEADS * DSA_LATENT, DSA_LATENT,
    IDX_HEADS * IDX_DIM, IDX_DIM, IDX_HEADS,
    1024,
    N_BRANCHES * D_MODEL,
)
(SEG_DA_Q, SEG_DA_K, SEG_DA_V, SEG_DA_Z, SEG_ML_QK, SEG_ML_V, SEG_ML_IF, SEG_ML_O, SEG_ML_Z,
 SEG_DSA_Q, SEG_DSA_KV, SEG_IDX_Q, SEG_IDX_K, SEG_IDX_W, SEG_DSA_Z, SEG_GATES) = range(16)

MAIN_SEGS = (SEG_DA_Q, SEG_DA_K, SEG_DA_V, SEG_DA_Z, SEG_ML_QK, SEG_ML_V, SEG_ML_O, SEG_ML_Z,
             SEG_DSA_Q, SEG_IDX_Q, SEG_DSA_Z, SEG_GATES)
SMALL_SEGS = (SEG_DSA_KV, SEG_IDX_K, SEG_IDX_W, SEG_ML_IF)
SMALL_WIDTH = 384

LANE = 128
VMEM_LIMIT = 56 * 1024 * 1024
INT_MIN = -2147483648
NEG_BIG = -1e30


def _seg_bounds():
    offs = [0]
    for n in IN_SPLITS:
        offs.append(offs[-1] + n)
    return offs


def _main_offsets():
    out, off = {}, 0
    for s in MAIN_SEGS:
        out[s] = off
        off += IN_SPLITS[s]
    return out, off


def _small_offsets():
    out, off = {}, 0
    for s in SMALL_SEGS:
        out[s] = off
        off += IN_SPLITS[s]
    return out, off


MAIN_OFF, MAIN_WIDTH = _main_offsets()
SMALL_OFF, SMALL_USED = _small_offsets()


def _cparams(sem):
    return pltpu.CompilerParams(dimension_semantics=sem, vmem_limit_bytes=VMEM_LIMIT)


def _silu(v):
    return v * (1.0 / (1.0 + jnp.exp(-v)))


def _sigmoid(v):
    return 1.0 / (1.0 + jnp.exp(-v))


def _dot_nt(a, b):
    return lax.dot_general(a, b, (((1,), (1,)), ((), ())), preferred_element_type=F32)


def _proj_kernel(x_ref, g_ref, w_ref, o_ref, h_ref):
    @pl.when(pl.program_id(1) == 0)
    def _():
        x = x_ref[...]
        ms = jnp.mean(x * x, axis=-1, keepdims=True)
        h_ref[...] = (x * lax.rsqrt(ms + NORM_EPS) * g_ref[...]).astype(BF16)

    o_ref[...] = jnp.dot(h_ref[...], w_ref[...], preferred_element_type=F32).astype(o_ref.dtype)


def _project(x2, g, w, out_dtype, tm, tn):
    m, d = x2.shape
    n = w.shape[1]
    return pl.pallas_call(
        _proj_kernel,
        grid=(m // tm, n // tn),
        in_specs=[pl.BlockSpec((tm, d), lambda i, j: (i, 0)),
                  pl.BlockSpec((1, d), lambda i, j: (0, 0)),
                  pl.BlockSpec((d, tn), lambda i, j: (0, j))],
        out_specs=pl.BlockSpec((tm, tn), lambda i, j: (i, j)),
        out_shape=jax.ShapeDtypeStruct((m, n), out_dtype),
        scratch_shapes=[pltpu.VMEM((tm, d), BF16)],
        compiler_params=_cparams(("parallel", "arbitrary")),
        name="rmsnorm_in_proj",
    )(x2, g.reshape(1, d), w)


def _da_kernel(lam_ref, q_ref, k_ref, v_ref, z_ref, g_ref, o_ref, m_ref, l_ref, acc_ref,
               *, t, lam_init):
    qi = pl.program_id(2)
    ki = pl.program_id(3)

    @pl.when(ki == 0)
    def _():
        m_ref[...] = jnp.full(m_ref.shape, -jnp.inf, F32)
        l_ref[...] = jnp.zeros(l_ref.shape, F32)
        acc_ref[...] = jnp.zeros(acc_ref.shape, F32)

    def step(masked):
        q = q_ref[...] * (DA_HEAD_DIM ** -0.5)
        k = k_ref[...]
        v = v_ref[...]
        if masked:
            row = lax.broadcasted_iota(I32, (t, t), 0)
            col = lax.broadcasted_iota(I32, (t, t), 1)
            keep = col <= row
        for c in range(2):
            s = _dot_nt(q[:, c * DA_HEAD_DIM:(c + 1) * DA_HEAD_DIM],
                        k[:, c * DA_HEAD_DIM:(c + 1) * DA_HEAD_DIM])
            if masked:
                s = jnp.where(keep, s, -jnp.inf)
            m_prev = m_ref[c]
            m_new = jnp.maximum(m_prev, jnp.max(s, axis=-1, keepdims=True))
            alpha = jnp.exp(m_prev - m_new)
            p = jnp.exp(s - m_new)
            l_ref[c] = alpha * l_ref[c] + jnp.sum(p, axis=-1, keepdims=True)
            acc_ref[c] = alpha * acc_ref[c] + jnp.dot(p.astype(BF16), v, preferred_element_type=F32)
            m_ref[c] = m_new

    @pl.when(ki < qi)
    def _():
        step(False)

    @pl.when(ki == qi)
    def _():
        step(True)
        lp = lam_ref[...]
        lam = (jnp.exp(jnp.sum(lp[0:1] * lp[1:2], axis=-1, keepdims=True))
               - jnp.exp(jnp.sum(lp[2:3] * lp[3:4], axis=-1, keepdims=True)) + lam_init)
        o = acc_ref[0] / l_ref[0] - lam * (acc_ref[1] / l_ref[1])
        ms = jnp.mean(o * o, axis=-1, keepdims=True)
        o = o * lax.rsqrt(ms + NORM_EPS) * g_ref[...] * (1.0 - lam_init)
        o_ref[...] = (o * _silu(z_ref[...].astype(F32))).astype(o_ref.dtype)


def _diff_attention(p_main, lam_params, norm_g, lam_init, b, s, t):
    m = b * s
    nq = s // t
    cq = MAIN_OFF[SEG_DA_Q] // LANE
    ck = MAIN_OFF[SEG_DA_K] // LANE
    cv = MAIN_OFF[SEG_DA_V] // LANE
    cz = MAIN_OFF[SEG_DA_Z] // LANE
    kern = functools.partial(_da_kernel, t=t, lam_init=lam_init)
    return pl.pallas_call(
        kern,
        grid=(b, DA_HEADS, nq, nq),
        in_specs=[
            pl.BlockSpec((4, DA_HEAD_DIM), lambda bi, h, qi, ki: (0, 0)),
            pl.BlockSpec((t, LANE), lambda bi, h, qi, ki: (bi * nq + qi, cq + h)),
            pl.BlockSpec((t, LANE), lambda bi, h, qi, ki: (bi * nq + jnp.minimum(ki, qi), ck + h)),
            pl.BlockSpec((t, LANE), lambda bi, h, qi, ki: (bi * nq + jnp.minimum(ki, qi), cv + h)),
            pl.BlockSpec((t, LANE), lambda bi, h, qi, ki: (bi * nq + qi, cz + h)),
            pl.BlockSpec((1, DA_V_DIM), lambda bi, h, qi, ki: (0, 0)),
        ],
        out_specs=pl.BlockSpec((t, LANE), lambda bi, h, qi, ki: (bi * nq + qi, h)),
        out_shape=jax.ShapeDtypeStruct((m, DA_HEADS * DA_V_DIM), BF16),
        scratch_shapes=[pltpu.VMEM((2, t, 1), F32), pltpu.VMEM((2, t, 1), F32),
                        pltpu.VMEM((2, t, DA_V_DIM), F32)],
        compiler_params=_cparams(("parallel", "parallel", "parallel", "arbitrary")),
        name="diff_attention",
    )(lam_params, p_main, p_main, p_main, p_main, norm_g.reshape(1, DA_V_DIM))


def _split3(v):
    hi = v.astype(BF16)
    r1 = v - hi.astype(F32)
    mid = r1.astype(BF16)
    lo = (r1 - mid.astype(F32)).astype(BF16)
    return hi, mid, lo


def _log_sigmoid(v):
    return jnp.minimum(v, 0.0) - jnp.log(1.0 + jnp.exp(-jnp.abs(v)))


def _ml_kernel(qk_ref, v_ref, o_ref, z_ref, gif_ref, gift_ref, cw_ref, cb_ref, gb_ref, gbt_ref, ng_ref,
               y_ref, xbuf, c_st, n_st, m_st, *, L):
    ci = pl.program_id(1)
    pad = 8

    @pl.when(ci == 0)
    def _():
        xbuf[0:pad, :] = jnp.zeros((pad, ML_QK_WIDTH), F32)
        c_st[...] = jnp.zeros(c_st.shape, F32)
        n_st[...] = jnp.zeros(n_st.shape, F32)
        m_st[...] = jnp.zeros(m_st.shape, F32)

    xbuf[pad:pad + L, :] = qk_ref[...].astype(F32)
    cw = cw_ref[...]
    y = xbuf[pad:pad + L, :] * cw[ML_CONV - 1:ML_CONV, :]
    for j in range(ML_CONV - 1):
        sh = ML_CONV - 1 - j
        y = y + xbuf[pad - sh:pad - sh + L, :] * cw[j:j + 1, :]
    y = _silu(y + cb_ref[...])
    xbuf[0:pad, :] = xbuf[L:L + pad, :]

    g_c = gif_ref[...] + gb_ref[...]
    g_r = gift_ref[0] + gbt_ref[...]
    lf_c = _log_sigmoid(g_c)
    lf_r = _log_sigmoid(g_r)
    ti = lax.broadcasted_iota(I32, (L, L), 0)
    si = lax.broadcasted_iota(I32, (L, L), 1)
    causal = si <= ti
    tri = jnp.where(causal, 1.0, 0.0).astype(BF16)
    triu = jnp.where(ti <= si, 1.0, 0.0).astype(BF16)
    bc_all = sum(jnp.dot(tri, piece, preferred_element_type=F32) for piece in _split3(lf_c))
    br_all = sum(jnp.dot(piece, triu, preferred_element_type=F32) for piece in _split3(lf_r))

    for h in range(ML_HEADS):
        q = y[:, h * ML_QK_DIM:(h + 1) * ML_QK_DIM].astype(BF16)
        kf = y[:, ML_HEADS * ML_QK_DIM + h * ML_QK_DIM:ML_HEADS * ML_QK_DIM + (h + 1) * ML_QK_DIM] * (ML_QK_DIM ** -0.5)
        k = kf.astype(BF16)
        v = v_ref[:, h * ML_V_DIM:(h + 1) * ML_V_DIM]
        bc = bc_all[:, ML_HEADS + h:ML_HEADS + h + 1]
        br = br_all[ML_HEADS + h:ML_HEADS + h + 1, :]
        ig_c = g_c[:, h:h + 1]
        ig_r = g_r[h:h + 1, :]
        m_prev = m_st[h][:, 0:1]
        c_prev = c_st[h]
        n_prev = n_st[h]

        dmat = jnp.where(causal, bc - br + ig_r, -jnp.inf)
        inter = bc + m_prev
        m_t = jnp.maximum(jnp.max(dmat, axis=-1, keepdims=True), inter)
        w = jnp.exp(dmat - m_t) * _dot_nt(q, k)
        decay = jnp.exp(inter - m_t)
        num = (jnp.dot(w.astype(BF16), v, preferred_element_type=F32)
               + decay * jnp.dot(q, c_prev.astype(BF16), preferred_element_type=F32))
        qn = jnp.sum(q.astype(F32) * n_prev.astype(BF16).astype(F32), axis=-1, keepdims=True)
        den = jnp.sum(w, axis=-1, keepdims=True) + decay * qn
        hh = num / jnp.maximum(jnp.abs(den), jnp.exp(-m_t))

        b_last = br[:, L - 1:L]
        g_row = b_last - br + ig_r
        m_new = jnp.maximum(b_last + m_prev, jnp.max(g_row, axis=-1, keepdims=True))
        wk_c = jnp.exp(b_last - bc + ig_c - m_new)
        cd = jnp.exp(b_last + m_prev - m_new)
        kw = (kf * wk_c)
        c_st[h] = cd * c_prev + lax.dot_general(kw.astype(BF16), v, (((0,), (0,)), ((), ())),
                                                preferred_element_type=F32)
        n_st[h] = cd * n_prev + jnp.sum(kw, axis=0, keepdims=True)
        m_st[h] = jnp.broadcast_to(m_new, (1, LANE))

        ms = jnp.mean(hh * hh, axis=-1, keepdims=True)
        hn = hh * lax.rsqrt(ms + NORM_EPS) * ng_ref[...]
        sl = slice(h * ML_V_DIM, (h + 1) * ML_V_DIM)
        out = hn * _sigmoid(o_ref[:, sl].astype(F32)) * _silu(z_ref[:, sl].astype(F32))
        y_ref[:, sl] = out.astype(y_ref.dtype)


def _mlstm(p_main, gif, gif_t, conv_w, conv_b, gate_b, norm_g, b, s, L):
    m = b * s
    nc = s // L
    w1k = 1024
    c_qk = MAIN_OFF[SEG_ML_QK] // w1k
    c_v = MAIN_OFF[SEG_ML_V] // w1k
    c_o = MAIN_OFF[SEG_ML_O] // w1k
    c_z = MAIN_OFF[SEG_ML_Z] // w1k
    gb = gate_b.reshape(1, 2 * ML_HEADS)
    kern = functools.partial(_ml_kernel, L=L)
    full = lambda shp: pl.BlockSpec(shp, lambda bi, ci: (0,) * len(shp))
    return pl.pallas_call(
        kern,
        grid=(b, nc),
        in_specs=[
            pl.BlockSpec((L, w1k), lambda bi, ci: (bi * nc + ci, c_qk)),
            pl.BlockSpec((L, w1k), lambda bi, ci: (bi * nc + ci, c_v)),
            pl.BlockSpec((L, w1k), lambda bi, ci: (bi * nc + ci, c_o)),
            pl.BlockSpec((L, w1k), lambda bi, ci: (bi * nc + ci, c_z)),
            pl.BlockSpec((L, 2 * ML_HEADS), lambda bi, ci: (bi * nc + ci, 0)),
            pl.BlockSpec((1, 2 * ML_HEADS, L), lambda bi, ci: (bi, 0, ci)),
            full((ML_CONV, ML_QK_WIDTH)),
            full((1, ML_QK_WIDTH)),
            full((1, 2 * ML_HEADS)),
            full((2 * ML_HEADS, 1)),
            full((1, ML_V_DIM)),
        ],
        out_specs=pl.BlockSpec((L, w1k), lambda bi, ci: (bi * nc + ci, 0)),
        out_shape=jax.ShapeDtypeStruct((m, ML_HEADS * ML_V_DIM), BF16),
        scratch_shapes=[pltpu.VMEM((L + 8, ML_QK_WIDTH), F32),
                        pltpu.VMEM((ML_HEADS, ML_QK_DIM, ML_V_DIM), F32),
                        pltpu.VMEM((ML_HEADS, 1, ML_QK_DIM), F32),
                        pltpu.VMEM((ML_HEADS, 1, LANE), F32)],
        compiler_params=_cparams(("parallel", "arbitrary")),
        name="mlstm",
    )(p_main, p_main, p_main, p_main, gif, gif_t, conv_w, conv_b.reshape(1, ML_QK_WIDTH),
      gb, gb.reshape(2 * ML_HEADS, 1), norm_g.reshape(1, ML_V_DIM))


def _dsa_norm_kernel(p_ref, gkv_ref, gik_ref, c_ref, ik_ref):
    o_kv = SMALL_OFF[SEG_DSA_KV]
    o_ik = SMALL_OFF[SEG_IDX_K]
    ckv = p_ref[:, o_kv:o_kv + DSA_LATENT]
    ms = jnp.mean(ckv * ckv, axis=-1, keepdims=True)
    c_ref[...] = (ckv * lax.rsqrt(ms + NORM_EPS) * gkv_ref[...]).astype(c_ref.dtype)
    ik = p_ref[:, o_ik:o_ik + IDX_DIM]
    ms = jnp.mean(ik * ik, axis=-1, keepdims=True)
    ik_ref[...] = (ik * lax.rsqrt(ms + NORM_EPS) * gik_ref[...]).astype(ik_ref.dtype)


def _dsa_norms(p_small, g_kv, g_ik, tm):
    m = p_small.shape[0]
    return pl.pallas_call(
        _dsa_norm_kernel,
        grid=(m // tm,),
        in_specs=[pl.BlockSpec((tm, SMALL_WIDTH), lambda i: (i, 0)),
                  pl.BlockSpec((1, DSA_LATENT), lambda i: (0, 0)),
                  pl.BlockSpec((1, IDX_DIM), lambda i: (0, 0))],
        out_specs=[pl.BlockSpec((tm, DSA_LATENT), lambda i: (i, 0)),
                   pl.BlockSpec((tm, IDX_DIM), lambda i: (i, 0))],
        out_shape=[jax.ShapeDtypeStruct((m, DSA_LATENT), BF16),
                   jax.ShapeDtypeStruct((m, IDX_DIM), BF16)],
        compiler_params=_cparams(("parallel",)),
        name="dsa_norms",
    )(p_small, g_kv.reshape(1, DSA_LATENT), g_ik.reshape(1, IDX_DIM))


def _dsa_kernel(q_ref, iq_ref, iw_ref, z_ref, c_ref, ik_ref, wuv_ref, o_ref,
                key_ref, m_ref, l_ref, acc_ref, *, tq, tk, topk, nbits_col):
    qi = pl.program_id(1)
    nkc = (qi * tq + tq + tk - 1) // tk
    row = qi * tq + lax.broadcasted_iota(I32, (tq, tk), 0)
    col0 = lax.broadcasted_iota(I32, (tq, tk), 1)
    nlb = tk // LANE

    iw = iw_ref[...] * ((IDX_HEADS ** -0.5) * (IDX_DIM ** -0.5))
    iq = iq_ref[...]

    def score_body(kc, carry):
        off = pl.multiple_of(kc * tk, tk)
        ikc = ik_ref[pl.ds(off, tk), :]
        acc = jnp.zeros((tq, tk), F32)
        for h in range(IDX_HEADS):
            sc = _dot_nt(iq[:, h * IDX_DIM:(h + 1) * IDX_DIM], ikc)
            acc = acc + jnp.maximum(sc, 0.0) * iw[:, h:h + 1]
        bits = lax.bitcast_convert_type(acc, I32)
        skey = jnp.where(bits < 0, (bits ^ 0x7FFFFFFF) + 1, bits)
        key_ref[:, pl.ds(off, tk)] = jnp.where(col0 + off <= row, skey, INT_MIN)
        return carry

    lax.fori_loop(0, nkc, score_body, 0)

    def lane_fold(x):
        part = x[:, 0:LANE]
        for j in range(1, nlb):
            part = part + x[:, j * LANE:(j + 1) * LANE]
        return part

    def count(pred):
        def body(kc, cnt):
            off = pl.multiple_of(kc * tk, tk)
            hit = pred(key_ref[:, pl.ds(off, tk)], col0 + off)
            return cnt + lane_fold(jnp.where(hit, 1, 0))
        cnt = lax.fori_loop(0, nkc, body, jnp.zeros((tq, LANE), I32))
        return jnp.sum(cnt, axis=-1, keepdims=True)

    def bis_body(it, t_u):
        cand_u = t_u | jnp.left_shift(jnp.int32(1), 31 - it)
        cand_s = cand_u ^ INT_MIN
        cnt = count(lambda k, c: k >= cand_s)
        return jnp.where(cnt >= topk, cand_u, t_u)

    t_u = lax.fori_loop(0, 32, bis_body, jnp.zeros((tq, 1), I32))
    thr = jnp.maximum(t_u ^ INT_MIN, INT_MIN + 1)
    cnt_ge = count(lambda k, c: k >= thr)
    has_ties = jnp.max(jnp.where(cnt_ge > topk, 1, 0)) > 0

    q = q_ref[...] * (DSA_LATENT ** -0.5)

    def attend(tie_cut):
        m_ref[...] = jnp.full(m_ref.shape, NEG_BIG, F32)
        l_ref[...] = jnp.zeros(l_ref.shape, F32)
        acc_ref[...] = jnp.zeros(acc_ref.shape, F32)

        def body(kc, carry):
            off = pl.multiple_of(kc * tk, tk)
            keys = key_ref[:, pl.ds(off, tk)]
            if tie_cut is None:
                sel = keys >= thr
            else:
                sel = (keys > thr) | ((keys == thr) & (col0 + off < tie_cut))
            cc = c_ref[pl.ds(off, tk), :]
            for h in range(DSA_HEADS):
                s = _dot_nt(q[:, h * DSA_LATENT:(h + 1) * DSA_LATENT], cc)
                s = jnp.where(sel, s, NEG_BIG)
                m_prev = m_ref[h]
                m_new = jnp.maximum(m_prev, jnp.max(s, axis=-1, keepdims=True))
                alpha = jnp.exp(m_prev - m_new)
                p = jnp.exp(s - m_new)
                l_ref[h] = alpha * l_ref[h] + jnp.sum(p, axis=-1, keepdims=True)
                acc_ref[h] = alpha * acc_ref[h] + jnp.dot(p.astype(BF16), cc, preferred_element_type=F32)
                m_ref[h] = m_new
            return carry

        lax.fori_loop(0, nkc, body, 0)
        z = z_ref[...].astype(F32)
        for h in range(DSA_HEADS):
            ob = (acc_ref[h] / l_ref[h]).astype(BF16)
            oh = jnp.dot(ob, wuv_ref[h], preferred_element_type=F32)
            sl = slice(h * DSA_V_DIM, (h + 1) * DSA_V_DIM)
            o_ref[:, sl] = (oh * _silu(z[:, sl])).astype(o_ref.dtype)

    @pl.when(jnp.logical_not(has_ties))
    def _():
        attend(None)

    @pl.when(has_ties)
    def _():
        need = topk - count(lambda k, c: k > thr)

        def cut_body(it, cut):
            cand = cut | jnp.left_shift(jnp.int32(1), nbits_col - 1 - it)
            cnt = count(lambda k, c: (k == thr) & (c < cand))
            return jnp.where(cnt < need, cand, cut)

        cut = lax.fori_loop(0, nbits_col, cut_body, jnp.zeros((tq, 1), I32))
        attend(cut + 1)


def _dsa_attention(p_main, iw, c_lat, ik_n, w_uv, b, s, tq, tk):
    m = b * s
    nq = s // tq
    topk = min(IDX_TOPK_MAX, s // 4)
    c_q = MAIN_OFF[SEG_DSA_Q] // (DSA_HEADS * DSA_LATENT)
    c_iq = MAIN_OFF[SEG_IDX_Q] // (IDX_HEADS * IDX_DIM)
    c_z = MAIN_OFF[SEG_DSA_Z] // 1024
    nbits_col = max(1, int(math.ceil(math.log2(s))) + 1)
    kern = functools.partial(_dsa_kernel, tq=tq, tk=tk, topk=topk, nbits_col=nbits_col)
    return pl.pallas_call(
        kern,
        grid=(b, nq),
        in_specs=[
            pl.BlockSpec((tq, DSA_HEADS * DSA_LATENT), lambda bi, qi: (bi * nq + qi, c_q)),
            pl.BlockSpec((tq, IDX_HEADS * IDX_DIM), lambda bi, qi: (bi * nq + qi, c_iq)),
            pl.BlockSpec((tq, IDX_HEADS), lambda bi, qi: (bi * nq + qi, 0)),
            pl.BlockSpec((tq, 1024), lambda bi, qi: (bi * nq + qi, c_z)),
            pl.BlockSpec((s, DSA_LATENT), lambda bi, qi: (bi, 0)),
            pl.BlockSpec((s, IDX_DIM), lambda bi, qi: (bi, 0)),
            pl.BlockSpec((DSA_HEADS, DSA_LATENT, DSA_V_DIM), lambda bi, qi: (0, 0, 0)),
        ],
        out_specs=pl.BlockSpec((tq, DSA_HEADS * DSA_V_DIM), lambda bi, qi: (bi * nq + qi, 0)),
        out_shape=jax.ShapeDtypeStruct((m, DSA_HEADS * DSA_V_DIM), BF16),
        scratch_shapes=[pltpu.VMEM((tq, s), I32),
                        pltpu.VMEM((DSA_HEADS, tq, 1), F32),
                        pltpu.VMEM((DSA_HEADS, tq, 1), F32),
                        pltpu.VMEM((DSA_HEADS, tq, DSA_LATENT), F32)],
        compiler_params=_cparams(("parallel", "arbitrary")),
        name="dsa_attention",
    )(p_main, p_main, iw, p_main, c_lat, ik_n, w_uv)


def _out_kernel(x_ref, ya_ref, yb_ref, yc_ref, ga_ref, gb_ref, gc_ref, wb_ref, wo_ref, pg_ref, o_ref):
    mixed = None
    for i, (y_ref, g_ref) in enumerate(((ya_ref, ga_ref), (yb_ref, gb_ref), (yc_ref, gc_ref))):
        t = _sigmoid(g_ref[...].astype(F32)) * jnp.dot(y_ref[...], wb_ref[i], preferred_element_type=F32)
        mixed = t if mixed is None else mixed + t
    out = jnp.dot(mixed.astype(BF16), wo_ref[...], preferred_element_type=F32)
    ms = jnp.mean(out * out, axis=-1, keepdims=True)
    o_ref[...] = x_ref[...] + out * lax.rsqrt(ms + NORM_EPS) * pg_ref[...]


def _merge_out(x2, ya, yb, yc, p_main, w_branch, w_out, post_g, tm):
    m, d = x2.shape
    cg = MAIN_OFF[SEG_GATES] // d
    row = lambda i: (i, 0)
    return pl.pallas_call(
        _out_kernel,
        grid=(m // tm,),
        in_specs=[
            pl.BlockSpec((tm, d), row),
            pl.BlockSpec((tm, BRANCH_WIDTH), row),
            pl.BlockSpec((tm, BRANCH_WIDTH), row),
            pl.BlockSpec((tm, BRANCH_WIDTH), row),
            pl.BlockSpec((tm, d), lambda i: (i, cg)),
            pl.BlockSpec((tm, d), lambda i: (i, cg + 1)),
            pl.BlockSpec((tm, d), lambda i: (i, cg + 2)),
            pl.BlockSpec((N_BRANCHES, BRANCH_WIDTH, d), lambda i: (0, 0, 0), pipeline_mode=pl.Buffered(1)),
            pl.BlockSpec((d, d), lambda i: (0, 0), pipeline_mode=pl.Buffered(1)),
            pl.BlockSpec((1, d), lambda i: (0, 0)),
        ],
        out_specs=pl.BlockSpec((tm, d), row),
        out_shape=jax.ShapeDtypeStruct((m, d), F32),
        compiler_params=_cparams(("parallel",)),
        name="merge_out_proj",
    )(x2, ya, yb, yc, p_main, p_main, p_main, w_branch, w_out, post_g.reshape(1, d))


def _pick(total, want):
    t = min(total, want)
    while total % t:
        t //= 2
    return t


def kernel(x, pre_norm_g, w_in, da_lambda, da_norm_g, ml_conv_w, ml_conv_b, ml_gate_b, ml_norm_g,
           dsa_kv_norm_g, dsa_ik_norm_g, dsa_w_uv, w_branch, w_out, post_norm_g):
    b, s, d = x.shape
    m = b * s
    bounds = _seg_bounds()
    x2 = x.reshape(m, d)
    tm_proj = _pick(m, 1024)
    t_da = _pick(s, 512)
    l_ml = _pick(s, 256)
    tq_dsa = _pick(s, 128)
    tk_dsa = _pick(s, 512)
    o_if = SMALL_OFF[SEG_ML_IF]
    o_iw = SMALL_OFF[SEG_IDX_W]
    for l in range(DEPTH):
        w = w_in[l]
        w_main = jnp.concatenate([w[:, bounds[sg]:bounds[sg + 1]] for sg in MAIN_SEGS], axis=1).astype(BF16)
        w_small = jnp.concatenate([w[:, bounds[sg]:bounds[sg + 1]] for sg in SMALL_SEGS]
                                  + [jnp.zeros((d, SMALL_WIDTH - SMALL_USED), w.dtype)], axis=1).astype(BF16)
        p_main = _project(x2, pre_norm_g[l], w_main, BF16, tm_proj, 512)
        p_small = _project(x2, pre_norm_g[l], w_small, F32, tm_proj, SMALL_WIDTH)

        lam_init = 0.8 - 0.6 * math.exp(-0.3 * l)
        y_a = _diff_attention(p_main, da_lambda[l], da_norm_g[l], lam_init, b, s, t_da)

        gif = p_small[:, o_if:o_if + 2 * ML_HEADS]
        gif_t = jnp.transpose(gif.reshape(b, s, 2 * ML_HEADS), (0, 2, 1))
        y_b = _mlstm(p_main, gif, gif_t, ml_conv_w[l], ml_conv_b[l], ml_gate_b[l], ml_norm_g[l], b, s, l_ml)

        c_lat, ik_n = _dsa_norms(p_small, dsa_kv_norm_g[l], dsa_ik_norm_g[l], _pick(m, 2048))
        iw = p_small[:, o_iw:o_iw + IDX_HEADS]
        y_c = _dsa_attention(p_main, iw, c_lat, ik_n, dsa_w_uv[l].astype(BF16), b, s, tq_dsa, tk_dsa)

        x2 = _merge_out(x2, y_a, y_b, y_c, p_main, w_branch[l].astype(BF16), w_out[l].astype(BF16),
                        post_norm_g[l], _pick(m, 256))
    return x2.reshape(b, s, d)
```

```python
import functools
import math

import jax
import jax.numpy as jnp
from jax import lax
from jax.experimental import pallas as pl
from jax.experimental.pallas import tpu as pltpu

F32 = jnp.float32
BF16 = jnp.bfloat16
I32 = jnp.int32

D_MODEL = 2048
DEPTH = 2
BRANCH_WIDTH = 1024
N_BRANCHES = 3
NORM_EPS = 1e-6
DA_HEADS = 8
DA_HEAD_DIM = 64
DA_V_DIM = 128
ML_HEADS = 4
ML_QK_DIM = 128
ML_V_DIM = 256
ML_CONV = 4
ML_QK_WIDTH = 2 * ML_HEADS * ML_QK_DIM
DSA_HEADS = 8
DSA_LATENT = 256
DSA_V_DIM = 128
IDX_HEADS = 16
IDX_DIM = 64
IDX_TOPK_MAX = 256

IN_SPLITS = (
    1024, 1024, 1024, 1024,
    ML_QK_WIDTH, 1024, 2 * ML_HEADS,
    1024, 1024,
    DSA_HEADS * DSA_LATENT, DSA_LATENT,
    IDX_HEADS * IDX_DIM, IDX_DIM, IDX_HEADS,
    1024,
    N_BRANCHES * D_MODEL,
)
(SEG_DA_Q, SEG_DA_K, SEG_DA_V, SEG_DA_Z, SEG_ML_QK, SEG_ML_V, SEG_ML_IF, SEG_ML_O, SEG_ML_Z,
 SEG_DSA_Q, SEG_DSA_KV, SEG_IDX_Q, SEG_IDX_K, SEG_IDX_W, SEG_DSA_Z, SEG_GATES) = range(16)

MAIN_SEGS = (SEG_DA_Q, SEG_DA_K, SEG_DA_V, SEG_DA_Z, SEG_ML_QK, SEG_ML_V, SEG_ML_O, SEG_ML_Z,
             SEG_DSA_Q, SEG_IDX_Q, SEG_DSA_Z, SEG_GATES)
SMALL_SEGS = (SEG_DSA_KV, SEG_IDX_K, SEG_IDX_W, SEG_ML_IF)
SMALL_WIDTH = 384

LANE = 128
VMEM_LIMIT = 56 * 1024 * 1024
INT_MIN = -2147483648
NEG_BIG = -1e30
LOG2E = math.log2(math.e)


def _seg_bounds():
    offs = [0]
    for n in IN_SPLITS:
        offs.append(offs[-1] + n)
    return offs


def _main_offsets():
    out, off = {}, 0
    for s in MAIN_SEGS:
        out[s] = off
        off += IN_SPLITS[s]
    return out, off


def _small_offsets():
    out, off = {}, 0
    for s in SMALL_SEGS:
        out[s] = off
        off += IN_SPLITS[s]
    return out, off


MAIN_OFF, MAIN_WIDTH = _main_offsets()
SMALL_OFF, SMALL_USED = _small_offsets()


def _cparams(sem):
    return pltpu.CompilerParams(dimension_semantics=sem, vmem_limit_bytes=VMEM_LIMIT)


def _silu(v):
    return v * (1.0 / (1.0 + jnp.exp(-v)))


def _sigmoid(v):
    return 1.0 / (1.0 + jnp.exp(-v))


def _dot_nt(a, b):
    return lax.dot_general(a, b, (((1,), (1,)), ((), ())), preferred_element_type=F32)


def _proj_kernel(x_ref, g_ref, w_ref, o_ref, h_ref):
    @pl.when(pl.program_id(1) == 0)
    def _():
        x = x_ref[...]
        ms = jnp.mean(x * x, axis=-1, keepdims=True)
        h_ref[...] = (x * lax.rsqrt(ms + NORM_EPS) * g_ref[...]).astype(BF16)

    o_ref[...] = jnp.dot(h_ref[...], w_ref[...], preferred_element_type=F32).astype(o_ref.dtype)


def _project(x2, g, w, out_dtype, tm, tn):
    m, d = x2.shape
    n = w.shape[1]
    return pl.pallas_call(
        _proj_kernel,
        grid=(m // tm, n // tn),
        in_specs=[pl.BlockSpec((tm, d), lambda i, j: (i, 0)),
                  pl.BlockSpec((1, d), lambda i, j: (0, 0)),
                  pl.BlockSpec((d, tn), lambda i, j: (0, j))],
        out_specs=pl.BlockSpec((tm, tn), lambda i, j: (i, j)),
        out_shape=jax.ShapeDtypeStruct((m, n), out_dtype),
        scratch_shapes=[pltpu.VMEM((tm, d), BF16)],
        compiler_params=_cparams(("parallel", "arbitrary")),
        name="rmsnorm_in_proj",
    )(x2, g.reshape(1, d), w)


def _da_kernel(lam_ref, q_ref, k_ref, v_ref, z_ref, g_ref, o_ref, m_ref, l_ref, acc_ref,
               *, t, lam_init):
    qi = pl.program_id(2)
    nrep = t // LANE
    m_ref[...] = jnp.full(m_ref.shape, -jnp.inf, F32)
    l_ref[...] = jnp.zeros(l_ref.shape, F32)
    acc_ref[...] = jnp.zeros(acc_ref.shape, F32)
    q = q_ref[...]

    def block(off, masked):
        k = k_ref[pl.ds(off, t), :]
        v = v_ref[pl.ds(off, t), :]
        if masked:
            row = lax.broadcasted_iota(I32, (t, t), 0)
            col = lax.broadcasted_iota(I32, (t, t), 1)
            keep = col <= row
        for c in range(2):
            s = _dot_nt(q[:, c * DA_HEAD_DIM:(c + 1) * DA_HEAD_DIM],
                        k[:, c * DA_HEAD_DIM:(c + 1) * DA_HEAD_DIM])
            if masked:
                s = jnp.where(keep, s, -jnp.inf)
            m_prev = m_ref[c]
            m_new = jnp.maximum(m_prev, jnp.max(s, axis=-1, keepdims=True))
            alpha = jnp.exp2(m_prev - m_new)
            p = jnp.exp2(s - jnp.tile(m_new, (1, nrep)))
            l_ref[c] = alpha * l_ref[c] + jnp.sum(p, axis=-1, keepdims=True)
            acc_ref[c] = alpha * acc_ref[c] + jnp.dot(p.astype(BF16), v, preferred_element_type=F32)
            m_ref[c] = m_new

    def body(ki, carry):
        block(pl.multiple_of(ki * t, t), False)
        return carry

    lax.fori_loop(0, qi, body, 0)
    block(pl.multiple_of(qi * t, t), True)

    lp = lam_ref[...]
    lam = (jnp.exp(jnp.sum(lp[0:1] * lp[1:2], axis=-1, keepdims=True))
           - jnp.exp(jnp.sum(lp[2:3] * lp[3:4], axis=-1, keepdims=True)) + lam_init)
    o = acc_ref[0] / l_ref[0] - lam * (acc_ref[1] / l_ref[1])
    ms = jnp.mean(o * o, axis=-1, keepdims=True)
    o = o * lax.rsqrt(ms + NORM_EPS) * g_ref[...] * (1.0 - lam_init)
    o_ref[...] = (o * _silu(z_ref[...].astype(F32))).astype(o_ref.dtype)


def _diff_attention(p_main, lam_params, norm_g, lam_init, b, s, t):
    m = b * s
    nq = s // t
    cq = MAIN_OFF[SEG_DA_Q] // LANE
    ck = MAIN_OFF[SEG_DA_K] // LANE
    cv = MAIN_OFF[SEG_DA_V] // LANE
    cz = MAIN_OFF[SEG_DA_Z] // LANE
    kern = functools.partial(_da_kernel, t=t, lam_init=lam_init)
    return pl.pallas_call(
        kern,
        grid=(b, DA_HEADS, nq),
        in_specs=[
            pl.BlockSpec((4, DA_HEAD_DIM), lambda bi, h, qi: (0, 0)),
            pl.BlockSpec((t, LANE), lambda bi, h, qi: (bi * nq + qi, cq + h)),
            pl.BlockSpec((s, LANE), lambda bi, h, qi: (bi, ck + h)),
            pl.BlockSpec((s, LANE), lambda bi, h, qi: (bi, cv + h)),
            pl.BlockSpec((t, LANE), lambda bi, h, qi: (bi * nq + qi, cz + h)),
            pl.BlockSpec((1, DA_V_DIM), lambda bi, h, qi: (0, 0)),
        ],
        out_specs=pl.BlockSpec((t, LANE), lambda bi, h, qi: (bi * nq + qi, h)),
        out_shape=jax.ShapeDtypeStruct((m, DA_HEADS * DA_V_DIM), BF16),
        scratch_shapes=[pltpu.VMEM((2, t, LANE), F32), pltpu.VMEM((2, t, LANE), F32),
                        pltpu.VMEM((2, t, DA_V_DIM), F32)],
        compiler_params=_cparams(("parallel", "parallel", "arbitrary")),
        name="diff_attention",
    )(lam_params, p_main, p_main, p_main, p_main, norm_g.reshape(1, DA_V_DIM))


def _split3(v):
    hi = v.astype(BF16)
    r1 = v - hi.astype(F32)
    mid = r1.astype(BF16)
    lo = (r1 - mid.astype(F32)).astype(BF16)
    return hi, mid, lo


def _log_sigmoid(v):
    return jnp.minimum(v, 0.0) - jnp.log(1.0 + jnp.exp(-jnp.abs(v)))


def _ml_kernel(qk_ref, v_ref, o_ref, z_ref, gif_ref, gift_ref, cw_ref, cb_ref, gb_ref, gbt_ref, ng_ref,
               y_ref, xbuf, c_st, n_st, m_st, *, L):
    ci = pl.program_id(1)
    pad = 8

    @pl.when(ci == 0)
    def _():
        xbuf[0:pad, :] = jnp.zeros((pad, ML_QK_WIDTH), F32)
        c_st[...] = jnp.zeros(c_st.shape, F32)
        n_st[...] = jnp.zeros(n_st.shape, F32)
        m_st[...] = jnp.zeros(m_st.shape, F32)

    xbuf[pad:pad + L, :] = qk_ref[...].astype(F32)
    cw = cw_ref[...]
    y = xbuf[pad:pad + L, :] * cw[ML_CONV - 1:ML_CONV, :]
    for j in range(ML_CONV - 1):
        sh = ML_CONV - 1 - j
        y = y + xbuf[pad - sh:pad - sh + L, :] * cw[j:j + 1, :]
    y = _silu(y + cb_ref[...])
    xbuf[0:pad, :] = xbuf[L:L + pad, :]

    g_c = gif_ref[...] + gb_ref[...]
    g_r = gift_ref[0] + gbt_ref[...]
    lf_c = _log_sigmoid(g_c)
    lf_r = _log_sigmoid(g_r)
    ti = lax.broadcasted_iota(I32, (L, L), 0)
    si = lax.broadcasted_iota(I32, (L, L), 1)
    causal = si <= ti
    tri = jnp.where(causal, 1.0, 0.0).astype(BF16)
    triu = jnp.where(ti <= si, 1.0, 0.0).astype(BF16)
    bc_all = sum(jnp.dot(tri, piece, preferred_element_type=F32) for piece in _split3(lf_c))
    br_all = sum(jnp.dot(piece, triu, preferred_element_type=F32) for piece in _split3(lf_r))

    for h in range(ML_HEADS):
        q = y[:, h * ML_QK_DIM:(h + 1) * ML_QK_DIM].astype(BF16)
        kf = y[:, ML_HEADS * ML_QK_DIM + h * ML_QK_DIM:ML_HEADS * ML_QK_DIM + (h + 1) * ML_QK_DIM] * (ML_QK_DIM ** -0.5)
        k = kf.astype(BF16)
        v = v_ref[:, h * ML_V_DIM:(h + 1) * ML_V_DIM]
        bc = bc_all[:, ML_HEADS + h:ML_HEADS + h + 1]
        br = br_all[ML_HEADS + h:ML_HEADS + h + 1, :]
        ig_c = g_c[:, h:h + 1]
        ig_r = g_r[h:h + 1, :]
        m_prev = m_st[h][:, 0:1]
        c_prev = c_st[h]
        n_prev = n_st[h]

        dmat = jnp.where(causal, bc - br + ig_r, -jnp.inf)
        inter = bc + m_prev
        m_t = jnp.maximum(jnp.max(dmat, axis=-1, keepdims=True), inter)
        w = jnp.exp(dmat - m_t) * _dot_nt(q, k)
        decay = jnp.exp(inter - m_t)
        num = (jnp.dot(w.astype(BF16), v, preferred_element_type=F32)
               + decay * jnp.dot(q, c_prev.astype(BF16), preferred_element_type=F32))
        qn = jnp.sum(q.astype(F32) * n_prev.astype(BF16).astype(F32), axis=-1, keepdims=True)
        den = jnp.sum(w, axis=-1, keepdims=True) + decay * qn
        hh = num / jnp.maximum(jnp.abs(den), jnp.exp(-m_t))

        b_last = br[:, L - 1:L]
        g_row = b_last - br + ig_r
        m_new = jnp.maximum(b_last + m_prev, jnp.max(g_row, axis=-1, keepdims=True))
        wk_c = jnp.exp(b_last - bc + ig_c - m_new)
        cd = jnp.exp(b_last + m_prev - m_new)
        kw = (kf * wk_c)
        c_st[h] = cd * c_prev + lax.dot_general(kw.astype(BF16), v, (((0,), (0,)), ((), ())),
                                                preferred_element_type=F32)
        n_st[h] = cd * n_prev + jnp.sum(kw, axis=0, keepdims=True)
        m_st[h] = jnp.broadcast_to(m_new, (1, LANE))

        ms = jnp.mean(hh * hh, axis=-1, keepdims=True)
        hn = hh * lax.rsqrt(ms + NORM_EPS) * ng_ref[...]
        sl = slice(h * ML_V_DIM, (h + 1) * ML_V_DIM)
        out = hn * _sigmoid(o_ref[:, sl].astype(F32)) * _silu(z_ref[:, sl].astype(F32))
        y_ref[:, sl] = out.astype(y_ref.dtype)


def _mlstm(p_main, gif, gif_t, conv_w, conv_b, gate_b, norm_g, b, s, L):
    m = b * s
    nc = s // L
    w1k = 1024
    c_qk = MAIN_OFF[SEG_ML_QK] // w1k
    c_v = MAIN_OFF[SEG_ML_V] // w1k
    c_o = MAIN_OFF[SEG_ML_O] // w1k
    c_z = MAIN_OFF[SEG_ML_Z] // w1k
    gb = gate_b.reshape(1, 2 * ML_HEADS)
    kern = functools.partial(_ml_kernel, L=L)
    full = lambda shp: pl.BlockSpec(shp, lambda bi, ci: (0,) * len(shp))
    return pl.pallas_call(
        kern,
        grid=(b, nc),
        in_specs=[
            pl.BlockSpec((L, w1k), lambda bi, ci: (bi * nc + ci, c_qk)),
            pl.BlockSpec((L, w1k), lambda bi, ci: (bi * nc + ci, c_v)),
            pl.BlockSpec((L, w1k), lambda bi, ci: (bi * nc + ci, c_o)),
            pl.BlockSpec((L, w1k), lambda bi, ci: (bi * nc + ci, c_z)),
            pl.BlockSpec((L, 2 * ML_HEADS), lambda bi, ci: (bi * nc + ci, 0)),
            pl.BlockSpec((1, 2 * ML_HEADS, L), lambda bi, ci: (bi, 0, ci)),
            full((ML_CONV, ML_QK_WIDTH)),
            full((1, ML_QK_WIDTH)),
            full((1, 2 * ML_HEADS)),
            full((2 * ML_HEADS, 1)),
            full((1, ML_V_DIM)),
        ],
        out_specs=pl.BlockSpec((L, w1k), lambda bi, ci: (bi * nc + ci, 0)),
        out_shape=jax.ShapeDtypeStruct((m, ML_HEADS * ML_V_DIM), BF16),
        scratch_shapes=[pltpu.VMEM((L + 8, ML_QK_WIDTH), F32),
                        pltpu.VMEM((ML_HEADS, ML_QK_DIM, ML_V_DIM), F32),
                        pltpu.VMEM((ML_HEADS, 1, ML_QK_DIM), F32),
                        pltpu.VMEM((ML_HEADS, 1, LANE), F32)],
        compiler_params=_cparams(("parallel", "arbitrary")),
        name="mlstm",
    )(p_main, p_main, p_main, p_main, gif, gif_t, conv_w, conv_b.reshape(1, ML_QK_WIDTH),
      gb, gb.reshape(2 * ML_HEADS, 1), norm_g.reshape(1, ML_V_DIM))


def _dsa_norm_kernel(p_ref, gkv_ref, gik_ref, c_ref, ik_ref):
    o_kv = SMALL_OFF[SEG_DSA_KV]
    o_ik = SMALL_OFF[SEG_IDX_K]
    ckv = p_ref[:, o_kv:o_kv + DSA_LATENT]
    ms = jnp.mean(ckv * ckv, axis=-1, keepdims=True)
    c_ref[...] = (ckv * lax.rsqrt(ms + NORM_EPS) * gkv_ref[...]).astype(c_ref.dtype)
    ik = p_ref[:, o_ik:o_ik + IDX_DIM]
    ms = jnp.mean(ik * ik, axis=-1, keepdims=True)
    ik_ref[...] = (ik * lax.rsqrt(ms + NORM_EPS) * gik_ref[...]).astype(ik_ref.dtype)


def _dsa_norms(p_small, g_kv, g_ik, tm):
    m = p_small.shape[0]
    return pl.pallas_call(
        _dsa_norm_kernel,
        grid=(m // tm,),
        in_specs=[pl.BlockSpec((tm, SMALL_WIDTH), lambda i: (i, 0)),
                  pl.BlockSpec((1, DSA_LATENT), lambda i: (0, 0)),
                  pl.BlockSpec((1, IDX_DIM), lambda i: (0, 0))],
        out_specs=[pl.BlockSpec((tm, DSA_LATENT), lambda i: (i, 0)),
                   pl.BlockSpec((tm, IDX_DIM), lambda i: (i, 0))],
        out_shape=[jax.ShapeDtypeStruct((m, DSA_LATENT), BF16),
                   jax.ShapeDtypeStruct((m, IDX_DIM), BF16)],
        compiler_params=_cparams(("parallel",)),
        name="dsa_norms",
    )(p_small, g_kv.reshape(1, DSA_LATENT), g_ik.reshape(1, IDX_DIM))


def _dsa_kernel(q_ref, iq_ref, iw_ref, z_ref, c_ref, ik_ref, wuv_ref, o_ref,
                key_ref, m_ref, l_ref, acc_ref, *, tq, tk, topk, nbits_col):
    qi = pl.program_id(1)
    nkc = (qi * tq + tq + tk - 1) // tk
    row = qi * tq + lax.broadcasted_iota(I32, (tq, tk), 0)
    col0 = lax.broadcasted_iota(I32, (tq, tk), 1)
    nlb = tk // LANE

    iw = iw_ref[...] * ((IDX_HEADS ** -0.5) * (IDX_DIM ** -0.5))
    iq = iq_ref[...]

    def score_body(kc, carry):
        top1, top2 = carry
        off = pl.multiple_of(kc * tk, tk)
        ikc = ik_ref[pl.ds(off, tk), :]
        acc = jnp.zeros((tq, tk), F32)
        for h in range(IDX_HEADS):
            sc = _dot_nt(iq[:, h * IDX_DIM:(h + 1) * IDX_DIM], ikc)
            acc = acc + jnp.maximum(sc, 0.0) * iw[:, h:h + 1]
        bits = lax.bitcast_convert_type(acc, I32)
        skey = jnp.where(bits < 0, (bits ^ 0x7FFFFFFF) + 1, bits)
        skey = jnp.where(col0 + off <= row, skey, INT_MIN)
        key_ref[:, pl.ds(off, tk)] = skey
        for j in range(nlb):
            blk = skey[:, j * LANE:(j + 1) * LANE]
            top2 = jnp.maximum(top2, jnp.minimum(top1, blk))
            top1 = jnp.maximum(top1, blk)
        return top1, top2

    lowest = jnp.full((tq, LANE), INT_MIN, I32)
    top1, top2 = lax.fori_loop(0, nkc, score_body, (lowest, lowest))

    def lane_fold(x):
        part = x[:, 0:LANE]
        for j in range(1, nlb):
            part = part + x[:, j * LANE:(j + 1) * LANE]
        return part

    def count(pred):
        def body(kc, cnt):
            off = pl.multiple_of(kc * tk, tk)
            hit = pred(key_ref[:, pl.ds(off, tk)], col0 + off)
            return cnt + lane_fold(jnp.where(hit, 1, 0))
        cnt = lax.fori_loop(0, nkc, body, jnp.zeros((tq, LANE), I32))
        return jnp.sum(cnt, axis=-1, keepdims=True)

    n_valid = row[:, 0:1] + 1
    lo0 = jnp.maximum(jnp.min(top2, axis=-1, keepdims=True), INT_MIN + 1)
    hi0 = jnp.maximum(jnp.max(top1, axis=-1, keepdims=True), lo0)
    searching0 = n_valid > topk

    def any_row(flag):
        return jnp.max(jnp.where(flag, 1, 0))

    def bis_cond(st):
        return st[0] > 0

    def bis_body(st):
        _, lo, hi, cnt_lo = st
        active = searching0 & (lo < hi) & (cnt_lo != topk)
        mid = (lo >> 1) + (hi >> 1) + ((lo | hi) & 1)
        cnt = count(lambda k, c: k >= mid)
        up = active & (cnt >= topk)
        dn = active & (cnt < topk)
        lo = jnp.where(up, mid, lo)
        cnt_lo = jnp.where(up, cnt, cnt_lo)
        hi = jnp.where(dn, mid - 1, hi)
        return any_row(searching0 & (lo < hi) & (cnt_lo != topk)), lo, hi, cnt_lo

    st0 = (any_row(searching0 & (lo0 < hi0)), lo0, hi0, jnp.full((tq, 1), topk + 1, I32))
    _, thr, _, _ = lax.while_loop(bis_cond, bis_body, st0)
    thr = jnp.where(searching0, thr, INT_MIN + 1)
    cnt_ge = count(lambda k, c: k >= thr)
    has_ties = any_row(searching0 & (cnt_ge > topk)) > 0

    q = q_ref[...]
    qs = jnp.concatenate([q[:, h * DSA_LATENT:(h + 1) * DSA_LATENT] for h in range(DSA_HEADS)], axis=0)
    rows = DSA_HEADS * tq

    def attend(tie_cut):
        m_ref[...] = jnp.full(m_ref.shape, NEG_BIG, F32)
        l_ref[...] = jnp.zeros(l_ref.shape, F32)
        acc_ref[...] = jnp.zeros(acc_ref.shape, F32)

        def body(kc, carry):
            off = pl.multiple_of(kc * tk, tk)
            keys = key_ref[:, pl.ds(off, tk)]
            if tie_cut is None:
                sel = keys >= thr
            else:
                sel = (keys > thr) | ((keys == thr) & (col0 + off < tie_cut))
            bias = jnp.where(sel, 0.0, NEG_BIG)
            cc = c_ref[pl.ds(off, tk), :]
            s = _dot_nt(qs, cc)
            s = (s.reshape(DSA_HEADS, tq, tk) + bias[None]).reshape(rows, tk)
            m_prev = m_ref[...]
            m_new = jnp.maximum(m_prev, jnp.max(s, axis=-1, keepdims=True))
            alpha = jnp.exp2(m_prev - m_new)
            p = jnp.exp2(s - jnp.tile(m_new, (1, nlb)))
            l_ref[...] = alpha * l_ref[...] + jnp.sum(p, axis=-1, keepdims=True)
            acc_ref[...] = (jnp.tile(alpha, (1, DSA_LATENT // LANE)) * acc_ref[...]
                            + jnp.dot(p.astype(BF16), cc, preferred_element_type=F32))
            m_ref[...] = m_new
            return carry

        lax.fori_loop(0, nkc, body, 0)
        z = z_ref[...].astype(F32)
        for h in range(DSA_HEADS):
            rs = slice(h * tq, (h + 1) * tq)
            ob = (acc_ref[rs, :] / jnp.tile(l_ref[rs, :], (1, DSA_LATENT // LANE))).astype(BF16)
            oh = jnp.dot(ob, wuv_ref[h], preferred_element_type=F32)
            sl = slice(h * DSA_V_DIM, (h + 1) * DSA_V_DIM)
            o_ref[:, sl] = (oh * _silu(z[:, sl])).astype(o_ref.dtype)

    @pl.when(jnp.logical_not(has_ties))
    def _():
        attend(None)

    @pl.when(has_ties)
    def _():
        need = topk - count(lambda k, c: k > thr)

        def cut_body(it, cut):
            cand = cut | jnp.left_shift(jnp.int32(1), nbits_col - 1 - it)
            cnt = count(lambda k, c: (k == thr) & (c < cand))
            return jnp.where(cnt < need, cand, cut)

        cut = lax.fori_loop(0, nbits_col, cut_body, jnp.zeros((tq, 1), I32))
        attend(cut + 1)


def _dsa_attention(p_main, iw, c_lat, ik_n, w_uv, b, s, tq, tk):
    m = b * s
    nq = s // tq
    topk = min(IDX_TOPK_MAX, s // 4)
    c_q = MAIN_OFF[SEG_DSA_Q] // (DSA_HEADS * DSA_LATENT)
    c_iq = MAIN_OFF[SEG_IDX_Q] // (IDX_HEADS * IDX_DIM)
    c_z = MAIN_OFF[SEG_DSA_Z] // 1024
    nbits_col = max(1, int(math.ceil(math.log2(s))) + 1)
    kern = functools.partial(_dsa_kernel, tq=tq, tk=tk, topk=topk, nbits_col=nbits_col)
    return pl.pallas_call(
        kern,
        grid=(b, nq),
        in_specs=[
            pl.BlockSpec((tq, DSA_HEADS * DSA_LATENT), lambda bi, qi: (bi * nq + qi, c_q)),
            pl.BlockSpec((tq, IDX_HEADS * IDX_DIM), lambda bi, qi: (bi * nq + qi, c_iq)),
            pl.BlockSpec((tq, IDX_HEADS), lambda bi, qi: (bi * nq + qi, 0)),
            pl.BlockSpec((tq, 1024), lambda bi, qi: (bi * nq + qi, c_z)),
            pl.BlockSpec((s, DSA_LATENT), lambda bi, qi: (bi, 0)),
            pl.BlockSpec((s, IDX_DIM), lambda bi, qi: (bi, 0)),
            pl.BlockSpec((DSA_HEADS, DSA_LATENT, DSA_V_DIM), lambda bi, qi: (0, 0, 0)),
        ],
        out_specs=pl.BlockSpec((tq, DSA_HEADS * DSA_V_DIM), lambda bi, qi: (bi * nq + qi, 0)),
        out_shape=jax.ShapeDtypeStruct((m, DSA_HEADS * DSA_V_DIM), BF16),
        scratch_shapes=[pltpu.VMEM((tq, s), I32),
                        pltpu.VMEM((DSA_HEADS * tq, LANE), F32),
                        pltpu.VMEM((DSA_HEADS * tq, LANE), F32),
                        pltpu.VMEM((DSA_HEADS * tq, DSA_LATENT), F32)],
        compiler_params=_cparams(("parallel", "arbitrary")),
        name="dsa_attention",
    )(p_main, p_main, iw, p_main, c_lat, ik_n, w_uv)


def _out_kernel(x_ref, ya_ref, yb_ref, yc_ref, ga_ref, gb_ref, gc_ref, wb_ref, wo_ref, pg_ref, o_ref):
    mixed = None
    for i, (y_ref, g_ref) in enumerate(((ya_ref, ga_ref), (yb_ref, gb_ref), (yc_ref, gc_ref))):
        t = _sigmoid(g_ref[...].astype(F32)) * jnp.dot(y_ref[...], wb_ref[i], preferred_element_type=F32)
        mixed = t if mixed is None else mixed + t
    out = jnp.dot(mixed.astype(BF16), wo_ref[...], preferred_element_type=F32)
    ms = jnp.mean(out * out, axis=-1, keepdims=True)
    o_ref[...] = x_ref[...] + out * lax.rsqrt(ms + NORM_EPS) * pg_ref[...]


def _merge_out(x2, ya, yb, yc, p_main, w_branch, w_out, post_g, tm):
    m, d = x2.shape
    cg = MAIN_OFF[SEG_GATES] // d
    row = lambda i: (i, 0)
    return pl.pallas_call(
        _out_kernel,
        grid=(m // tm,),
        in_specs=[
            pl.BlockSpec((tm, d), row),
            pl.BlockSpec((tm, BRANCH_WIDTH), row),
            pl.BlockSpec((tm, BRANCH_WIDTH), row),
            pl.BlockSpec((tm, BRANCH_WIDTH), row),
            pl.BlockSpec((tm, d), lambda i: (i, cg)),
            pl.BlockSpec((tm, d), lambda i: (i, cg + 1)),
            pl.BlockSpec((tm, d), lambda i: (i, cg + 2)),
            pl.BlockSpec((N_BRANCHES, BRANCH_WIDTH, d), lambda i: (0, 0, 0), pipeline_mode=pl.Buffered(1)),
            pl.BlockSpec((d, d), lambda i: (0, 0), pipeline_mode=pl.Buffered(1)),
            pl.BlockSpec((1, d), lambda i: (0, 0)),
        ],
        out_specs=pl.BlockSpec((tm, d), row),
        out_shape=jax.ShapeDtypeStruct((m, d), F32),
        compiler_params=_cparams(("parallel",)),
        name="merge_out_proj",
    )(x2, ya, yb, yc, p_main, p_main, p_main, w_branch, w_out, post_g.reshape(1, d))


def _pick(total, want):
    t = min(total, want)
    while total % t:
        t //= 2
    return t


def kernel(x, pre_norm_g, w_in, da_lambda, da_norm_g, ml_conv_w, ml_conv_b, ml_gate_b, ml_norm_g,
           dsa_kv_norm_g, dsa_ik_norm_g, dsa_w_uv, w_branch, w_out, post_norm_g):
    b, s, d = x.shape
    m = b * s
    bounds = _seg_bounds()
    x2 = x.reshape(m, d)
    tm_proj = _pick(m, 1024)
    t_da = _pick(s, 512)
    l_ml = _pick(s, 256)
    tq_dsa = _pick(s, 128)
    tk_dsa = _pick(s, 512)
    o_if = SMALL_OFF[SEG_ML_IF]
    o_iw = SMALL_OFF[SEG_IDX_W]
    for l in range(DEPTH):
        w = w_in[l]
        col_scale = {SEG_DA_Q: DA_HEAD_DIM ** -0.5 * LOG2E, SEG_DSA_Q: DSA_LATENT ** -0.5 * LOG2E}
        w_main = jnp.concatenate([w[:, bounds[sg]:bounds[sg + 1]] * col_scale.get(sg, 1.0) for sg in MAIN_SEGS],
                                 axis=1).astype(BF16)
        w_small = jnp.concatenate([w[:, bounds[sg]:bounds[sg + 1]] for sg in SMALL_SEGS]
                                  + [jnp.zeros((d, SMALL_WIDTH - SMALL_USED), w.dtype)], axis=1).astype(BF16)
        p_main = _project(x2, pre_norm_g[l], w_main, BF16, tm_proj, 512)
        p_small = _project(x2, pre_norm_g[l], w_small, F32, tm_proj, SMALL_WIDTH)

        lam_init = 0.8 - 0.6 * math.exp(-0.3 * l)
        y_a = _diff_attention(p_main, da_lambda[l], da_norm_g[l], lam_init, b, s, t_da)

        gif = p_small[:, o_if:o_if + 2 * ML_HEADS]
        gif_t = jnp.transpose(gif.reshape(b, s, 2 * ML_HEADS), (0, 2, 1))
        y_b = _mlstm(p_main, gif, gif_t, ml_conv_w[l], ml_conv_b[l], ml_gate_b[l], ml_norm_g[l], b, s, l_ml)

        c_lat, ik_n = _dsa_norms(p_small, dsa_kv_norm_g[l], dsa_ik_norm_g[l], _pick(m, 2048))
        iw = p_small[:, o_iw:o_iw + IDX_HEADS]
        y_c = _dsa_attention(p_main, iw, c_lat, ik_n, dsa_w_uv[l].astype(BF16), b, s, tq_dsa, tk_dsa)

        x2 = _merge_out(x2, y_a, y_b, y_c, p_main, w_branch[l].astype(BF16), w_out[l].astype(BF16),
                        post_norm_g[l], _pick(m, 256))
    return x2.reshape(b, s, d)
```

```python
import functools
import math

import jax
import jax.numpy as jnp
from jax import lax
from jax.experimental import pallas as pl
from jax.experimental.pallas import tpu as pltpu

F32 = jnp.float32
BF16 = jnp.bfloat16
I32 = jnp.int32

D_MODEL = 2048
DEPTH = 2
BRANCH_WIDTH = 1024
N_BRANCHES = 3
NORM_EPS = 1e-6
DA_HEADS = 8
DA_HEAD_DIM = 64
DA_V_DIM = 128
ML_HEADS = 4
ML_QK_DIM = 128
ML_V_DIM = 256
ML_CONV = 4
ML_QK_WIDTH = 2 * ML_HEADS * ML_QK_DIM
DSA_HEADS = 8
DSA_LATENT = 256
DSA_V_DIM = 128
IDX_HEADS = 16
IDX_DIM = 64
IDX_TOPK_MAX = 256

IN_SPLITS = (
    1024, 1024, 1024, 1024,
    ML_QK_WIDTH, 1024, 2 * ML_HEADS,
    1024, 1024,
    DSA_HEADS * DSA_LATENT, DSA_LATENT,
    IDX_HEADS * IDX_DIM, IDX_DIM, IDX_HEADS,
    1024,
    N_BRANCHES * D_MODEL,
)
(SEG_DA_Q, SEG_DA_K, SEG_DA_V, SEG_DA_Z, SEG_ML_QK, SEG_ML_V, SEG_ML_IF, SEG_ML_O, SEG_ML_Z,
 SEG_DSA_Q, SEG_DSA_KV, SEG_IDX_Q, SEG_IDX_K, SEG_IDX_W, SEG_DSA_Z, SEG_GATES) = range(16)

MAIN_SEGS = (SEG_DA_Q, SEG_DA_K, SEG_DA_V, SEG_DA_Z, SEG_ML_QK, SEG_ML_V, SEG_ML_O, SEG_ML_Z,
             SEG_DSA_Q, SEG_IDX_Q, SEG_DSA_Z, SEG_GATES)
SMALL_SEGS = (SEG_DSA_KV, SEG_IDX_K, SEG_IDX_W, SEG_ML_IF)
SMALL_WIDTH = 384

LANE = 128
VMEM_LIMIT = 56 * 1024 * 1024
INT_MIN = -2147483648
NEG_BIG = -1e30
LOG2E = math.log2(math.e)
BOUND_SLACK = 1.01
UNDERFLOW_GUARD = 2.0 ** -100


def _seg_bounds():
    offs = [0]
    for n in IN_SPLITS:
        offs.append(offs[-1] + n)
    return offs


def _main_offsets():
    out, off = {}, 0
    for s in MAIN_SEGS:
        out[s] = off
        off += IN_SPLITS[s]
    return out, off


def _small_offsets():
    out, off = {}, 0
    for s in SMALL_SEGS:
        out[s] = off
        off += IN_SPLITS[s]
    return out, off


MAIN_OFF, MAIN_WIDTH = _main_offsets()
SMALL_OFF, SMALL_USED = _small_offsets()


def _cparams(sem):
    return pltpu.CompilerParams(dimension_semantics=sem, vmem_limit_bytes=VMEM_LIMIT)


def _silu(v):
    return v * (1.0 / (1.0 + jnp.exp(-v)))


def _sigmoid(v):
    return 1.0 / (1.0 + jnp.exp(-v))


def _dot_nt(a, b):
    return lax.dot_general(a, b, (((1,), (1,)), ((), ())), preferred_element_type=F32)


def _proj_kernel(x_ref, g_ref, w_ref, o_ref, h_ref):
    @pl.when(pl.program_id(1) == 0)
    def _():
        x = x_ref[...]
        ms = jnp.mean(x * x, axis=-1, keepdims=True)
        h_ref[...] = (x * lax.rsqrt(ms + NORM_EPS) * g_ref[...]).astype(BF16)

    o_ref[...] = jnp.dot(h_ref[...], w_ref[...], preferred_element_type=F32).astype(o_ref.dtype)


def _project(x2, g, w, out_dtype, tm, tn):
    m, d = x2.shape
    n = w.shape[1]
    return pl.pallas_call(
        _proj_kernel,
        grid=(m // tm, n // tn),
        in_specs=[pl.BlockSpec((tm, d), lambda i, j: (i, 0)),
                  pl.BlockSpec((1, d), lambda i, j: (0, 0)),
                  pl.BlockSpec((d, tn), lambda i, j: (0, j))],
        out_specs=pl.BlockSpec((tm, tn), lambda i, j: (i, j)),
        out_shape=jax.ShapeDtypeStruct((m, n), out_dtype),
        scratch_shapes=[pltpu.VMEM((tm, d), BF16)],
        compiler_params=_cparams(("parallel", "arbitrary")),
        name="rmsnorm_in_proj",
    )(x2, g.reshape(1, d), w)


def _lane_fold(x):
    part = x[:, 0:LANE]
    for j in range(1, x.shape[1] // LANE):
        part = part + x[:, j * LANE:(j + 1) * LANE]
    return part


def _da_kernel(lam_ref, q_ref, k_ref, v_ref, z_ref, g_ref, o_ref, m_ref, l_ref, acc_ref, kn_ref,
               *, t, s_len, lam_init):
    qi = pl.program_id(2)
    nrep = t // LANE
    half = [slice(c * DA_HEAD_DIM, (c + 1) * DA_HEAD_DIM) for c in range(2)]

    @pl.when(qi == 0)
    def _():
        def body(i, mx):
            kk = k_ref[pl.ds(pl.multiple_of(i * t, t), t), :].astype(F32)
            sq = kk * kk
            return tuple(jnp.maximum(mx[c], jnp.max(jnp.sum(sq[:, half[c]], axis=-1, keepdims=True),
                                                    axis=0, keepdims=True)) for c in range(2))
        mx = lax.fori_loop(0, s_len // t, body, (jnp.zeros((1, 1), F32), jnp.zeros((1, 1), F32)))
        for c in range(2):
            kn_ref[c] = jnp.broadcast_to(jnp.sqrt(mx[c]), (1, LANE))

    q = q_ref[...]
    qsq = q.astype(F32)
    qsq = qsq * qsq
    bound = [jnp.broadcast_to(jnp.sqrt(jnp.sum(qsq[:, half[c]], axis=-1, keepdims=True))
                              * kn_ref[c][:, 0:1] * BOUND_SLACK, (t, LANE)) for c in range(2)]

    def causal_keep():
        row = lax.broadcasted_iota(I32, (t, t), 0)
        col = lax.broadcasted_iota(I32, (t, t), 1)
        return col <= row

    def sweep(block):
        def body(ki, carry):
            block(pl.multiple_of(ki * t, t), False)
            return carry
        lax.fori_loop(0, qi, body, 0)
        block(pl.multiple_of(qi * t, t), True)

    def bounded_block(off, masked):
        k = k_ref[pl.ds(off, t), :]
        v = v_ref[pl.ds(off, t), :]
        for c in range(2):
            s = _dot_nt(q[:, half[c]], k[:, half[c]])
            if masked:
                s = jnp.where(causal_keep(), s, -jnp.inf)
            p = jnp.exp2(s - jnp.tile(bound[c], (1, nrep)))
            l_ref[c] = l_ref[c] + _lane_fold(p)
            acc_ref[c] = acc_ref[c] + jnp.dot(p.astype(BF16), v, preferred_element_type=F32)

    def running_max_block(off, masked):
        k = k_ref[pl.ds(off, t), :]
        v = v_ref[pl.ds(off, t), :]
        for c in range(2):
            s = _dot_nt(q[:, half[c]], k[:, half[c]])
            if masked:
                s = jnp.where(causal_keep(), s, -jnp.inf)
            m_prev = m_ref[c]
            m_new = jnp.maximum(m_prev, jnp.max(s, axis=-1, keepdims=True))
            alpha = jnp.exp2(m_prev - m_new)
            p = jnp.exp2(s - jnp.tile(m_new, (1, nrep)))
            l_ref[c] = alpha * l_ref[c] + jnp.sum(p, axis=-1, keepdims=True)
            acc_ref[c] = alpha * acc_ref[c] + jnp.dot(p.astype(BF16), v, preferred_element_type=F32)
            m_ref[c] = m_new

    l_ref[...] = jnp.zeros(l_ref.shape, F32)
    acc_ref[...] = jnp.zeros(acc_ref.shape, F32)
    sweep(bounded_block)
    l_tot = [jnp.sum(l_ref[c], axis=-1, keepdims=True) for c in range(2)]
    underflow = jnp.logical_not(jnp.min(jnp.minimum(l_tot[0], l_tot[1])) >= UNDERFLOW_GUARD)

    @pl.when(jnp.logical_not(underflow))
    def _():
        for c in range(2):
            acc_ref[c] = acc_ref[c] / l_tot[c]

    @pl.when(underflow)
    def _():
        m_ref[...] = jnp.full(m_ref.shape, -jnp.inf, F32)
        l_ref[...] = jnp.zeros(l_ref.shape, F32)
        acc_ref[...] = jnp.zeros(acc_ref.shape, F32)
        sweep(running_max_block)
        for c in range(2):
            acc_ref[c] = acc_ref[c] / l_ref[c]

    lp = lam_ref[...]
    lam = (jnp.exp(jnp.sum(lp[0:1] * lp[1:2], axis=-1, keepdims=True))
           - jnp.exp(jnp.sum(lp[2:3] * lp[3:4], axis=-1, keepdims=True)) + lam_init)
    o = acc_ref[0] - lam * acc_ref[1]
    ms = jnp.mean(o * o, axis=-1, keepdims=True)
    o = o * lax.rsqrt(ms + NORM_EPS) * g_ref[...] * (1.0 - lam_init)
    o_ref[...] = (o * _silu(z_ref[...].astype(F32))).astype(o_ref.dtype)


def _diff_attention(p_main, lam_params, norm_g, lam_init, b, s, t):
    m = b * s
    nq = s // t
    cq = MAIN_OFF[SEG_DA_Q] // LANE
    ck = MAIN_OFF[SEG_DA_K] // LANE
    cv = MAIN_OFF[SEG_DA_V] // LANE
    cz = MAIN_OFF[SEG_DA_Z] // LANE
    kern = functools.partial(_da_kernel, t=t, s_len=s, lam_init=lam_init)
    return pl.pallas_call(
        kern,
        grid=(b, DA_HEADS, nq),
        in_specs=[
            pl.BlockSpec((4, DA_HEAD_DIM), lambda bi, h, qi: (0, 0)),
            pl.BlockSpec((t, LANE), lambda bi, h, qi: (bi * nq + qi, cq + h)),
            pl.BlockSpec((s, LANE), lambda bi, h, qi: (bi, ck + h)),
            pl.BlockSpec((s, LANE), lambda bi, h, qi: (bi, cv + h)),
            pl.BlockSpec((t, LANE), lambda bi, h, qi: (bi * nq + qi, cz + h)),
            pl.BlockSpec((1, DA_V_DIM), lambda bi, h, qi: (0, 0)),
        ],
        out_specs=pl.BlockSpec((t, LANE), lambda bi, h, qi: (bi * nq + qi, h)),
        out_shape=jax.ShapeDtypeStruct((m, DA_HEADS * DA_V_DIM), BF16),
        scratch_shapes=[pltpu.VMEM((2, t, LANE), F32), pltpu.VMEM((2, t, LANE), F32),
                        pltpu.VMEM((2, t, DA_V_DIM), F32), pltpu.VMEM((2, 1, LANE), F32)],
        compiler_params=_cparams(("parallel", "parallel", "arbitrary")),
        name="diff_attention",
    )(lam_params, p_main, p_main, p_main, p_main, norm_g.reshape(1, DA_V_DIM))


def _split3(v):
    hi = v.astype(BF16)
    r1 = v - hi.astype(F32)
    mid = r1.astype(BF16)
    lo = (r1 - mid.astype(F32)).astype(BF16)
    return hi, mid, lo


def _log_sigmoid(v):
    return jnp.minimum(v, 0.0) - jnp.log(1.0 + jnp.exp(-jnp.abs(v)))


def _ml_kernel(qk_ref, v_ref, o_ref, z_ref, gif_ref, gift_ref, cw_ref, cb_ref, gb_ref, gbt_ref, ng_ref,
               y_ref, xbuf, c_st, n_st, m_st, *, L):
    ci = pl.program_id(1)
    pad = 8

    @pl.when(ci == 0)
    def _():
        xbuf[0:pad, :] = jnp.zeros((pad, ML_QK_WIDTH), F32)
        c_st[...] = jnp.zeros(c_st.shape, F32)
        n_st[...] = jnp.zeros(n_st.shape, F32)
        m_st[...] = jnp.zeros(m_st.shape, F32)

    xbuf[pad:pad + L, :] = qk_ref[...].astype(F32)
    cw = cw_ref[...]
    y = xbuf[pad:pad + L, :] * cw[ML_CONV - 1:ML_CONV, :]
    for j in range(ML_CONV - 1):
        sh = ML_CONV - 1 - j
        y = y + xbuf[pad - sh:pad - sh + L, :] * cw[j:j + 1, :]
    y = _silu(y + cb_ref[...])
    xbuf[0:pad, :] = xbuf[L:L + pad, :]

    g_c = gif_ref[...] + gb_ref[...]
    g_r = gift_ref[0] + gbt_ref[...]
    lf_c = _log_sigmoid(g_c)
    lf_r = _log_sigmoid(g_r)
    ti = lax.broadcasted_iota(I32, (L, L), 0)
    si = lax.broadcasted_iota(I32, (L, L), 1)
    causal = si <= ti
    tri = jnp.where(causal, 1.0, 0.0).astype(BF16)
    triu = jnp.where(ti <= si, 1.0, 0.0).astype(BF16)
    bc_all = sum(jnp.dot(tri, piece, preferred_element_type=F32) for piece in _split3(lf_c))
    br_all = sum(jnp.dot(piece, triu, preferred_element_type=F32) for piece in _split3(lf_r))

    for h in range(ML_HEADS):
        q = y[:, h * ML_QK_DIM:(h + 1) * ML_QK_DIM].astype(BF16)
        kf = y[:, ML_HEADS * ML_QK_DIM + h * ML_QK_DIM:ML_HEADS * ML_QK_DIM + (h + 1) * ML_QK_DIM] * (ML_QK_DIM ** -0.5)
        k = kf.astype(BF16)
        v = v_ref[:, h * ML_V_DIM:(h + 1) * ML_V_DIM]
        bc = bc_all[:, ML_HEADS + h:ML_HEADS + h + 1]
        br = br_all[ML_HEADS + h:ML_HEADS + h + 1, :]
        ig_c = g_c[:, h:h + 1]
        ig_r = g_r[h:h + 1, :]
        m_prev = m_st[h][:, 0:1]
        c_prev = c_st[h]
        n_prev = n_st[h]

        dmat = jnp.where(causal, bc - br + ig_r, -jnp.inf)
        inter = bc + m_prev
        m_t = jnp.maximum(jnp.max(dmat, axis=-1, keepdims=True), inter)
        w = jnp.exp(dmat - m_t) * _dot_nt(q, k)
        decay = jnp.exp(inter - m_t)
        num = (jnp.dot(w.astype(BF16), v, preferred_element_type=F32)
               + decay * jnp.dot(q, c_prev.astype(BF16), preferred_element_type=F32))
        qn = jnp.sum(q.astype(F32) * n_prev.astype(BF16).astype(F32), axis=-1, keepdims=True)
        den = jnp.sum(w, axis=-1, keepdims=True) + decay * qn
        hh = num / jnp.maximum(jnp.abs(den), jnp.exp(-m_t))

        b_last = br[:, L - 1:L]
        g_row = b_last - br + ig_r
        m_new = jnp.maximum(b_last + m_prev, jnp.max(g_row, axis=-1, keepdims=True))
        wk_c = jnp.exp(b_last - bc + ig_c - m_new)
        cd = jnp.exp(b_last + m_prev - m_new)
        kw = (kf * wk_c)
        c_st[h] = cd * c_prev + lax.dot_general(kw.astype(BF16), v, (((0,), (0,)), ((), ())),
                                                preferred_element_type=F32)
        n_st[h] = cd * n_prev + jnp.sum(kw, axis=0, keepdims=True)
        m_st[h] = jnp.broadcast_to(m_new, (1, LANE))

        ms = jnp.mean(hh * hh, axis=-1, keepdims=True)
        hn = hh * lax.rsqrt(ms + NORM_EPS) * ng_ref[...]
        sl = slice(h * ML_V_DIM, (h + 1) * ML_V_DIM)
        out = hn * _sigmoid(o_ref[:, sl].astype(F32)) * _silu(z_ref[:, sl].astype(F32))
        y_ref[:, sl] = out.astype(y_ref.dtype)


def _mlstm(p_main, gif, gif_t, conv_w, conv_b, gate_b, norm_g, b, s, L):
    m = b * s
    nc = s // L
    w1k = 1024
    c_qk = MAIN_OFF[SEG_ML_QK] // w1k
    c_v = MAIN_OFF[SEG_ML_V] // w1k
    c_o = MAIN_OFF[SEG_ML_O] // w1k
    c_z = MAIN_OFF[SEG_ML_Z] // w1k
    gb = gate_b.reshape(1, 2 * ML_HEADS)
    kern = functools.partial(_ml_kernel, L=L)
    full = lambda shp: pl.BlockSpec(shp, lambda bi, ci: (0,) * len(shp))
    return pl.pallas_call(
        kern,
        grid=(b, nc),
        in_specs=[
            pl.BlockSpec((L, w1k), lambda bi, ci: (bi * nc + ci, c_qk)),
            pl.BlockSpec((L, w1k), lambda bi, ci: (bi * nc + ci, c_v)),
            pl.BlockSpec((L, w1k), lambda bi, ci: (bi * nc + ci, c_o)),
            pl.BlockSpec((L, w1k), lambda bi, ci: (bi * nc + ci, c_z)),
            pl.BlockSpec((L, 2 * ML_HEADS), lambda bi, ci: (bi * nc + ci, 0)),
            pl.BlockSpec((1, 2 * ML_HEADS, L), lambda bi, ci: (bi, 0, ci)),
            full((ML_CONV, ML_QK_WIDTH)),
            full((1, ML_QK_WIDTH)),
            full((1, 2 * ML_HEADS)),
            full((2 * ML_HEADS, 1)),
            full((1, ML_V_DIM)),
        ],
        out_specs=pl.BlockSpec((L, w1k), lambda bi, ci: (bi * nc + ci, 0)),
        out_shape=jax.ShapeDtypeStruct((m, ML_HEADS * ML_V_DIM), BF16),
        scratch_shapes=[pltpu.VMEM((L + 8, ML_QK_WIDTH), F32),
                        pltpu.VMEM((ML_HEADS, ML_QK_DIM, ML_V_DIM), F32),
                        pltpu.VMEM((ML_HEADS, 1, ML_QK_DIM), F32),
                        pltpu.VMEM((ML_HEADS, 1, LANE), F32)],
        compiler_params=_cparams(("parallel", "arbitrary")),
        name="mlstm",
    )(p_main, p_main, p_main, p_main, gif, gif_t, conv_w, conv_b.reshape(1, ML_QK_WIDTH),
      gb, gb.reshape(2 * ML_HEADS, 1), norm_g.reshape(1, ML_V_DIM))


def _dsa_norm_kernel(p_ref, gkv_ref, gik_ref, c_ref, ik_ref):
    o_kv = SMALL_OFF[SEG_DSA_KV]
    o_ik = SMALL_OFF[SEG_IDX_K]
    ckv = p_ref[:, o_kv:o_kv + DSA_LATENT]
    ms = jnp.mean(ckv * ckv, axis=-1, keepdims=True)
    c_ref[...] = (ckv * lax.rsqrt(ms + NORM_EPS) * gkv_ref[...]).astype(c_ref.dtype)
    ik = p_ref[:, o_ik:o_ik + IDX_DIM]
    ms = jnp.mean(ik * ik, axis=-1, keepdims=True)
    ik_ref[...] = (ik * lax.rsqrt(ms + NORM_EPS) * gik_ref[...]).astype(ik_ref.dtype)


def _dsa_norms(p_small, g_kv, g_ik, tm):
    m = p_small.shape[0]
    return pl.pallas_call(
        _dsa_norm_kernel,
        grid=(m // tm,),
        in_specs=[pl.BlockSpec((tm, SMALL_WIDTH), lambda i: (i, 0)),
                  pl.BlockSpec((1, DSA_LATENT), lambda i: (0, 0)),
                  pl.BlockSpec((1, IDX_DIM), lambda i: (0, 0))],
        out_specs=[pl.BlockSpec((tm, DSA_LATENT), lambda i: (i, 0)),
                   pl.BlockSpec((tm, IDX_DIM), lambda i: (i, 0))],
        out_shape=[jax.ShapeDtypeStruct((m, DSA_LATENT), BF16),
                   jax.ShapeDtypeStruct((m, IDX_DIM), BF16)],
        compiler_params=_cparams(("parallel",)),
        name="dsa_norms",
    )(p_small, g_kv.reshape(1, DSA_LATENT), g_ik.reshape(1, IDX_DIM))


def _dsa_kernel(q_ref, iq_ref, iw_ref, z_ref, c_ref, ik_ref, wuv_ref, o_ref,
                key_ref, m_ref, l_ref, acc_ref, cut_ref, cn_ref, *, tq, tk, topk, nbits_col, s_len):
    qi = pl.program_id(1)
    nkc = (qi * tq + tq + tk - 1) // tk
    row = qi * tq + lax.broadcasted_iota(I32, (tq, tk), 0)
    col0 = lax.broadcasted_iota(I32, (tq, tk), 1)
    nlb = tk // LANE

    iw = iw_ref[...] * ((IDX_HEADS ** -0.5) * (IDX_DIM ** -0.5))
    iq = iq_ref[...]

    def score_body(kc, carry):
        top1, top2 = carry
        off = pl.multiple_of(kc * tk, tk)
        ikc = ik_ref[pl.ds(off, tk), :]
        acc = jnp.zeros((tq, tk), F32)
        for h in range(IDX_HEADS):
            sc = _dot_nt(iq[:, h * IDX_DIM:(h + 1) * IDX_DIM], ikc)
            acc = acc + jnp.maximum(sc, 0.0) * iw[:, h:h + 1]
        bits = lax.bitcast_convert_type(acc, I32)
        skey = jnp.where(bits < 0, (bits ^ 0x7FFFFFFF) + 1, bits)
        skey = jnp.where(col0 + off <= row, skey, INT_MIN)
        key_ref[:, pl.ds(off, tk)] = skey
        for j in range(nlb):
            blk = skey[:, j * LANE:(j + 1) * LANE]
            top2 = jnp.maximum(top2, jnp.minimum(top1, blk))
            top1 = jnp.maximum(top1, blk)
        return top1, top2

    lowest = jnp.full((tq, LANE), INT_MIN, I32)
    top1, top2 = lax.fori_loop(0, nkc, score_body, (lowest, lowest))

    def count(pred):
        def body(kc, cnt):
            off = pl.multiple_of(kc * tk, tk)
            hit = pred(key_ref[:, pl.ds(off, tk)], col0 + off)
            return cnt + _lane_fold(jnp.where(hit, 1, 0))
        cnt = lax.fori_loop(0, nkc, body, jnp.zeros((tq, LANE), I32))
        return jnp.sum(cnt, axis=-1, keepdims=True)

    n_valid = row[:, 0:1] + 1
    lo0 = jnp.maximum(jnp.min(top2, axis=-1, keepdims=True), INT_MIN + 1)
    hi0 = jnp.maximum(jnp.max(top1, axis=-1, keepdims=True), lo0)
    searching0 = n_valid > topk

    def any_row(flag):
        return jnp.max(jnp.where(flag, 1, 0))

    def bis_cond(st):
        return st[0] > 0

    def bis_body(st):
        _, lo, hi, cnt_lo = st
        active = searching0 & (lo < hi) & (cnt_lo != topk)
        mid = (lo >> 1) + (hi >> 1) + ((lo | hi) & 1)
        cnt = count(lambda k, c: k >= mid)
        up = active & (cnt >= topk)
        dn = active & (cnt < topk)
        lo = jnp.where(up, mid, lo)
        cnt_lo = jnp.where(up, cnt, cnt_lo)
        hi = jnp.where(dn, mid - 1, hi)
        return any_row(searching0 & (lo < hi) & (cnt_lo != topk)), lo, hi, cnt_lo

    st0 = (any_row(searching0 & (lo0 < hi0)), lo0, hi0, jnp.full((tq, 1), topk + 1, I32))
    _, thr, _, _ = lax.while_loop(bis_cond, bis_body, st0)
    thr = jnp.where(searching0, thr, INT_MIN + 1)
    cnt_ge = count(lambda k, c: k >= thr)
    has_ties = any_row(searching0 & (cnt_ge > topk)) > 0

    q = q_ref[...]
    qs = jnp.concatenate([q[:, h * DSA_LATENT:(h + 1) * DSA_LATENT] for h in range(DSA_HEADS)], axis=0)
    n_groups = 2
    grows = DSA_HEADS * tq // n_groups

    cut_ref[...] = jnp.full(cut_ref.shape, s_len, I32)

    @pl.when(has_ties)
    def _():
        need = topk - count(lambda k, c: k > thr)

        def cut_body(it, cut):
            cand = cut | jnp.left_shift(jnp.int32(1), nbits_col - 1 - it)
            cnt = count(lambda k, c: (k == thr) & (c < cand))
            return jnp.where(cnt < need, cand, cut)

        cut = lax.fori_loop(0, nbits_col, cut_body, jnp.zeros((tq, 1), I32))
        cut_ref[...] = jnp.broadcast_to(cut + 1, cut_ref.shape)

    tie_cut = cut_ref[:, 0:1]

    @pl.when(qi == 0)
    def _():
        def body(i, mx):
            cf = c_ref[pl.ds(pl.multiple_of(i * tk, tk), tk), :].astype(F32)
            return jnp.maximum(mx, jnp.max(jnp.sum(cf * cf, axis=-1, keepdims=True), axis=0, keepdims=True))
        mx = lax.fori_loop(0, s_len // tk, body, jnp.zeros((1, 1), F32))
        cn_ref[...] = jnp.broadcast_to(jnp.sqrt(mx), cn_ref.shape)

    def selection_bias(kc):
        off = pl.multiple_of(kc * tk, tk)
        keys = key_ref[:, pl.ds(off, tk)]
        sel = (keys > thr) | ((keys == thr) & (col0 + off < tie_cut))
        return off, jnp.where(sel, 0.0, NEG_BIG)

    def group_scores(g, cc, bias):
        s = _dot_nt(qs[g * grows:(g + 1) * grows, :], cc)
        return (s.reshape(grows // tq, tq, tk) + bias[None]).reshape(grows, tk)

    def finalize(l_col):
        z = z_ref[...].astype(F32)
        for h in range(DSA_HEADS):
            rs = slice(h * tq, (h + 1) * tq)
            ob = (acc_ref[rs, :] / l_col[rs, :]).astype(BF16)
            oh = jnp.dot(ob, wuv_ref[h], preferred_element_type=F32)
            sl = slice(h * DSA_V_DIM, (h + 1) * DSA_V_DIM)
            o_ref[:, sl] = (oh * _silu(z[:, sl])).astype(o_ref.dtype)

    qf = qs.astype(F32)
    m_ref[...] = jnp.broadcast_to(jnp.sqrt(jnp.sum(qf * qf, axis=-1, keepdims=True))
                                  * cn_ref[:, 0:1] * BOUND_SLACK, m_ref.shape)
    l_ref[...] = jnp.zeros(l_ref.shape, F32)
    acc_ref[...] = jnp.zeros(acc_ref.shape, F32)

    def bounded_body(kc, carry):
        off, bias = selection_bias(kc)
        cc = c_ref[pl.ds(off, tk), :]
        for g in range(n_groups):
            rs = slice(g * grows, (g + 1) * grows)
            p = jnp.exp2(group_scores(g, cc, bias) - jnp.tile(m_ref[rs, :], (1, nlb)))
            l_ref[rs, :] = l_ref[rs, :] + _lane_fold(p)
            acc_ref[rs, :] = acc_ref[rs, :] + jnp.dot(p.astype(BF16), cc, preferred_element_type=F32)
        return carry

    lax.fori_loop(0, nkc, bounded_body, 0)
    l_tot = jnp.sum(l_ref[...], axis=-1, keepdims=True)
    underflow = jnp.logical_not(jnp.min(l_tot) >= UNDERFLOW_GUARD)

    @pl.when(jnp.logical_not(underflow))
    def _():
        finalize(l_tot)

    @pl.when(underflow)
    def _():
        m_ref[...] = jnp.full(m_ref.shape, NEG_BIG, F32)
        l_ref[...] = jnp.zeros(l_ref.shape, F32)
        acc_ref[...] = jnp.zeros(acc_ref.shape, F32)

        def body(kc, carry):
            off, bias = selection_bias(kc)
            cc = c_ref[pl.ds(off, tk), :]
            for g in range(n_groups):
                rs = slice(g * grows, (g + 1) * grows)
                s = group_scores(g, cc, bias)
                m_prev = m_ref[rs, :]
                m_new = jnp.maximum(m_prev, jnp.max(s, axis=-1, keepdims=True))
                alpha = jnp.exp2(m_prev - m_new)
                p = jnp.exp2(s - jnp.tile(m_new, (1, nlb)))
                l_ref[rs, :] = alpha * l_ref[rs, :] + jnp.sum(p, axis=-1, keepdims=True)
                acc_ref[rs, :] = (jnp.tile(alpha, (1, DSA_LATENT // LANE)) * acc_ref[rs, :]
                                  + jnp.dot(p.astype(BF16), cc, preferred_element_type=F32))
                m_ref[rs, :] = m_new
            return carry

        lax.fori_loop(0, nkc, body, 0)
        finalize(l_ref[:, 0:1])


def _dsa_attention(p_main, iw, c_lat, ik_n, w_uv, b, s, tq, tk):
    m = b * s
    nq = s // tq
    topk = min(IDX_TOPK_MAX, s // 4)
    c_q = MAIN_OFF[SEG_DSA_Q] // (DSA_HEADS * DSA_LATENT)
    c_iq = MAIN_OFF[SEG_IDX_Q] // (IDX_HEADS * IDX_DIM)
    c_z = MAIN_OFF[SEG_DSA_Z] // 1024
    nbits_col = max(1, int(math.ceil(math.log2(s))) + 1)
    kern = functools.partial(_dsa_kernel, tq=tq, tk=tk, topk=topk, nbits_col=nbits_col, s_len=s)
    return pl.pallas_call(
        kern,
        grid=(b, nq),
        in_specs=[
            pl.BlockSpec((tq, DSA_HEADS * DSA_LATENT), lambda bi, qi: (bi * nq + qi, c_q)),
            pl.BlockSpec((tq, IDX_HEADS * IDX_DIM), lambda bi, qi: (bi * nq + qi, c_iq)),
            pl.BlockSpec((tq, IDX_HEADS), lambda bi, qi: (bi * nq + qi, 0)),
            pl.BlockSpec((tq, 1024), lambda bi, qi: (bi * nq + qi, c_z)),
            pl.BlockSpec((s, DSA_LATENT), lambda bi, qi: (bi, 0)),
            pl.BlockSpec((s, IDX_DIM), lambda bi, qi: (bi, 0)),
            pl.BlockSpec((DSA_HEADS, DSA_LATENT, DSA_V_DIM), lambda bi, qi: (0, 0, 0)),
        ],
        out_specs=pl.BlockSpec((tq, DSA_HEADS * DSA_V_DIM), lambda bi, qi: (bi * nq + qi, 0)),
        out_shape=jax.ShapeDtypeStruct((m, DSA_HEADS * DSA_V_DIM), BF16),
        scratch_shapes=[pltpu.VMEM((tq, s), I32),
                        pltpu.VMEM((DSA_HEADS * tq, LANE), F32),
                        pltpu.VMEM((DSA_HEADS * tq, LANE), F32),
                        pltpu.VMEM((DSA_HEADS * tq, DSA_LATENT), F32),
                        pltpu.VMEM((tq, LANE), I32),
                        pltpu.VMEM((1, LANE), F32)],
        compiler_params=_cparams(("parallel", "arbitrary")),
        name="dsa_attention",
    )(p_main, p_main, iw, p_main, c_lat, ik_n, w_uv)


def _out_kernel(x_ref, ya_ref, yb_ref, yc_ref, ga_ref, gb_ref, gc_ref, wb_ref, wo_ref, pg_ref, o_ref):
    mixed = None
    for i, (y_ref, g_ref) in enumerate(((ya_ref, ga_ref), (yb_ref, gb_ref), (yc_ref, gc_ref))):
        t = _sigmoid(g_ref[...].astype(F32)) * jnp.dot(y_ref[...], wb_ref[i], preferred_element_type=F32)
        mixed = t if mixed is None else mixed + t
    out = jnp.dot(mixed.astype(BF16), wo_ref[...], preferred_element_type=F32)
    ms = jnp.mean(out * out, axis=-1, keepdims=True)
    o_ref[...] = x_ref[...] + out * lax.rsqrt(ms + NORM_EPS) * pg_ref[...]


def _merge_out(x2, ya, yb, yc, p_main, w_branch, w_out, post_g, tm):
    m, d = x2.shape
    cg = MAIN_OFF[SEG_GATES] // d
    row = lambda i: (i, 0)
    return pl.pallas_call(
        _out_kernel,
        grid=(m // tm,),
        in_specs=[
            pl.BlockSpec((tm, d), row),
            pl.BlockSpec((tm, BRANCH_WIDTH), row),
            pl.BlockSpec((tm, BRANCH_WIDTH), row),
            pl.BlockSpec((tm, BRANCH_WIDTH), row),
            pl.BlockSpec((tm, d), lambda i: (i, cg)),
            pl.BlockSpec((tm, d), lambda i: (i, cg + 1)),
            pl.BlockSpec((tm, d), lambda i: (i, cg + 2)),
            pl.BlockSpec((N_BRANCHES, BRANCH_WIDTH, d), lambda i: (0, 0, 0), pipeline_mode=pl.Buffered(1)),
            pl.BlockSpec((d, d), lambda i: (0, 0), pipeline_mode=pl.Buffered(1)),
            pl.BlockSpec((1, d), lambda i: (0, 0)),
        ],
        out_specs=pl.BlockSpec((tm, d), row),
        out_shape=jax.ShapeDtypeStruct((m, d), F32),
        compiler_params=_cparams(("parallel",)),
        name="merge_out_proj",
    )(x2, ya, yb, yc, p_main, p_main, p_main, w_branch, w_out, post_g.reshape(1, d))


def _pick(total, want):
    t = min(total, want)
    while total % t:
        t //= 2
    return t


def kernel(x, pre_norm_g, w_in, da_lambda, da_norm_g, ml_conv_w, ml_conv_b, ml_gate_b, ml_norm_g,
           dsa_kv_norm_g, dsa_ik_norm_g, dsa_w_uv, w_branch, w_out, post_norm_g):
    b, s, d = x.shape
    m = b * s
    bounds = _seg_bounds()
    x2 = x.reshape(m, d)
    tm_proj = _pick(m, 1024)
    t_da = _pick(s, 512)
    l_ml = _pick(s, 256)
    tq_dsa = _pick(s, 128)
    tk_dsa = _pick(s, 512)
    o_if = SMALL_OFF[SEG_ML_IF]
    o_iw = SMALL_OFF[SEG_IDX_W]
    for l in range(DEPTH):
        w = w_in[l]
        col_scale = {SEG_DA_Q: DA_HEAD_DIM ** -0.5 * LOG2E, SEG_DSA_Q: DSA_LATENT ** -0.5 * LOG2E}
        w_main = jnp.concatenate([w[:, bounds[sg]:bounds[sg + 1]] * col_scale.get(sg, 1.0) for sg in MAIN_SEGS],
                                 axis=1).astype(BF16)
        w_small = jnp.concatenate([w[:, bounds[sg]:bounds[sg + 1]] for sg in SMALL_SEGS]
                                  + [jnp.zeros((d, SMALL_WIDTH - SMALL_USED), w.dtype)], axis=1).astype(BF16)
        p_main = _project(x2, pre_norm_g[l], w_main, BF16, tm_proj, 512)
        p_small = _project(x2, pre_norm_g[l], w_small, F32, tm_proj, SMALL_WIDTH)

        lam_init = 0.8 - 0.6 * math.exp(-0.3 * l)
        y_a = _diff_attention(p_main, da_lambda[l], da_norm_g[l], lam_init, b, s, t_da)

        gif = p_small[:, o_if:o_if + 2 * ML_HEADS]
        gif_t = jnp.transpose(gif.reshape(b, s, 2 * ML_HEADS), (0, 2, 1))
        y_b = _mlstm(p_main, gif, gif_t, ml_conv_w[l], ml_conv_b[l], ml_gate_b[l], ml_norm_g[l], b, s, l_ml)

        c_lat, ik_n = _dsa_norms(p_small, dsa_kv_norm_g[l], dsa_ik_norm_g[l], _pick(m, 2048))
        iw = p_small[:, o_iw:o_iw + IDX_HEADS]
        y_c = _dsa_attention(p_main, iw, c_lat, ik_n, dsa_w_uv[l].astype(BF16), b, s, tq_dsa, tk_dsa)

        x2 = _merge_out(x2, y_a, y_b, y_c, p_main, w_branch[l].astype(BF16), w_out[l].astype(BF16),
                        post_norm_g[l], _pick(m, 256))
    return x2.reshape(b, s, d)
```

```python
import functools
import math

import jax
import jax.numpy as jnp
from jax import lax
from jax.experimental import pallas as pl
from jax.experimental.pallas import tpu as pltpu

F32 = jnp.float32
BF16 = jnp.bfloat16
I32 = jnp.int32

D_MODEL = 2048
DEPTH = 2
BRANCH_WIDTH = 1024
N_BRANCHES = 3
NORM_EPS = 1e-6
DA_HEADS = 8
DA_HEAD_DIM = 64
DA_V_DIM = 128
ML_HEADS = 4
ML_QK_DIM = 128
ML_V_DIM = 256
ML_CONV = 4
ML_QK_WIDTH = 2 * ML_HEADS * ML_QK_DIM
DSA_HEADS = 8
DSA_LATENT = 256
DSA_V_DIM = 128
IDX_HEADS = 16
IDX_DIM = 64
IDX_TOPK_MAX = 256

IN_SPLITS = (
    1024, 1024, 1024, 1024,
    ML_QK_WIDTH, 1024, 2 * ML_HEADS,
    1024, 1024,
    DSA_HEADS * DSA_LATENT, DSA_LATENT,
    IDX_HEADS * IDX_DIM, IDX_DIM, IDX_HEADS,
    1024,
    N_BRANCHES * D_MODEL,
)
(SEG_DA_Q, SEG_DA_K, SEG_DA_V, SEG_DA_Z, SEG_ML_QK, SEG_ML_V, SEG_ML_IF, SEG_ML_O, SEG_ML_Z,
 SEG_DSA_Q, SEG_DSA_KV, SEG_IDX_Q, SEG_IDX_K, SEG_IDX_W, SEG_DSA_Z, SEG_GATES) = range(16)

MAIN_SEGS = (SEG_DA_Q, SEG_DA_K, SEG_DA_V, SEG_DA_Z, SEG_ML_QK, SEG_ML_V, SEG_ML_O, SEG_ML_Z,
             SEG_DSA_Q, SEG_IDX_Q, SEG_DSA_Z, SEG_GATES)
SMALL_SEGS = (SEG_DSA_KV, SEG_IDX_K, SEG_IDX_W, SEG_ML_IF)
SMALL_WIDTH = 384

LANE = 128
VMEM_LIMIT = 56 * 1024 * 1024
INT_MIN = -2147483648
NEG_BIG = -1e30
LOG2E = math.log2(math.e)
BOUND_SLACK = 1.01
UNDERFLOW_GUARD = 2.0 ** -100


def _seg_bounds():
    offs = [0]
    for n in IN_SPLITS:
        offs.append(offs[-1] + n)
    return offs


def _main_offsets():
    out, off = {}, 0
    for s in MAIN_SEGS:
        out[s] = off
        off += IN_SPLITS[s]
    return out, off


def _small_offsets():
    out, off = {}, 0
    for s in SMALL_SEGS:
        out[s] = off
        off += IN_SPLITS[s]
    return out, off


MAIN_OFF, MAIN_WIDTH = _main_offsets()
SMALL_OFF, SMALL_USED = _small_offsets()


def _cparams(sem):
    return pltpu.CompilerParams(dimension_semantics=sem, vmem_limit_bytes=VMEM_LIMIT)


def _silu(v):
    return v * (1.0 / (1.0 + jnp.exp(-v)))


def _sigmoid(v):
    return 1.0 / (1.0 + jnp.exp(-v))


def _dot_nt(a, b):
    return lax.dot_general(a, b, (((1,), (1,)), ((), ())), preferred_element_type=F32)


def _proj_kernel(x_ref, g_ref, w_ref, o_ref, h_ref):
    @pl.when(pl.program_id(1) == 0)
    def _():
        x = x_ref[...]
        ms = jnp.mean(x * x, axis=-1, keepdims=True)
        h_ref[...] = (x * lax.rsqrt(ms + NORM_EPS) * g_ref[...]).astype(BF16)

    o_ref[...] = jnp.dot(h_ref[...], w_ref[...], preferred_element_type=F32).astype(o_ref.dtype)


def _project(x2, g, w, out_dtype, tm, tn):
    m, d = x2.shape
    n = w.shape[1]
    return pl.pallas_call(
        _proj_kernel,
        grid=(m // tm, n // tn),
        in_specs=[pl.BlockSpec((tm, d), lambda i, j: (i, 0)),
                  pl.BlockSpec((1, d), lambda i, j: (0, 0)),
                  pl.BlockSpec((d, tn), lambda i, j: (0, j))],
        out_specs=pl.BlockSpec((tm, tn), lambda i, j: (i, j)),
        out_shape=jax.ShapeDtypeStruct((m, n), out_dtype),
        scratch_shapes=[pltpu.VMEM((tm, d), BF16)],
        compiler_params=_cparams(("parallel", "arbitrary")),
        name="rmsnorm_in_proj",
    )(x2, g.reshape(1, d), w)


def _lane_fold(x):
    part = x[:, 0:LANE]
    for j in range(1, x.shape[1] // LANE):
        part = part + x[:, j * LANE:(j + 1) * LANE]
    return part


def _da_kernel(lam_ref, q_ref, k_ref, v_ref, z_ref, g_ref, o_ref, m_ref, l_ref, acc_ref, kn_ref,
               *, t, s_len, lam_init):
    qi = pl.program_id(2)
    nrep = t // LANE
    half = [slice(c * DA_HEAD_DIM, (c + 1) * DA_HEAD_DIM) for c in range(2)]

    @pl.when(qi == 0)
    def _():
        def body(i, mx):
            kk = k_ref[pl.ds(pl.multiple_of(i * t, t), t), :].astype(F32)
            sq = kk * kk
            return tuple(jnp.maximum(mx[c], jnp.max(jnp.sum(sq[:, half[c]], axis=-1, keepdims=True),
                                                    axis=0, keepdims=True)) for c in range(2))
        mx = lax.fori_loop(0, s_len // t, body, (jnp.zeros((1, 1), F32), jnp.zeros((1, 1), F32)))
        for c in range(2):
            kn_ref[c] = jnp.broadcast_to(jnp.sqrt(mx[c]), (1, LANE))

    q = q_ref[...]
    qsq = q.astype(F32)
    qsq = qsq * qsq
    bound = [jnp.broadcast_to(jnp.sqrt(jnp.sum(qsq[:, half[c]], axis=-1, keepdims=True))
                              * kn_ref[c][:, 0:1] * BOUND_SLACK, (t, LANE)) for c in range(2)]

    def causal_keep():
        row = lax.broadcasted_iota(I32, (t, t), 0)
        col = lax.broadcasted_iota(I32, (t, t), 1)
        return col <= row

    def sweep(block):
        def body(kp, carry):
            block(pl.multiple_of(kp * (2 * t), 2 * t), False)
            block(pl.multiple_of(kp * (2 * t) + t, t), False)
            return carry
        lax.fori_loop(0, qi // 2, body, 0)

        @pl.when(qi % 2 == 1)
        def _():
            block(pl.multiple_of((qi - 1) * t, t), False)

        block(pl.multiple_of(qi * t, t), True)

    def bounded_block(off, masked):
        k = k_ref[pl.ds(off, t), :]
        v = v_ref[pl.ds(off, t), :]
        for c in range(2):
            s = _dot_nt(q[:, half[c]], k[:, half[c]])
            if masked:
                s = jnp.where(causal_keep(), s, -jnp.inf)
            p = jnp.exp2(s - jnp.tile(bound[c], (1, nrep)))
            l_ref[c] = l_ref[c] + _lane_fold(p)
            acc_ref[c] = acc_ref[c] + jnp.dot(p.astype(BF16), v, preferred_element_type=F32)

    def running_max_block(off, masked):
        k = k_ref[pl.ds(off, t), :]
        v = v_ref[pl.ds(off, t), :]
        for c in range(2):
            s = _dot_nt(q[:, half[c]], k[:, half[c]])
            if masked:
                s = jnp.where(causal_keep(), s, -jnp.inf)
            m_prev = m_ref[c]
            m_new = jnp.maximum(m_prev, jnp.max(s, axis=-1, keepdims=True))
            alpha = jnp.exp2(m_prev - m_new)
            p = jnp.exp2(s - jnp.tile(m_new, (1, nrep)))
            l_ref[c] = alpha * l_ref[c] + jnp.sum(p, axis=-1, keepdims=True)
            acc_ref[c] = alpha * acc_ref[c] + jnp.dot(p.astype(BF16), v, preferred_element_type=F32)
            m_ref[c] = m_new

    l_ref[...] = jnp.zeros(l_ref.shape, F32)
    acc_ref[...] = jnp.zeros(acc_ref.shape, F32)
    sweep(bounded_block)
    l_tot = [jnp.sum(l_ref[c], axis=-1, keepdims=True) for c in range(2)]
    underflow = jnp.logical_not(jnp.min(jnp.minimum(l_tot[0], l_tot[1])) >= UNDERFLOW_GUARD)

    @pl.when(jnp.logical_not(underflow))
    def _():
        for c in range(2):
            acc_ref[c] = acc_ref[c] / l_tot[c]

    @pl.when(underflow)
    def _():
        m_ref[...] = jnp.full(m_ref.shape, -jnp.inf, F32)
        l_ref[...] = jnp.zeros(l_ref.shape, F32)
        acc_ref[...] = jnp.zeros(acc_ref.shape, F32)
        sweep(running_max_block)
        for c in range(2):
            acc_ref[c] = acc_ref[c] / l_ref[c]

    lp = lam_ref[...]
    lam = (jnp.exp(jnp.sum(lp[0:1] * lp[1:2], axis=-1, keepdims=True))
           - jnp.exp(jnp.sum(lp[2:3] * lp[3:4], axis=-1, keepdims=True)) + lam_init)
    o = acc_ref[0] - lam * acc_ref[1]
    ms = jnp.mean(o * o, axis=-1, keepdims=True)
    o = o * lax.rsqrt(ms + NORM_EPS) * g_ref[...] * (1.0 - lam_init)
    o_ref[...] = (o * _silu(z_ref[...].astype(F32))).astype(o_ref.dtype)


def _diff_attention(p_main, lam_params, norm_g, lam_init, b, s, t):
    m = b * s
    nq = s // t
    cq = MAIN_OFF[SEG_DA_Q] // LANE
    ck = MAIN_OFF[SEG_DA_K] // LANE
    cv = MAIN_OFF[SEG_DA_V] // LANE
    cz = MAIN_OFF[SEG_DA_Z] // LANE
    kern = functools.partial(_da_kernel, t=t, s_len=s, lam_init=lam_init)
    return pl.pallas_call(
        kern,
        grid=(b, DA_HEADS, nq),
        in_specs=[
            pl.BlockSpec((4, DA_HEAD_DIM), lambda bi, h, qi: (0, 0)),
            pl.BlockSpec((t, LANE), lambda bi, h, qi: (bi * nq + qi, cq + h)),
            pl.BlockSpec((s, LANE), lambda bi, h, qi: (bi, ck + h)),
            pl.BlockSpec((s, LANE), lambda bi, h, qi: (bi, cv + h)),
            pl.BlockSpec((t, LANE), lambda bi, h, qi: (bi * nq + qi, cz + h)),
            pl.BlockSpec((1, DA_V_DIM), lambda bi, h, qi: (0, 0)),
        ],
        out_specs=pl.BlockSpec((t, LANE), lambda bi, h, qi: (bi * nq + qi, h)),
        out_shape=jax.ShapeDtypeStruct((m, DA_HEADS * DA_V_DIM), BF16),
        scratch_shapes=[pltpu.VMEM((2, t, LANE), F32), pltpu.VMEM((2, t, LANE), F32),
                        pltpu.VMEM((2, t, DA_V_DIM), F32), pltpu.VMEM((2, 1, LANE), F32)],
        compiler_params=_cparams(("parallel", "parallel", "arbitrary")),
        name="diff_attention",
    )(lam_params, p_main, p_main, p_main, p_main, norm_g.reshape(1, DA_V_DIM))


def _split3(v):
    hi = v.astype(BF16)
    r1 = v - hi.astype(F32)
    mid = r1.astype(BF16)
    lo = (r1 - mid.astype(F32)).astype(BF16)
    return hi, mid, lo


def _log_sigmoid(v):
    return jnp.minimum(v, 0.0) - jnp.log(1.0 + jnp.exp(-jnp.abs(v)))


def _ml_kernel(qk_ref, v_ref, o_ref, z_ref, gif_ref, gift_ref, cw_ref, cb_ref, gb_ref, gbt_ref, ng_ref,
               y_ref, xbuf, c_st, n_st, m_st, *, L):
    ci = pl.program_id(1)
    pad = 8

    @pl.when(ci == 0)
    def _():
        xbuf[0:pad, :] = jnp.zeros((pad, ML_QK_WIDTH), F32)
        c_st[...] = jnp.zeros(c_st.shape, F32)
        n_st[...] = jnp.zeros(n_st.shape, F32)
        m_st[...] = jnp.zeros(m_st.shape, F32)

    xbuf[pad:pad + L, :] = qk_ref[...].astype(F32)
    cw = cw_ref[...]
    y = xbuf[pad:pad + L, :] * cw[ML_CONV - 1:ML_CONV, :]
    for j in range(ML_CONV - 1):
        sh = ML_CONV - 1 - j
        y = y + xbuf[pad - sh:pad - sh + L, :] * cw[j:j + 1, :]
    y = _silu(y + cb_ref[...])
    xbuf[0:pad, :] = xbuf[L:L + pad, :]

    g_c = gif_ref[...] + gb_ref[...]
    g_r = gift_ref[0] + gbt_ref[...]
    lf_c = _log_sigmoid(g_c)
    lf_r = _log_sigmoid(g_r)
    ti = lax.broadcasted_iota(I32, (L, L), 0)
    si = lax.broadcasted_iota(I32, (L, L), 1)
    causal = si <= ti
    tri = jnp.where(causal, 1.0, 0.0).astype(BF16)
    triu = jnp.where(ti <= si, 1.0, 0.0).astype(BF16)
    bc_all = sum(jnp.dot(tri, piece, preferred_element_type=F32) for piece in _split3(lf_c))
    br_all = sum(jnp.dot(piece, triu, preferred_element_type=F32) for piece in _split3(lf_r))

    for h in range(ML_HEADS):
        q = y[:, h * ML_QK_DIM:(h + 1) * ML_QK_DIM].astype(BF16)
        kf = y[:, ML_HEADS * ML_QK_DIM + h * ML_QK_DIM:ML_HEADS * ML_QK_DIM + (h + 1) * ML_QK_DIM] * (ML_QK_DIM ** -0.5)
        k = kf.astype(BF16)
        v = v_ref[:, h * ML_V_DIM:(h + 1) * ML_V_DIM]
        bc = bc_all[:, ML_HEADS + h:ML_HEADS + h + 1]
        br = br_all[ML_HEADS + h:ML_HEADS + h + 1, :]
        ig_c = g_c[:, h:h + 1]
        ig_r = g_r[h:h + 1, :]
        m_prev = m_st[h][:, 0:1]
        c_prev = c_st[h]
        n_prev = n_st[h]

        dmat = jnp.where(causal, bc - br + ig_r, -jnp.inf)
        inter = bc + m_prev
        m_t = jnp.maximum(jnp.max(dmat, axis=-1, keepdims=True), inter)
        w = jnp.exp(dmat - m_t) * _dot_nt(q, k)
        decay = jnp.exp(inter - m_t)
        num = (jnp.dot(w.astype(BF16), v, preferred_element_type=F32)
               + decay * jnp.dot(q, c_prev.astype(BF16), preferred_element_type=F32))
        qn = jnp.sum(q.astype(F32) * n_prev.astype(BF16).astype(F32), axis=-1, keepdims=True)
        den = jnp.sum(w, axis=-1, keepdims=True) + decay * qn
        hh = num / jnp.maximum(jnp.abs(den), jnp.exp(-m_t))

        b_last = br[:, L - 1:L]
        g_row = b_last - br + ig_r
        m_new = jnp.maximum(b_last + m_prev, jnp.max(g_row, axis=-1, keepdims=True))
        wk_c = jnp.exp(b_last - bc + ig_c - m_new)
        cd = jnp.exp(b_last + m_prev - m_new)
        kw = (kf * wk_c)
        c_st[h] = cd * c_prev + lax.dot_general(kw.astype(BF16), v, (((0,), (0,)), ((), ())),
                                                preferred_element_type=F32)
        n_st[h] = cd * n_prev + jnp.sum(kw, axis=0, keepdims=True)
        m_st[h] = jnp.broadcast_to(m_new, (1, LANE))

        ms = jnp.mean(hh * hh, axis=-1, keepdims=True)
        hn = hh * lax.rsqrt(ms + NORM_EPS) * ng_ref[...]
        sl = slice(h * ML_V_DIM, (h + 1) * ML_V_DIM)
        out = hn * _sigmoid(o_ref[:, sl].astype(F32)) * _silu(z_ref[:, sl].astype(F32))
        y_ref[:, sl] = out.astype(y_ref.dtype)


def _mlstm(p_main, gif, gif_t, conv_w, conv_b, gate_b, norm_g, b, s, L):
    m = b * s
    nc = s // L
    w1k = 1024
    c_qk = MAIN_OFF[SEG_ML_QK] // w1k
    c_v = MAIN_OFF[SEG_ML_V] // w1k
    c_o = MAIN_OFF[SEG_ML_O] // w1k
    c_z = MAIN_OFF[SEG_ML_Z] // w1k
    gb = gate_b.reshape(1, 2 * ML_HEADS)
    kern = functools.partial(_ml_kernel, L=L)
    full = lambda shp: pl.BlockSpec(shp, lambda bi, ci: (0,) * len(shp))
    return pl.pallas_call(
        kern,
        grid=(b, nc),
        in_specs=[
            pl.BlockSpec((L, w1k), lambda bi, ci: (bi * nc + ci, c_qk)),
            pl.BlockSpec((L, w1k), lambda bi, ci: (bi * nc + ci, c_v)),
            pl.BlockSpec((L, w1k), lambda bi, ci: (bi * nc + ci, c_o)),
            pl.BlockSpec((L, w1k), lambda bi, ci: (bi * nc + ci, c_z)),
            pl.BlockSpec((L, 2 * ML_HEADS), lambda bi, ci: (bi * nc + ci, 0)),
            pl.BlockSpec((1, 2 * ML_HEADS, L), lambda bi, ci: (bi, 0, ci)),
            full((ML_CONV, ML_QK_WIDTH)),
            full((1, ML_QK_WIDTH)),
            full((1, 2 * ML_HEADS)),
            full((2 * ML_HEADS, 1)),
            full((1, ML_V_DIM)),
        ],
        out_specs=pl.BlockSpec((L, w1k), lambda bi, ci: (bi * nc + ci, 0)),
        out_shape=jax.ShapeDtypeStruct((m, ML_HEADS * ML_V_DIM), BF16),
        scratch_shapes=[pltpu.VMEM((L + 8, ML_QK_WIDTH), F32),
                        pltpu.VMEM((ML_HEADS, ML_QK_DIM, ML_V_DIM), F32),
                        pltpu.VMEM((ML_HEADS, 1, ML_QK_DIM), F32),
                        pltpu.VMEM((ML_HEADS, 1, LANE), F32)],
        compiler_params=_cparams(("parallel", "arbitrary")),
        name="mlstm",
    )(p_main, p_main, p_main, p_main, gif, gif_t, conv_w, conv_b.reshape(1, ML_QK_WIDTH),
      gb, gb.reshape(2 * ML_HEADS, 1), norm_g.reshape(1, ML_V_DIM))


def _dsa_norm_kernel(p_ref, gkv_ref, gik_ref, c_ref, ik_ref):
    o_kv = SMALL_OFF[SEG_DSA_KV]
    o_ik = SMALL_OFF[SEG_IDX_K]
    ckv = p_ref[:, o_kv:o_kv + DSA_LATENT]
    ms = jnp.mean(ckv * ckv, axis=-1, keepdims=True)
    c_ref[...] = (ckv * lax.rsqrt(ms + NORM_EPS) * gkv_ref[...]).astype(c_ref.dtype)
    ik = p_ref[:, o_ik:o_ik + IDX_DIM]
    ms = jnp.mean(ik * ik, axis=-1, keepdims=True)
    ik_ref[...] = (ik * lax.rsqrt(ms + NORM_EPS) * gik_ref[...]).astype(ik_ref.dtype)


def _dsa_norms(p_small, g_kv, g_ik, tm):
    m = p_small.shape[0]
    return pl.pallas_call(
        _dsa_norm_kernel,
        grid=(m // tm,),
        in_specs=[pl.BlockSpec((tm, SMALL_WIDTH), lambda i: (i, 0)),
                  pl.BlockSpec((1, DSA_LATENT), lambda i: (0, 0)),
                  pl.BlockSpec((1, IDX_DIM), lambda i: (0, 0))],
        out_specs=[pl.BlockSpec((tm, DSA_LATENT), lambda i: (i, 0)),
                   pl.BlockSpec((tm, IDX_DIM), lambda i: (i, 0))],
        out_shape=[jax.ShapeDtypeStruct((m, DSA_LATENT), BF16),
                   jax.ShapeDtypeStruct((m, IDX_DIM), BF16)],
        compiler_params=_cparams(("parallel",)),
        name="dsa_norms",
    )(p_small, g_kv.reshape(1, DSA_LATENT), g_ik.reshape(1, IDX_DIM))


def _dsa_kernel(q_ref, iq_ref, iw_ref, z_ref, c_ref, ik_ref, wuv_ref, o_ref,
                key_ref, m_ref, l_ref, acc_ref, cut_ref, cn_ref, wb_ref, *, tq, tk, topk, nbits_col, s_len):
    qi = pl.program_id(1)
    nkc = (qi * tq + tq + tk - 1) // tk
    row = qi * tq + lax.broadcasted_iota(I32, (tq, tk), 0)
    col0 = lax.broadcasted_iota(I32, (tq, tk), 1)
    nlb = tk // LANE

    iw = iw_ref[...] * ((IDX_HEADS ** -0.5) * (IDX_DIM ** -0.5))
    for h in range(IDX_HEADS):
        wb_ref[h] = jnp.broadcast_to(iw[:, h:h + 1], (tq, LANE))
    iq = iq_ref[0]
    sub = 2 * LANE
    nsub = tk // sub

    def score_body(kc, carry):
        top1, top2 = carry
        off = pl.multiple_of(kc * tk, tk)
        parts = []
        for j in range(nsub):
            ikc = ik_ref[pl.ds(pl.multiple_of(off + j * sub, sub), sub), :]
            sc = _dot_nt(iq, ikc)
            part = None
            for h in range(IDX_HEADS):
                term = jnp.maximum(sc[h * tq:(h + 1) * tq, :], 0.0) * jnp.tile(wb_ref[h], (1, sub // LANE))
                part = term if part is None else part + term
            parts.append(part)
        acc = jnp.concatenate(parts, axis=1)
        bits = lax.bitcast_convert_type(acc, I32)
        skey = jnp.where(bits < 0, (bits ^ 0x7FFFFFFF) + 1, bits)
        skey = jnp.where(col0 + off <= row, skey, INT_MIN)
        key_ref[:, pl.ds(off, tk)] = skey
        for j in range(nlb):
            blk = skey[:, j * LANE:(j + 1) * LANE]
            top2 = jnp.maximum(top2, jnp.minimum(top1, blk))
            top1 = jnp.maximum(top1, blk)
        return top1, top2

    def score_pair(kp, carry):
        return score_body(2 * kp + 1, score_body(2 * kp, carry))

    lowest = jnp.full((tq, LANE), INT_MIN, I32)
    tops = lax.fori_loop(0, nkc // 2, score_pair, (lowest, lowest))
    top1, top2 = lax.cond(nkc % 2 == 1, lambda c: score_body(nkc - 1, c), lambda c: c, tops)

    def count(pred):
        def chunk(kc, cnt):
            off = pl.multiple_of(kc * tk, tk)
            hit = pred(key_ref[:, pl.ds(off, tk)], col0 + off)
            return cnt + _lane_fold(jnp.where(hit, 1, 0))
        cnt = lax.fori_loop(0, nkc // 2, lambda kp, c: chunk(2 * kp + 1, chunk(2 * kp, c)),
                            jnp.zeros((tq, LANE), I32))
        cnt = lax.cond(nkc % 2 == 1, lambda c: chunk(nkc - 1, c), lambda c: c, cnt)
        return jnp.sum(cnt, axis=-1, keepdims=True)

    n_valid = row[:, 0:1] + 1
    lo0 = jnp.maximum(jnp.min(top2, axis=-1, keepdims=True), INT_MIN + 1)
    hi0 = jnp.maximum(jnp.max(top1, axis=-1, keepdims=True), lo0)
    searching0 = n_valid > topk

    def any_row(flag):
        return jnp.max(jnp.where(flag, 1, 0))

    def bis_cond(st):
        return st[0] > 0

    def bis_body(st):
        _, lo, hi, cnt_lo = st
        active = searching0 & (lo < hi) & (cnt_lo != topk)
        mid = (lo >> 1) + (hi >> 1) + ((lo | hi) & 1)
        cnt = count(lambda k, c: k >= mid)
        up = active & (cnt >= topk)
        dn = active & (cnt < topk)
        lo = jnp.where(up, mid, lo)
        cnt_lo = jnp.where(up, cnt, cnt_lo)
        hi = jnp.where(dn, mid - 1, hi)
        return any_row(searching0 & (lo < hi) & (cnt_lo != topk)), lo, hi, cnt_lo

    st0 = (any_row(searching0 & (lo0 < hi0)), lo0, hi0, jnp.full((tq, 1), topk + 1, I32))
    _, thr, _, _ = lax.while_loop(bis_cond, bis_body, st0)
    thr = jnp.where(searching0, thr, INT_MIN + 1)
    cnt_ge = count(lambda k, c: k >= thr)
    has_ties = any_row(searching0 & (cnt_ge > topk)) > 0

    q = q_ref[...]
    qs = jnp.concatenate([q[:, h * DSA_LATENT:(h + 1) * DSA_LATENT] for h in range(DSA_HEADS)], axis=0)
    n_groups = 4
    grows = DSA_HEADS * tq // n_groups

    cut_ref[...] = jnp.full(cut_ref.shape, s_len, I32)

    @pl.when(has_ties)
    def _():
        need = topk - count(lambda k, c: k > thr)

        def cut_body(it, cut):
            cand = cut | jnp.left_shift(jnp.int32(1), nbits_col - 1 - it)
            cnt = count(lambda k, c: (k == thr) & (c < cand))
            return jnp.where(cnt < need, cand, cut)

        cut = lax.fori_loop(0, nbits_col, cut_body, jnp.zeros((tq, 1), I32))
        cut_ref[...] = jnp.broadcast_to(cut + 1, cut_ref.shape)

    tie_cut = cut_ref[:, 0:1]

    @pl.when(qi == 0)
    def _():
        def body(i, mx):
            cf = c_ref[pl.ds(pl.multiple_of(i * tk, tk), tk), :].astype(F32)
            return jnp.maximum(mx, jnp.max(jnp.sum(cf * cf, axis=-1, keepdims=True), axis=0, keepdims=True))
        mx = lax.fori_loop(0, s_len // tk, body, jnp.zeros((1, 1), F32))
        cn_ref[...] = jnp.broadcast_to(jnp.sqrt(mx), cn_ref.shape)

    def selection_bias(kc):
        off = pl.multiple_of(kc * tk, tk)
        keys = key_ref[:, pl.ds(off, tk)]
        sel = (keys > thr) | ((keys == thr) & (col0 + off < tie_cut))
        return off, jnp.where(sel, 0.0, NEG_BIG)

    def group_scores(g, cc, bias):
        s = _dot_nt(qs[g * grows:(g + 1) * grows, :], cc)
        return (s.reshape(grows // tq, tq, tk) + bias[None]).reshape(grows, tk)

    def finalize(l_col):
        z = z_ref[...].astype(F32)
        for h in range(DSA_HEADS):
            rs = slice(h * tq, (h + 1) * tq)
            ob = (acc_ref[rs, :] / l_col[rs, :]).astype(BF16)
            oh = jnp.dot(ob, wuv_ref[h], preferred_element_type=F32)
            sl = slice(h * DSA_V_DIM, (h + 1) * DSA_V_DIM)
            o_ref[:, sl] = (oh * _silu(z[:, sl])).astype(o_ref.dtype)

    qf = qs.astype(F32)
    m_ref[...] = jnp.broadcast_to(jnp.sqrt(jnp.sum(qf * qf, axis=-1, keepdims=True))
                                  * cn_ref[:, 0:1] * BOUND_SLACK, m_ref.shape)
    l_ref[...] = jnp.zeros(l_ref.shape, F32)
    acc_ref[...] = jnp.zeros(acc_ref.shape, F32)

    def bounded_chunk(kc):
        off, bias = selection_bias(kc)
        cc = c_ref[pl.ds(off, tk), :]
        for g in range(n_groups):
            rs = slice(g * grows, (g + 1) * grows)
            p = jnp.exp2(group_scores(g, cc, bias) - jnp.tile(m_ref[rs, :], (1, nlb)))
            l_ref[rs, :] = l_ref[rs, :] + _lane_fold(p)
            acc_ref[rs, :] = acc_ref[rs, :] + jnp.dot(p.astype(BF16), cc, preferred_element_type=F32)

    def bounded_pair(kp, carry):
        bounded_chunk(2 * kp)
        bounded_chunk(2 * kp + 1)
        return carry

    lax.fori_loop(0, nkc // 2, bounded_pair, 0)

    @pl.when(nkc % 2 == 1)
    def _():
        bounded_chunk(nkc - 1)
    l_tot = jnp.sum(l_ref[...], axis=-1, keepdims=True)
    underflow = jnp.logical_not(jnp.min(l_tot) >= UNDERFLOW_GUARD)

    @pl.when(jnp.logical_not(underflow))
    def _():
        finalize(l_tot)

    @pl.when(underflow)
    def _():
        m_ref[...] = jnp.full(m_ref.shape, NEG_BIG, F32)
        l_ref[...] = jnp.zeros(l_ref.shape, F32)
        acc_ref[...] = jnp.zeros(acc_ref.shape, F32)

        def body(kc, carry):
            off, bias = selection_bias(kc)
            cc = c_ref[pl.ds(off, tk), :]
            for g in range(n_groups):
                rs = slice(g * grows, (g + 1) * grows)
                s = group_scores(g, cc, bias)
                m_prev = m_ref[rs, :]
                m_new = jnp.maximum(m_prev, jnp.max(s, axis=-1, keepdims=True))
                alpha = jnp.exp2(m_prev - m_new)
                p = jnp.exp2(s - jnp.tile(m_new, (1, nlb)))
                l_ref[rs, :] = alpha * l_ref[rs, :] + jnp.sum(p, axis=-1, keepdims=True)
                acc_ref[rs, :] = (jnp.tile(alpha, (1, DSA_LATENT // LANE)) * acc_ref[rs, :]
                                  + jnp.dot(p.astype(BF16), cc, preferred_element_type=F32))
                m_ref[rs, :] = m_new
            return carry

        lax.fori_loop(0, nkc, body, 0)
        finalize(l_ref[:, 0:1])


def _dsa_attention(p_main, iw, c_lat, ik_n, w_uv, b, s, tq, tk):
    m = b * s
    nq = s // tq
    topk = min(IDX_TOPK_MAX, s // 4)
    assert topk <= 2 * LANE, "the per-lane top-two start of the threshold search covers 2 * LANE keys"
    c_q = MAIN_OFF[SEG_DSA_Q] // (DSA_HEADS * DSA_LATENT)
    c_z = MAIN_OFF[SEG_DSA_Z] // 1024
    o_iq = MAIN_OFF[SEG_IDX_Q]
    iq_hm = p_main[:, o_iq:o_iq + IDX_HEADS * IDX_DIM].reshape(b * nq, tq, IDX_HEADS, IDX_DIM)
    iq_hm = jnp.transpose(iq_hm, (0, 2, 1, 3)).reshape(b * nq, IDX_HEADS * tq, IDX_DIM)
    nbits_col = max(1, int(math.ceil(math.log2(s))) + 1)
    kern = functools.partial(_dsa_kernel, tq=tq, tk=tk, topk=topk, nbits_col=nbits_col, s_len=s)
    return pl.pallas_call(
        kern,
        grid=(b, nq),
        in_specs=[
            pl.BlockSpec((tq, DSA_HEADS * DSA_LATENT), lambda bi, qi: (bi * nq + qi, c_q)),
            pl.BlockSpec((1, IDX_HEADS * tq, IDX_DIM), lambda bi, qi: (bi * nq + qi, 0, 0)),
            pl.BlockSpec((tq, IDX_HEADS), lambda bi, qi: (bi * nq + qi, 0)),
            pl.BlockSpec((tq, 1024), lambda bi, qi: (bi * nq + qi, c_z)),
            pl.BlockSpec((s, DSA_LATENT), lambda bi, qi: (bi, 0)),
            pl.BlockSpec((s, IDX_DIM), lambda bi, qi: (bi, 0)),
            pl.BlockSpec((DSA_HEADS, DSA_LATENT, DSA_V_DIM), lambda bi, qi: (0, 0, 0)),
        ],
        out_specs=pl.BlockSpec((tq, DSA_HEADS * DSA_V_DIM), lambda bi, qi: (bi * nq + qi, 0)),
        out_shape=jax.ShapeDtypeStruct((m, DSA_HEADS * DSA_V_DIM), BF16),
        scratch_shapes=[pltpu.VMEM((tq, s), I32),
                        pltpu.VMEM((DSA_HEADS * tq, LANE), F32),
                        pltpu.VMEM((DSA_HEADS * tq, LANE), F32),
                        pltpu.VMEM((DSA_HEADS * tq, DSA_LATENT), F32),
                        pltpu.VMEM((tq, LANE), I32),
                        pltpu.VMEM((1, LANE), F32),
                        pltpu.VMEM((IDX_HEADS, tq, LANE), F32)],
        compiler_params=_cparams(("parallel", "arbitrary")),
        name="dsa_attention",
    )(p_main, iq_hm, iw, p_main, c_lat, ik_n, w_uv)


def _out_kernel(x_ref, ya_ref, yb_ref, yc_ref, ga_ref, gb_ref, gc_ref, wb_ref, wo_ref, pg_ref, o_ref):
    mixed = None
    for i, (y_ref, g_ref) in enumerate(((ya_ref, ga_ref), (yb_ref, gb_ref), (yc_ref, gc_ref))):
        t = _sigmoid(g_ref[...].astype(F32)) * jnp.dot(y_ref[...], wb_ref[i], preferred_element_type=F32)
        mixed = t if mixed is None else mixed + t
    out = jnp.dot(mixed.astype(BF16), wo_ref[...], preferred_element_type=F32)
    ms = jnp.mean(out * out, axis=-1, keepdims=True)
    o_ref[...] = x_ref[...] + out * lax.rsqrt(ms + NORM_EPS) * pg_ref[...]


def _merge_out(x2, ya, yb, yc, p_main, w_branch, w_out, post_g, tm):
    m, d = x2.shape
    cg = MAIN_OFF[SEG_GATES] // d
    row = lambda i: (i, 0)
    return pl.pallas_call(
        _out_kernel,
        grid=(m // tm,),
        in_specs=[
            pl.BlockSpec((tm, d), row),
            pl.BlockSpec((tm, BRANCH_WIDTH), row),
            pl.BlockSpec((tm, BRANCH_WIDTH), row),
            pl.BlockSpec((tm, BRANCH_WIDTH), row),
            pl.BlockSpec((tm, d), lambda i: (i, cg)),
            pl.BlockSpec((tm, d), lambda i: (i, cg + 1)),
            pl.BlockSpec((tm, d), lambda i: (i, cg + 2)),
            pl.BlockSpec((N_BRANCHES, BRANCH_WIDTH, d), lambda i: (0, 0, 0), pipeline_mode=pl.Buffered(1)),
            pl.BlockSpec((d, d), lambda i: (0, 0), pipeline_mode=pl.Buffered(1)),
            pl.BlockSpec((1, d), lambda i: (0, 0)),
        ],
        out_specs=pl.BlockSpec((tm, d), row),
        out_shape=jax.ShapeDtypeStruct((m, d), F32),
        compiler_params=_cparams(("parallel",)),
        name="merge_out_proj",
    )(x2, ya, yb, yc, p_main, p_main, p_main, w_branch, w_out, post_g.reshape(1, d))


def _pick(total, want):
    t = min(total, want)
    while total % t:
        t //= 2
    return t


def kernel(x, pre_norm_g, w_in, da_lambda, da_norm_g, ml_conv_w, ml_conv_b, ml_gate_b, ml_norm_g,
           dsa_kv_norm_g, dsa_ik_norm_g, dsa_w_uv, w_branch, w_out, post_norm_g):
    b, s, d = x.shape
    m = b * s
    bounds = _seg_bounds()
    x2 = x.reshape(m, d)
    tm_proj = _pick(m, 1024)
    t_da = _pick(s, 512)
    l_ml = _pick(s, 256)
    tq_dsa = _pick(s, 128)
    tk_dsa = _pick(s, 512)
    o_if = SMALL_OFF[SEG_ML_IF]
    o_iw = SMALL_OFF[SEG_IDX_W]
    for l in range(DEPTH):
        w = w_in[l]
        col_scale = {SEG_DA_Q: DA_HEAD_DIM ** -0.5 * LOG2E, SEG_DSA_Q: DSA_LATENT ** -0.5 * LOG2E}
        w_main = jnp.concatenate([w[:, bounds[sg]:bounds[sg + 1]] * col_scale.get(sg, 1.0) for sg in MAIN_SEGS],
                                 axis=1).astype(BF16)
        w_small = jnp.concatenate([w[:, bounds[sg]:bounds[sg + 1]] for sg in SMALL_SEGS]
                                  + [jnp.zeros((d, SMALL_WIDTH - SMALL_USED), w.dtype)], axis=1).astype(BF16)
        p_main = _project(x2, pre_norm_g[l], w_main, BF16, tm_proj, 512)
        p_small = _project(x2, pre_norm_g[l], w_small, F32, tm_proj, SMALL_WIDTH)

        lam_init = 0.8 - 0.6 * math.exp(-0.3 * l)
        y_a = _diff_attention(p_main, da_lambda[l], da_norm_g[l], lam_init, b, s, t_da)

        gif = p_small[:, o_if:o_if + 2 * ML_HEADS]
        gif_t = jnp.transpose(gif.reshape(b, s, 2 * ML_HEADS), (0, 2, 1))
        y_b = _mlstm(p_main, gif, gif_t, ml_conv_w[l], ml_conv_b[l], ml_gate_b[l], ml_norm_g[l], b, s, l_ml)

        c_lat, ik_n = _dsa_norms(p_small, dsa_kv_norm_g[l], dsa_ik_norm_g[l], _pick(m, 2048))
        iw = p_small[:, o_iw:o_iw + IDX_HEADS]
        y_c = _dsa_attention(p_main, iw, c_lat, ik_n, dsa_w_uv[l].astype(BF16), b, s, tq_dsa, tk_dsa)

        x2 = _merge_out(x2, y_a, y_b, y_c, p_main, w_branch[l].astype(BF16), w_out[l].astype(BF16),
                        post_norm_g[l], _pick(m, 256))
    return x2.reshape(b, s, d)
```

```python
import functools
import math

import jax
import jax.numpy as jnp
from jax import lax
from jax.experimental import pallas as pl
from jax.experimental.pallas import tpu as pltpu

F32 = jnp.float32
BF16 = jnp.bfloat16
I32 = jnp.int32

D_MODEL = 2048
DEPTH = 2
BRANCH_WIDTH = 1024
N_BRANCHES = 3
NORM_EPS = 1e-6
DA_HEADS = 8
DA_HEAD_DIM = 64
DA_V_DIM = 128
ML_HEADS = 4
ML_QK_DIM = 128
ML_V_DIM = 256
ML_CONV = 4
ML_QK_WIDTH = 2 * ML_HEADS * ML_QK_DIM
DSA_HEADS = 8
DSA_LATENT = 256
DSA_V_DIM = 128
IDX_HEADS = 16
IDX_DIM = 64
IDX_TOPK_MAX = 256

IN_SPLITS = (
    1024, 1024, 1024, 1024,
    ML_QK_WIDTH, 1024, 2 * ML_HEADS,
    1024, 1024,
    DSA_HEADS * DSA_LATENT, DSA_LATENT,
    IDX_HEADS * IDX_DIM, IDX_DIM, IDX_HEADS,
    1024,
    N_BRANCHES * D_MODEL,
)
(SEG_DA_Q, SEG_DA_K, SEG_DA_V, SEG_DA_Z, SEG_ML_QK, SEG_ML_V, SEG_ML_IF, SEG_ML_O, SEG_ML_Z,
 SEG_DSA_Q, SEG_DSA_KV, SEG_IDX_Q, SEG_IDX_K, SEG_IDX_W, SEG_DSA_Z, SEG_GATES) = range(16)

MAIN_SEGS = (SEG_DA_Q, SEG_DA_K, SEG_DA_V, SEG_DA_Z, SEG_ML_QK, SEG_ML_V, SEG_ML_O, SEG_ML_Z,
             SEG_DSA_Q, SEG_IDX_Q, SEG_DSA_Z, SEG_GATES)
SMALL_SEGS = (SEG_DSA_KV, SEG_IDX_K, SEG_IDX_W, SEG_ML_IF)
SMALL_WIDTH = 384

LANE = 128
VMEM_LIMIT = 56 * 1024 * 1024
INT_MIN = -2147483648
NEG_BIG = -1e30
LOG2E = math.log2(math.e)
DA_UNROLL = 4
DSA_UNROLL = 4
BOUND_SLACK = 1.01
UNDERFLOW_GUARD = 2.0 ** -100


def _seg_bounds():
    offs = [0]
    for n in IN_SPLITS:
        offs.append(offs[-1] + n)
    return offs


def _main_offsets():
    out, off = {}, 0
    for s in MAIN_SEGS:
        out[s] = off
        off += IN_SPLITS[s]
    return out, off


def _small_offsets():
    out, off = {}, 0
    for s in SMALL_SEGS:
        out[s] = off
        off += IN_SPLITS[s]
    return out, off


MAIN_OFF, MAIN_WIDTH = _main_offsets()
SMALL_OFF, SMALL_USED = _small_offsets()


def _cparams(sem):
    return pltpu.CompilerParams(dimension_semantics=sem, vmem_limit_bytes=VMEM_LIMIT)


def _silu(v):
    return v * (1.0 / (1.0 + jnp.exp(-v)))


def _sigmoid(v):
    return 1.0 / (1.0 + jnp.exp(-v))


def _dot_nt(a, b):
    return lax.dot_general(a, b, (((1,), (1,)), ((), ())), preferred_element_type=F32)


def _proj_kernel(x_ref, g_ref, w_ref, o_ref, h_ref):
    @pl.when(pl.program_id(1) == 0)
    def _():
        x = x_ref[...]
        ms = jnp.mean(x * x, axis=-1, keepdims=True)
        h_ref[...] = (x * lax.rsqrt(ms + NORM_EPS) * g_ref[...]).astype(BF16)

    o_ref[...] = jnp.dot(h_ref[...], w_ref[...], preferred_element_type=F32).astype(o_ref.dtype)


def _project(x2, g, w, out_dtype, tm, tn):
    m, d = x2.shape
    n = w.shape[1]
    return pl.pallas_call(
        _proj_kernel,
        grid=(m // tm, n // tn),
        in_specs=[pl.BlockSpec((tm, d), lambda i, j: (i, 0)),
                  pl.BlockSpec((1, d), lambda i, j: (0, 0)),
                  pl.BlockSpec((d, tn), lambda i, j: (0, j))],
        out_specs=pl.BlockSpec((tm, tn), lambda i, j: (i, j)),
        out_shape=jax.ShapeDtypeStruct((m, n), out_dtype),
        scratch_shapes=[pltpu.VMEM((tm, d), BF16)],
        compiler_params=_cparams(("parallel", "arbitrary")),
        name="rmsnorm_in_proj",
    )(x2, g.reshape(1, d), w)


def _lane_fold(x):
    part = x[:, 0:LANE]
    for j in range(1, x.shape[1] // LANE):
        part = part + x[:, j * LANE:(j + 1) * LANE]
    return part


def _da_kernel(lam_ref, q_ref, k_ref, v_ref, z_ref, g_ref, o_ref, m_ref, l_ref, acc_ref, kn_ref,
               *, t, s_len, lam_init):
    qi = pl.program_id(2)
    nrep = t // LANE
    half = [slice(c * DA_HEAD_DIM, (c + 1) * DA_HEAD_DIM) for c in range(2)]

    @pl.when(qi == 0)
    def _():
        def body(i, mx):
            kk = k_ref[pl.ds(pl.multiple_of(i * t, t), t), :].astype(F32)
            sq = kk * kk
            return tuple(jnp.maximum(mx[c], jnp.max(jnp.sum(sq[:, half[c]], axis=-1, keepdims=True),
                                                    axis=0, keepdims=True)) for c in range(2))
        mx = lax.fori_loop(0, s_len // t, body, (jnp.zeros((1, 1), F32), jnp.zeros((1, 1), F32)))
        for c in range(2):
            kn_ref[c] = jnp.broadcast_to(jnp.sqrt(mx[c]), (1, LANE))

    q = q_ref[...]
    qsq = q.astype(F32)
    qsq = qsq * qsq
    bound = [jnp.broadcast_to(jnp.sqrt(jnp.sum(qsq[:, half[c]], axis=-1, keepdims=True))
                              * kn_ref[c][:, 0:1] * BOUND_SLACK, (t, LANE)) for c in range(2)]

    def causal_keep():
        row = lax.broadcasted_iota(I32, (t, t), 0)
        col = lax.broadcasted_iota(I32, (t, t), 1)
        return col <= row

    def sweep(block):
        def body(kp, carry):
            for u in range(DA_UNROLL):
                block(pl.multiple_of(kp * (DA_UNROLL * t) + u * t, t), False)
            return carry
        lax.fori_loop(0, qi // DA_UNROLL, body, 0)

        def tail(ki, carry):
            block(pl.multiple_of(ki * t, t), False)
            return carry
        lax.fori_loop((qi // DA_UNROLL) * DA_UNROLL, qi, tail, 0)

        block(pl.multiple_of(qi * t, t), True)

    def bounded_block(off, masked):
        k = k_ref[pl.ds(off, t), :]
        v = v_ref[pl.ds(off, t), :]
        for c in range(2):
            s = _dot_nt(q[:, half[c]], k[:, half[c]])
            if masked:
                s = jnp.where(causal_keep(), s, -jnp.inf)
            p = jnp.exp2(s - jnp.tile(bound[c], (1, nrep)))
            l_ref[c] = l_ref[c] + _lane_fold(p)
            acc_ref[c] = acc_ref[c] + jnp.dot(p.astype(BF16), v, preferred_element_type=F32)

    def running_max_block(off, masked):
        k = k_ref[pl.ds(off, t), :]
        v = v_ref[pl.ds(off, t), :]
        for c in range(2):
            s = _dot_nt(q[:, half[c]], k[:, half[c]])
            if masked:
                s = jnp.where(causal_keep(), s, -jnp.inf)
            m_prev = m_ref[c]
            m_new = jnp.maximum(m_prev, jnp.max(s, axis=-1, keepdims=True))
            alpha = jnp.exp2(m_prev - m_new)
            p = jnp.exp2(s - jnp.tile(m_new, (1, nrep)))
            l_ref[c] = alpha * l_ref[c] + jnp.sum(p, axis=-1, keepdims=True)
            acc_ref[c] = alpha * acc_ref[c] + jnp.dot(p.astype(BF16), v, preferred_element_type=F32)
            m_ref[c] = m_new

    l_ref[...] = jnp.zeros(l_ref.shape, F32)
    acc_ref[...] = jnp.zeros(acc_ref.shape, F32)
    sweep(bounded_block)
    l_tot = [jnp.sum(l_ref[c], axis=-1, keepdims=True) for c in range(2)]
    underflow = jnp.logical_not(jnp.min(jnp.minimum(l_tot[0], l_tot[1])) >= UNDERFLOW_GUARD)

    @pl.when(jnp.logical_not(underflow))
    def _():
        for c in range(2):
            acc_ref[c] = acc_ref[c] / l_tot[c]

    @pl.when(underflow)
    def _():
        m_ref[...] = jnp.full(m_ref.shape, -jnp.inf, F32)
        l_ref[...] = jnp.zeros(l_ref.shape, F32)
        acc_ref[...] = jnp.zeros(acc_ref.shape, F32)
        sweep(running_max_block)
        for c in range(2):
            acc_ref[c] = acc_ref[c] / l_ref[c]

    lp = lam_ref[...]
    lam = (jnp.exp(jnp.sum(lp[0:1] * lp[1:2], axis=-1, keepdims=True))
           - jnp.exp(jnp.sum(lp[2:3] * lp[3:4], axis=-1, keepdims=True)) + lam_init)
    o = acc_ref[0] - lam * acc_ref[1]
    ms = jnp.mean(o * o, axis=-1, keepdims=True)
    o = o * lax.rsqrt(ms + NORM_EPS) * g_ref[...] * (1.0 - lam_init)
    o_ref[...] = (o * _silu(z_ref[...].astype(F32))).astype(o_ref.dtype)


def _diff_attention(p_main, lam_params, norm_g, lam_init, b, s, t):
    m = b * s
    nq = s // t
    cq = MAIN_OFF[SEG_DA_Q] // LANE
    ck = MAIN_OFF[SEG_DA_K] // LANE
    cv = MAIN_OFF[SEG_DA_V] // LANE
    cz = MAIN_OFF[SEG_DA_Z] // LANE
    kern = functools.partial(_da_kernel, t=t, s_len=s, lam_init=lam_init)
    return pl.pallas_call(
        kern,
        grid=(b, DA_HEADS, nq),
        in_specs=[
            pl.BlockSpec((4, DA_HEAD_DIM), lambda bi, h, qi: (0, 0)),
            pl.BlockSpec((t, LANE), lambda bi, h, qi: (bi * nq + qi, cq + h)),
            pl.BlockSpec((s, LANE), lambda bi, h, qi: (bi, ck + h)),
            pl.BlockSpec((s, LANE), lambda bi, h, qi: (bi, cv + h)),
            pl.BlockSpec((t, LANE), lambda bi, h, qi: (bi * nq + qi, cz + h)),
            pl.BlockSpec((1, DA_V_DIM), lambda bi, h, qi: (0, 0)),
        ],
        out_specs=pl.BlockSpec((t, LANE), lambda bi, h, qi: (bi * nq + qi, h)),
        out_shape=jax.ShapeDtypeStruct((m, DA_HEADS * DA_V_DIM), BF16),
        scratch_shapes=[pltpu.VMEM((2, t, LANE), F32), pltpu.VMEM((2, t, LANE), F32),
                        pltpu.VMEM((2, t, DA_V_DIM), F32), pltpu.VMEM((2, 1, LANE), F32)],
        compiler_params=_cparams(("parallel", "parallel", "arbitrary")),
        name="diff_attention",
    )(lam_params, p_main, p_main, p_main, p_main, norm_g.reshape(1, DA_V_DIM))


def _split3(v):
    hi = v.astype(BF16)
    r1 = v - hi.astype(F32)
    mid = r1.astype(BF16)
    lo = (r1 - mid.astype(F32)).astype(BF16)
    return hi, mid, lo


def _log_sigmoid(v):
    return jnp.minimum(v, 0.0) - jnp.log(1.0 + jnp.exp(-jnp.abs(v)))


def _ml_kernel(qk_ref, v_ref, o_ref, z_ref, gif_ref, gift_ref, cw_ref, cb_ref, gb_ref, gbt_ref, ng_ref,
               y_ref, xbuf, c_st, n_st, m_st, *, L):
    ci = pl.program_id(1)
    pad = 8

    @pl.when(ci == 0)
    def _():
        xbuf[0:pad, :] = jnp.zeros((pad, ML_QK_WIDTH), F32)
        c_st[...] = jnp.zeros(c_st.shape, F32)
        n_st[...] = jnp.zeros(n_st.shape, F32)
        m_st[...] = jnp.zeros(m_st.shape, F32)

    xbuf[pad:pad + L, :] = qk_ref[...].astype(F32)
    cw = cw_ref[...]
    y = xbuf[pad:pad + L, :] * cw[ML_CONV - 1:ML_CONV, :]
    for j in range(ML_CONV - 1):
        sh = ML_CONV - 1 - j
        y = y + xbuf[pad - sh:pad - sh + L, :] * cw[j:j + 1, :]
    y = _silu(y + cb_ref[...])
    xbuf[0:pad, :] = xbuf[L:L + pad, :]

    g_c = gif_ref[...] + gb_ref[...]
    g_r = gift_ref[0] + gbt_ref[...]
    lf_c = _log_sigmoid(g_c)
    lf_r = _log_sigmoid(g_r)
    ti = lax.broadcasted_iota(I32, (L, L), 0)
    si = lax.broadcasted_iota(I32, (L, L), 1)
    causal = si <= ti
    tri = jnp.where(causal, 1.0, 0.0).astype(BF16)
    triu = jnp.where(ti <= si, 1.0, 0.0).astype(BF16)
    bc_all = sum(jnp.dot(tri, piece, preferred_element_type=F32) for piece in _split3(lf_c))
    br_all = sum(jnp.dot(piece, triu, preferred_element_type=F32) for piece in _split3(lf_r))

    for h in range(ML_HEADS):
        q = y[:, h * ML_QK_DIM:(h + 1) * ML_QK_DIM].astype(BF16)
        kf = y[:, ML_HEADS * ML_QK_DIM + h * ML_QK_DIM:ML_HEADS * ML_QK_DIM + (h + 1) * ML_QK_DIM] * (ML_QK_DIM ** -0.5)
        k = kf.astype(BF16)
        v = v_ref[:, h * ML_V_DIM:(h + 1) * ML_V_DIM]
        bc = bc_all[:, ML_HEADS + h:ML_HEADS + h + 1]
        br = br_all[ML_HEADS + h:ML_HEADS + h + 1, :]
        ig_c = g_c[:, h:h + 1]
        ig_r = g_r[h:h + 1, :]
        m_prev = m_st[h][:, 0:1]
        c_prev = c_st[h]
        n_prev = n_st[h]

        dmat = jnp.where(causal, bc - br + ig_r, -jnp.inf)
        inter = bc + m_prev
        m_t = jnp.maximum(jnp.max(dmat, axis=-1, keepdims=True), inter)
        w = jnp.exp(dmat - m_t) * _dot_nt(q, k)
        decay = jnp.exp(inter - m_t)
        num = (jnp.dot(w.astype(BF16), v, preferred_element_type=F32)
               + decay * jnp.dot(q, c_prev.astype(BF16), preferred_element_type=F32))
        qn = jnp.sum(q.astype(F32) * n_prev.astype(BF16).astype(F32), axis=-1, keepdims=True)
        den = jnp.sum(w, axis=-1, keepdims=True) + decay * qn
        hh = num / jnp.maximum(jnp.abs(den), jnp.exp(-m_t))

        b_last = br[:, L - 1:L]
        g_row = b_last - br + ig_r
        m_new = jnp.maximum(b_last + m_prev, jnp.max(g_row, axis=-1, keepdims=True))
        wk_c = jnp.exp(b_last - bc + ig_c - m_new)
        cd = jnp.exp(b_last + m_prev - m_new)
        kw = (kf * wk_c)
        c_st[h] = cd * c_prev + lax.dot_general(kw.astype(BF16), v, (((0,), (0,)), ((), ())),
                                                preferred_element_type=F32)
        n_st[h] = cd * n_prev + jnp.sum(kw, axis=0, keepdims=True)
        m_st[h] = jnp.broadcast_to(m_new, (1, LANE))

        ms = jnp.mean(hh * hh, axis=-1, keepdims=True)
        hn = hh * lax.rsqrt(ms + NORM_EPS) * ng_ref[...]
        sl = slice(h * ML_V_DIM, (h + 1) * ML_V_DIM)
        out = hn * _sigmoid(o_ref[:, sl].astype(F32)) * _silu(z_ref[:, sl].astype(F32))
        y_ref[:, sl] = out.astype(y_ref.dtype)


def _mlstm(p_main, gif, gif_t, conv_w, conv_b, gate_b, norm_g, b, s, L):
    m = b * s
    nc = s // L
    w1k = 1024
    c_qk = MAIN_OFF[SEG_ML_QK] // w1k
    c_v = MAIN_OFF[SEG_ML_V] // w1k
    c_o = MAIN_OFF[SEG_ML_O] // w1k
    c_z = MAIN_OFF[SEG_ML_Z] // w1k
    gb = gate_b.reshape(1, 2 * ML_HEADS)
    kern = functools.partial(_ml_kernel, L=L)
    full = lambda shp: pl.BlockSpec(shp, lambda bi, ci: (0,) * len(shp))
    return pl.pallas_call(
        kern,
        grid=(b, nc),
        in_specs=[
            pl.BlockSpec((L, w1k), lambda bi, ci: (bi * nc + ci, c_qk)),
            pl.BlockSpec((L, w1k), lambda bi, ci: (bi * nc + ci, c_v)),
            pl.BlockSpec((L, w1k), lambda bi, ci: (bi * nc + ci, c_o)),
            pl.BlockSpec((L, w1k), lambda bi, ci: (bi * nc + ci, c_z)),
            pl.BlockSpec((L, 2 * ML_HEADS), lambda bi, ci: (bi * nc + ci, 0)),
            pl.BlockSpec((1, 2 * ML_HEADS, L), lambda bi, ci: (bi, 0, ci)),
            full((ML_CONV, ML_QK_WIDTH)),
            full((1, ML_QK_WIDTH)),
            full((1, 2 * ML_HEADS)),
            full((2 * ML_HEADS, 1)),
            full((1, ML_V_DIM)),
        ],
        out_specs=pl.BlockSpec((L, w1k), lambda bi, ci: (bi * nc + ci, 0)),
        out_shape=jax.ShapeDtypeStruct((m, ML_HEADS * ML_V_DIM), BF16),
        scratch_shapes=[pltpu.VMEM((L + 8, ML_QK_WIDTH), F32),
                        pltpu.VMEM((ML_HEADS, ML_QK_DIM, ML_V_DIM), F32),
                        pltpu.VMEM((ML_HEADS, 1, ML_QK_DIM), F32),
                        pltpu.VMEM((ML_HEADS, 1, LANE), F32)],
        compiler_params=_cparams(("parallel", "arbitrary")),
        name="mlstm",
    )(p_main, p_main, p_main, p_main, gif, gif_t, conv_w, conv_b.reshape(1, ML_QK_WIDTH),
      gb, gb.reshape(2 * ML_HEADS, 1), norm_g.reshape(1, ML_V_DIM))


def _dsa_norm_kernel(p_ref, gkv_ref, gik_ref, c_ref, ik_ref):
    o_kv = SMALL_OFF[SEG_DSA_KV]
    o_ik = SMALL_OFF[SEG_IDX_K]
    ckv = p_ref[:, o_kv:o_kv + DSA_LATENT]
    ms = jnp.mean(ckv * ckv, axis=-1, keepdims=True)
    c_ref[...] = (ckv * lax.rsqrt(ms + NORM_EPS) * gkv_ref[...]).astype(c_ref.dtype)
    ik = p_ref[:, o_ik:o_ik + IDX_DIM]
    ms = jnp.mean(ik * ik, axis=-1, keepdims=True)
    ik_ref[...] = (ik * lax.rsqrt(ms + NORM_EPS) * gik_ref[...]).astype(ik_ref.dtype)


def _dsa_norms(p_small, g_kv, g_ik, tm):
    m = p_small.shape[0]
    return pl.pallas_call(
        _dsa_norm_kernel,
        grid=(m // tm,),
        in_specs=[pl.BlockSpec((tm, SMALL_WIDTH), lambda i: (i, 0)),
                  pl.BlockSpec((1, DSA_LATENT), lambda i: (0, 0)),
                  pl.BlockSpec((1, IDX_DIM), lambda i: (0, 0))],
        out_specs=[pl.BlockSpec((tm, DSA_LATENT), lambda i: (i, 0)),
                   pl.BlockSpec((tm, IDX_DIM), lambda i: (i, 0))],
        out_shape=[jax.ShapeDtypeStruct((m, DSA_LATENT), BF16),
                   jax.ShapeDtypeStruct((m, IDX_DIM), BF16)],
        compiler_params=_cparams(("parallel",)),
        name="dsa_norms",
    )(p_small, g_kv.reshape(1, DSA_LATENT), g_ik.reshape(1, IDX_DIM))


def _dsa_kernel(q_ref, iq_ref, iw_ref, z_ref, c_ref, ik_ref, wuv_ref, o_ref,
                key_ref, m_ref, l_ref, acc_ref, cut_ref, cn_ref, wb_ref, *, tq, tk, topk, nbits_col, s_len):
    qi = pl.program_id(1)
    nkc = (qi * tq + tq + tk - 1) // tk
    row = qi * tq + lax.broadcasted_iota(I32, (tq, tk), 0)
    col0 = lax.broadcasted_iota(I32, (tq, tk), 1)
    nlb = tk // LANE

    iw = iw_ref[...] * ((IDX_HEADS ** -0.5) * (IDX_DIM ** -0.5))
    for h in range(IDX_HEADS):
        wb_ref[h] = jnp.broadcast_to(iw[:, h:h + 1], (tq, LANE))
    iq = iq_ref[0]
    sub = 2 * LANE
    nsub = tk // sub

    def score_body(kc, carry):
        top1, top2 = carry
        off = pl.multiple_of(kc * tk, tk)
        parts = []
        for j in range(nsub):
            ikc = ik_ref[pl.ds(pl.multiple_of(off + j * sub, sub), sub), :]
            sc = _dot_nt(iq, ikc)
            part = None
            for h in range(IDX_HEADS):
                term = jnp.maximum(sc[h * tq:(h + 1) * tq, :], 0.0) * jnp.tile(wb_ref[h], (1, sub // LANE))
                part = term if part is None else part + term
            parts.append(part)
        acc = jnp.concatenate(parts, axis=1)
        bits = lax.bitcast_convert_type(acc, I32)
        skey = jnp.where(bits < 0, (bits ^ 0x7FFFFFFF) + 1, bits)
        skey = jnp.where(col0 + off <= row, skey, INT_MIN)
        key_ref[:, pl.ds(off, tk)] = skey
        for j in range(nlb):
            blk = skey[:, j * LANE:(j + 1) * LANE]
            top2 = jnp.maximum(top2, jnp.minimum(top1, blk))
            top1 = jnp.maximum(top1, blk)
        return top1, top2

    def score_group(kp, carry):
        for u in range(DSA_UNROLL):
            carry = score_body(kp * DSA_UNROLL + u, carry)
        return carry

    lowest = jnp.full((tq, LANE), INT_MIN, I32)
    tops = lax.fori_loop(0, nkc // DSA_UNROLL, score_group, (lowest, lowest))
    top1, top2 = lax.fori_loop((nkc // DSA_UNROLL) * DSA_UNROLL, nkc, score_body, tops)

    def count(pred):
        def chunk(kc, cnt):
            off = pl.multiple_of(kc * tk, tk)
            hit = pred(key_ref[:, pl.ds(off, tk)], col0 + off)
            return cnt + _lane_fold(jnp.where(hit, 1, 0))
        cnt = lax.fori_loop(0, nkc // 2, lambda kp, c: chunk(2 * kp + 1, chunk(2 * kp, c)),
                            jnp.zeros((tq, LANE), I32))
        cnt = lax.cond(nkc % 2 == 1, lambda c: chunk(nkc - 1, c), lambda c: c, cnt)
        return jnp.sum(cnt, axis=-1, keepdims=True)

    n_valid = row[:, 0:1] + 1
    lo0 = jnp.maximum(jnp.min(top2, axis=-1, keepdims=True), INT_MIN + 1)
    hi0 = jnp.maximum(jnp.max(top1, axis=-1, keepdims=True), lo0)
    searching0 = n_valid > topk

    def any_row(flag):
        return jnp.max(jnp.where(flag, 1, 0))

    def bis_cond(st):
        return st[0] > 0

    def bis_body(st):
        _, lo, hi, cnt_lo = st
        active = searching0 & (lo < hi) & (cnt_lo != topk)
        mid = (lo >> 1) + (hi >> 1) + ((lo | hi) & 1)
        cnt = count(lambda k, c: k >= mid)
        up = active & (cnt >= topk)
        dn = active & (cnt < topk)
        lo = jnp.where(up, mid, lo)
        cnt_lo = jnp.where(up, cnt, cnt_lo)
        hi = jnp.where(dn, mid - 1, hi)
        return any_row(searching0 & (lo < hi) & (cnt_lo != topk)), lo, hi, cnt_lo

    st0 = (any_row(searching0 & (lo0 < hi0)), lo0, hi0, jnp.full((tq, 1), topk + 1, I32))
    _, thr, _, _ = lax.while_loop(bis_cond, bis_body, st0)
    thr = jnp.where(searching0, thr, INT_MIN + 1)
    cnt_ge = count(lambda k, c: k >= thr)
    has_ties = any_row(searching0 & (cnt_ge > topk)) > 0

    q = q_ref[...]
    qs = jnp.concatenate([q[:, h * DSA_LATENT:(h + 1) * DSA_LATENT] for h in range(DSA_HEADS)], axis=0)
    n_groups = 4
    grows = DSA_HEADS * tq // n_groups

    cut_ref[...] = jnp.full(cut_ref.shape, s_len, I32)

    @pl.when(has_ties)
    def _():
        need = topk - count(lambda k, c: k > thr)

        def cut_body(it, cut):
            cand = cut | jnp.left_shift(jnp.int32(1), nbits_col - 1 - it)
            cnt = count(lambda k, c: (k == thr) & (c < cand))
            return jnp.where(cnt < need, cand, cut)

        cut = lax.fori_loop(0, nbits_col, cut_body, jnp.zeros((tq, 1), I32))
        cut_ref[...] = jnp.broadcast_to(cut + 1, cut_ref.shape)

    tie_cut = cut_ref[:, 0:1]

    @pl.when(qi == 0)
    def _():
        def body(i, mx):
            cf = c_ref[pl.ds(pl.multiple_of(i * tk, tk), tk), :].astype(F32)
            return jnp.maximum(mx, jnp.max(jnp.sum(cf * cf, axis=-1, keepdims=True), axis=0, keepdims=True))
        mx = lax.fori_loop(0, s_len // tk, body, jnp.zeros((1, 1), F32))
        cn_ref[...] = jnp.broadcast_to(jnp.sqrt(mx), cn_ref.shape)

    def selection_bias(kc):
        off = pl.multiple_of(kc * tk, tk)
        keys = key_ref[:, pl.ds(off, tk)]
        sel = (keys > thr) | ((keys == thr) & (col0 + off < tie_cut))
        return off, jnp.where(sel, 0.0, NEG_BIG)

    def group_scores(g, cc, bias):
        s = _dot_nt(qs[g * grows:(g + 1) * grows, :], cc)
        return (s.reshape(grows // tq, tq, tk) + bias[None]).reshape(grows, tk)

    def finalize(l_col):
        z = z_ref[...].astype(F32)
        for h in range(DSA_HEADS):
            rs = slice(h * tq, (h + 1) * tq)
            ob = (acc_ref[rs, :] / l_col[rs, :]).astype(BF16)
            oh = jnp.dot(ob, wuv_ref[h], preferred_element_type=F32)
            sl = slice(h * DSA_V_DIM, (h + 1) * DSA_V_DIM)
            o_ref[:, sl] = (oh * _silu(z[:, sl])).astype(o_ref.dtype)

    qf = qs.astype(F32)
    m_ref[...] = jnp.broadcast_to(jnp.sqrt(jnp.sum(qf * qf, axis=-1, keepdims=True))
                                  * cn_ref[:, 0:1] * BOUND_SLACK, m_ref.shape)
    l_ref[...] = jnp.zeros(l_ref.shape, F32)
    acc_ref[...] = jnp.zeros(acc_ref.shape, F32)

    def bounded_chunk(kc):
        off, bias = selection_bias(kc)
        cc = c_ref[pl.ds(off, tk), :]
        for g in range(n_groups):
            rs = slice(g * grows, (g + 1) * grows)
            p = jnp.exp2(group_scores(g, cc, bias) - jnp.tile(m_ref[rs, :], (1, nlb)))
            l_ref[rs, :] = l_ref[rs, :] + _lane_fold(p)
            acc_ref[rs, :] = acc_ref[rs, :] + jnp.dot(p.astype(BF16), cc, preferred_element_type=F32)

    def bounded_group(kp, carry):
        for u in range(DSA_UNROLL):
            bounded_chunk(kp * DSA_UNROLL + u)
        return carry

    def bounded_tail(kc, carry):
        bounded_chunk(kc)
        return carry

    lax.fori_loop(0, nkc // DSA_UNROLL, bounded_group, 0)
    lax.fori_loop((nkc // DSA_UNROLL) * DSA_UNROLL, nkc, bounded_tail, 0)
    l_tot = jnp.sum(l_ref[...], axis=-1, keepdims=True)
    underflow = jnp.logical_not(jnp.min(l_tot) >= UNDERFLOW_GUARD)

    @pl.when(jnp.logical_not(underflow))
    def _():
        finalize(l_tot)

    @pl.when(underflow)
    def _():
        m_ref[...] = jnp.full(m_ref.shape, NEG_BIG, F32)
        l_ref[...] = jnp.zeros(l_ref.shape, F32)
        acc_ref[...] = jnp.zeros(acc_ref.shape, F32)

        def body(kc, carry):
            off, bias = selection_bias(kc)
            cc = c_ref[pl.ds(off, tk), :]
            for g in range(n_groups):
                rs = slice(g * grows, (g + 1) * grows)
                s = group_scores(g, cc, bias)
                m_prev = m_ref[rs, :]
                m_new = jnp.maximum(m_prev, jnp.max(s, axis=-1, keepdims=True))
                alpha = jnp.exp2(m_prev - m_new)
                p = jnp.exp2(s - jnp.tile(m_new, (1, nlb)))
                l_ref[rs, :] = alpha * l_ref[rs, :] + jnp.sum(p, axis=-1, keepdims=True)
                acc_ref[rs, :] = (jnp.tile(alpha, (1, DSA_LATENT // LANE)) * acc_ref[rs, :]
                                  + jnp.dot(p.astype(BF16), cc, preferred_element_type=F32))
                m_ref[rs, :] = m_new
            return carry

        lax.fori_loop(0, nkc, body, 0)
        finalize(l_ref[:, 0:1])


def _dsa_attention(p_main, iw, c_lat, ik_n, w_uv, b, s, tq, tk):
    m = b * s
    nq = s // tq
    topk = min(IDX_TOPK_MAX, s // 4)
    assert topk <= 2 * LANE, "the per-lane top-two start of the threshold search covers 2 * LANE keys"
    c_q = MAIN_OFF[SEG_DSA_Q] // (DSA_HEADS * DSA_LATENT)
    c_z = MAIN_OFF[SEG_DSA_Z] // 1024
    o_iq = MAIN_OFF[SEG_IDX_Q]
    iq_hm = p_main[:, o_iq:o_iq + IDX_HEADS * IDX_DIM].reshape(b * nq, tq, IDX_HEADS, IDX_DIM)
    iq_hm = jnp.transpose(iq_hm, (0, 2, 1, 3)).reshape(b * nq, IDX_HEADS * tq, IDX_DIM)
    nbits_col = max(1, int(math.ceil(math.log2(s))) + 1)
    kern = functools.partial(_dsa_kernel, tq=tq, tk=tk, topk=topk, nbits_col=nbits_col, s_len=s)
    return pl.pallas_call(
        kern,
        grid=(b, nq),
        in_specs=[
            pl.BlockSpec((tq, DSA_HEADS * DSA_LATENT), lambda bi, qi: (bi * nq + qi, c_q)),
            pl.BlockSpec((1, IDX_HEADS * tq, IDX_DIM), lambda bi, qi: (bi * nq + qi, 0, 0)),
            pl.BlockSpec((tq, IDX_HEADS), lambda bi, qi: (bi * nq + qi, 0)),
            pl.BlockSpec((tq, 1024), lambda bi, qi: (bi * nq + qi, c_z)),
            pl.BlockSpec((s, DSA_LATENT), lambda bi, qi: (bi, 0)),
            pl.BlockSpec((s, IDX_DIM), lambda bi, qi: (bi, 0)),
            pl.BlockSpec((DSA_HEADS, DSA_LATENT, DSA_V_DIM), lambda bi, qi: (0, 0, 0)),
        ],
        out_specs=pl.BlockSpec((tq, DSA_HEADS * DSA_V_DIM), lambda bi, qi: (bi * nq + qi, 0)),
        out_shape=jax.ShapeDtypeStruct((m, DSA_HEADS * DSA_V_DIM), BF16),
        scratch_shapes=[pltpu.VMEM((tq, s), I32),
                        pltpu.VMEM((DSA_HEADS * tq, LANE), F32),
                        pltpu.VMEM((DSA_HEADS * tq, LANE), F32),
                        pltpu.VMEM((DSA_HEADS * tq, DSA_LATENT), F32),
                        pltpu.VMEM((tq, LANE), I32),
                        pltpu.VMEM((1, LANE), F32),
                        pltpu.VMEM((IDX_HEADS, tq, LANE), F32)],
        compiler_params=_cparams(("parallel", "arbitrary")),
        name="dsa_attention",
    )(p_main, iq_hm, iw, p_main, c_lat, ik_n, w_uv)


def _out_kernel(x_ref, ya_ref, yb_ref, yc_ref, ga_ref, gb_ref, gc_ref, wb_ref, wo_ref, pg_ref, o_ref):
    mixed = None
    for i, (y_ref, g_ref) in enumerate(((ya_ref, ga_ref), (yb_ref, gb_ref), (yc_ref, gc_ref))):
        t = _sigmoid(g_ref[...].astype(F32)) * jnp.dot(y_ref[...], wb_ref[i], preferred_element_type=F32)
        mixed = t if mixed is None else mixed + t
    out = jnp.dot(mixed.astype(BF16), wo_ref[...], preferred_element_type=F32)
    ms = jnp.mean(out * out, axis=-1, keepdims=True)
    o_ref[...] = x_ref[...] + out * lax.rsqrt(ms + NORM_EPS) * pg_ref[...]


def _merge_out(x2, ya, yb, yc, p_main, w_branch, w_out, post_g, tm):
    m, d = x2.shape
    cg = MAIN_OFF[SEG_GATES] // d
    row = lambda i: (i, 0)
    return pl.pallas_call(
        _out_kernel,
        grid=(m // tm,),
        in_specs=[
            pl.BlockSpec((tm, d), row),
            pl.BlockSpec((tm, BRANCH_WIDTH), row),
            pl.BlockSpec((tm, BRANCH_WIDTH), row),
            pl.BlockSpec((tm, BRANCH_WIDTH), row),
            pl.BlockSpec((tm, d), lambda i: (i, cg)),
            pl.BlockSpec((tm, d), lambda i: (i, cg + 1)),
            pl.BlockSpec((tm, d), lambda i: (i, cg + 2)),
            pl.BlockSpec((N_BRANCHES, BRANCH_WIDTH, d), lambda i: (0, 0, 0), pipeline_mode=pl.Buffered(1)),
            pl.BlockSpec((d, d), lambda i: (0, 0), pipeline_mode=pl.Buffered(1)),
            pl.BlockSpec((1, d), lambda i: (0, 0)),
        ],
        out_specs=pl.BlockSpec((tm, d), row),
        out_shape=jax.ShapeDtypeStruct((m, d), F32),
        compiler_params=_cparams(("parallel",)),
        name="merge_out_proj",
    )(x2, ya, yb, yc, p_main, p_main, p_main, w_branch, w_out, post_g.reshape(1, d))


def _pick(total, want):
    t = min(total, want)
    while total % t:
        t //= 2
    return t


def kernel(x, pre_norm_g, w_in, da_lambda, da_norm_g, ml_conv_w, ml_conv_b, ml_gate_b, ml_norm_g,
           dsa_kv_norm_g, dsa_ik_norm_g, dsa_w_uv, w_branch, w_out, post_norm_g):
    b, s, d = x.shape
    m = b * s
    bounds = _seg_bounds()
    x2 = x.reshape(m, d)
    tm_proj = _pick(m, 1024)
    t_da = _pick(s, 512)
    l_ml = _pick(s, 256)
    tq_dsa = _pick(s, 128)
    tk_dsa = _pick(s, 512)
    o_if = SMALL_OFF[SEG_ML_IF]
    o_iw = SMALL_OFF[SEG_IDX_W]
    for l in range(DEPTH):
        w = w_in[l]
        col_scale = {SEG_DA_Q: DA_HEAD_DIM ** -0.5 * LOG2E, SEG_DSA_Q: DSA_LATENT ** -0.5 * LOG2E}
        w_main = jnp.concatenate([w[:, bounds[sg]:bounds[sg + 1]] * col_scale.get(sg, 1.0) for sg in MAIN_SEGS],
                                 axis=1).astype(BF16)
        w_small = jnp.concatenate([w[:, bounds[sg]:bounds[sg + 1]] for sg in SMALL_SEGS]
                                  + [jnp.zeros((d, SMALL_WIDTH - SMALL_USED), w.dtype)], axis=1).astype(BF16)
        p_main = _project(x2, pre_norm_g[l], w_main, BF16, tm_proj, _pick(MAIN_WIDTH, 2048))
        p_small = _project(x2, pre_norm_g[l], w_small, F32, tm_proj, SMALL_WIDTH)

        lam_init = 0.8 - 0.6 * math.exp(-0.3 * l)
        y_a = _diff_attention(p_main, da_lambda[l], da_norm_g[l], lam_init, b, s, t_da)

        gif = p_small[:, o_if:o_if + 2 * ML_HEADS]
        gif_t = jnp.transpose(gif.reshape(b, s, 2 * ML_HEADS), (0, 2, 1))
        y_b = _mlstm(p_main, gif, gif_t, ml_conv_w[l], ml_conv_b[l], ml_gate_b[l], ml_norm_g[l], b, s, l_ml)

        c_lat, ik_n = _dsa_norms(p_small, dsa_kv_norm_g[l], dsa_ik_norm_g[l], _pick(m, 2048))
        iw = p_small[:, o_iw:o_iw + IDX_HEADS]
        y_c = _dsa_attention(p_main, iw, c_lat, ik_n, dsa_w_uv[l].astype(BF16), b, s, tq_dsa, tk_dsa)

        x2 = _merge_out(x2, y_a, y_b, y_c, p_main, w_branch[l].astype(BF16), w_out[l].astype(BF16),
                        post_norm_g[l], _pick(m, 256))
    return x2.reshape(b, s, d)
```

```python
import functools
import math

import jax
import jax.numpy as jnp
from jax import lax
from jax.experimental import pallas as pl
from jax.experimental.pallas import tpu as pltpu

F32 = jnp.float32
BF16 = jnp.bfloat16
I32 = jnp.int32

D_MODEL = 2048
DEPTH = 2
BRANCH_WIDTH = 1024
N_BRANCHES = 3
NORM_EPS = 1e-6
DA_HEADS = 8
DA_HEAD_DIM = 64
DA_V_DIM = 128
ML_HEADS = 4
ML_QK_DIM = 128
ML_V_DIM = 256
ML_CONV = 4
ML_QK_WIDTH = 2 * ML_HEADS * ML_QK_DIM
DSA_HEADS = 8
DSA_LATENT = 256
DSA_V_DIM = 128
IDX_HEADS = 16
IDX_DIM = 64
IDX_TOPK_MAX = 256

IN_SPLITS = (
    1024, 1024, 1024, 1024,
    ML_QK_WIDTH, 1024, 2 * ML_HEADS,
    1024, 1024,
    DSA_HEADS * DSA_LATENT, DSA_LATENT,
    IDX_HEADS * IDX_DIM, IDX_DIM, IDX_HEADS,
    1024,
    N_BRANCHES * D_MODEL,
)
(SEG_DA_Q, SEG_DA_K, SEG_DA_V, SEG_DA_Z, SEG_ML_QK, SEG_ML_V, SEG_ML_IF, SEG_ML_O, SEG_ML_Z,
 SEG_DSA_Q, SEG_DSA_KV, SEG_IDX_Q, SEG_IDX_K, SEG_IDX_W, SEG_DSA_Z, SEG_GATES) = range(16)

MAIN_SEGS = (SEG_DA_Q, SEG_DA_K, SEG_DA_V, SEG_DA_Z, SEG_ML_QK, SEG_ML_V, SEG_ML_O, SEG_ML_Z,
             SEG_DSA_Q, SEG_IDX_Q, SEG_DSA_Z, SEG_GATES)
SMALL_SEGS = (SEG_DSA_KV, SEG_IDX_K, SEG_IDX_W, SEG_ML_IF)
SMALL_WIDTH = 384

LANE = 128
VMEM_LIMIT = 56 * 1024 * 1024
INT_MIN = -2147483648
NEG_BIG = -1e30
LOG2E = math.log2(math.e)
DA_UNROLL = 4
DSA_UNROLL = 4
COUNT_ROWS = 64
COUNT_UNROLL = 4
BOUND_SLACK = 1.01
UNDERFLOW_GUARD = 2.0 ** -100


def _seg_bounds():
    offs = [0]
    for n in IN_SPLITS:
        offs.append(offs[-1] + n)
    return offs


def _main_offsets():
    out, off = {}, 0
    for s in MAIN_SEGS:
        out[s] = off
        off += IN_SPLITS[s]
    return out, off


def _small_offsets():
    out, off = {}, 0
    for s in SMALL_SEGS:
        out[s] = off
        off += IN_SPLITS[s]
    return out, off


MAIN_OFF, MAIN_WIDTH = _main_offsets()
SMALL_OFF, SMALL_USED = _small_offsets()


def _cparams(sem):
    return pltpu.CompilerParams(dimension_semantics=sem, vmem_limit_bytes=VMEM_LIMIT)


def _silu(v):
    return v * (1.0 / (1.0 + jnp.exp(-v)))


def _sigmoid(v):
    return 1.0 / (1.0 + jnp.exp(-v))


def _dot_nt(a, b):
    return lax.dot_general(a, b, (((1,), (1,)), ((), ())), preferred_element_type=F32)


def _proj_kernel(x_ref, g_ref, w_ref, o_ref, h_ref):
    @pl.when(pl.program_id(1) == 0)
    def _():
        x = x_ref[...]
        ms = jnp.mean(x * x, axis=-1, keepdims=True)
        h_ref[...] = (x * lax.rsqrt(ms + NORM_EPS) * g_ref[...]).astype(BF16)

    o_ref[...] = jnp.dot(h_ref[...], w_ref[...], preferred_element_type=F32).astype(o_ref.dtype)


def _project(x2, g, w, out_dtype, tm, tn):
    m, d = x2.shape
    n = w.shape[1]
    return pl.pallas_call(
        _proj_kernel,
        grid=(m // tm, n // tn),
        in_specs=[pl.BlockSpec((tm, d), lambda i, j: (i, 0)),
                  pl.BlockSpec((1, d), lambda i, j: (0, 0)),
                  pl.BlockSpec((d, tn), lambda i, j: (0, j))],
        out_specs=pl.BlockSpec((tm, tn), lambda i, j: (i, j)),
        out_shape=jax.ShapeDtypeStruct((m, n), out_dtype),
        scratch_shapes=[pltpu.VMEM((tm, d), BF16)],
        compiler_params=_cparams(("parallel", "arbitrary")),
        name="rmsnorm_in_proj",
    )(x2, g.reshape(1, d), w)


def _lane_fold(x):
    part = x[:, 0:LANE]
    for j in range(1, x.shape[1] // LANE):
        part = part + x[:, j * LANE:(j + 1) * LANE]
    return part


def _da_kernel(lam_ref, q_ref, k_ref, v_ref, z_ref, g_ref, o_ref, m_ref, l_ref, acc_ref, kn_ref,
               *, t, s_len, lam_init):
    qi = pl.program_id(2)
    nrep = t // LANE
    half = [slice(c * DA_HEAD_DIM, (c + 1) * DA_HEAD_DIM) for c in range(2)]

    @pl.when(qi == 0)
    def _():
        def body(i, mx):
            kk = k_ref[pl.ds(pl.multiple_of(i * t, t), t), :].astype(F32)
            sq = kk * kk
            return tuple(jnp.maximum(mx[c], jnp.max(jnp.sum(sq[:, half[c]], axis=-1, keepdims=True),
                                                    axis=0, keepdims=True)) for c in range(2))
        mx = lax.fori_loop(0, s_len // t, body, (jnp.zeros((1, 1), F32), jnp.zeros((1, 1), F32)))
        for c in range(2):
            kn_ref[c] = jnp.broadcast_to(jnp.sqrt(mx[c]), (1, LANE))

    q = q_ref[...]
    qsq = q.astype(F32)
    qsq = qsq * qsq
    bound = [jnp.broadcast_to(jnp.sqrt(jnp.sum(qsq[:, half[c]], axis=-1, keepdims=True))
                              * kn_ref[c][:, 0:1] * BOUND_SLACK, (t, LANE)) for c in range(2)]

    def causal_keep():
        row = lax.broadcasted_iota(I32, (t, t), 0)
        col = lax.broadcasted_iota(I32, (t, t), 1)
        return col <= row

    def sweep(block):
        def body(kp, carry):
            for u in range(DA_UNROLL):
                block(pl.multiple_of(kp * (DA_UNROLL * t) + u * t, t), False)
            return carry
        lax.fori_loop(0, qi // DA_UNROLL, body, 0)

        def tail(ki, carry):
            block(pl.multiple_of(ki * t, t), False)
            return carry
        lax.fori_loop((qi // DA_UNROLL) * DA_UNROLL, qi, tail, 0)

        block(pl.multiple_of(qi * t, t), True)

    def bounded_block(off, masked):
        k = k_ref[pl.ds(off, t), :]
        v = v_ref[pl.ds(off, t), :]
        for c in range(2):
            s = _dot_nt(q[:, half[c]], k[:, half[c]])
            if masked:
                s = jnp.where(causal_keep(), s, -jnp.inf)
            p = jnp.exp2(s - jnp.tile(bound[c], (1, nrep)))
            l_ref[c] = l_ref[c] + _lane_fold(p)
            acc_ref[c] = acc_ref[c] + jnp.dot(p.astype(BF16), v, preferred_element_type=F32)

    def running_max_block(off, masked):
        k = k_ref[pl.ds(off, t), :]
        v = v_ref[pl.ds(off, t), :]
        for c in range(2):
            s = _dot_nt(q[:, half[c]], k[:, half[c]])
            if masked:
                s = jnp.where(causal_keep(), s, -jnp.inf)
            m_prev = m_ref[c]
            m_new = jnp.maximum(m_prev, jnp.max(s, axis=-1, keepdims=True))
            alpha = jnp.exp2(m_prev - m_new)
            p = jnp.exp2(s - jnp.tile(m_new, (1, nrep)))
            l_ref[c] = alpha * l_ref[c] + jnp.sum(p, axis=-1, keepdims=True)
            acc_ref[c] = alpha * acc_ref[c] + jnp.dot(p.astype(BF16), v, preferred_element_type=F32)
            m_ref[c] = m_new

    l_ref[...] = jnp.zeros(l_ref.shape, F32)
    acc_ref[...] = jnp.zeros(acc_ref.shape, F32)
    sweep(bounded_block)
    l_tot = [jnp.sum(l_ref[c], axis=-1, keepdims=True) for c in range(2)]
    underflow = jnp.logical_not(jnp.min(jnp.minimum(l_tot[0], l_tot[1])) >= UNDERFLOW_GUARD)

    @pl.when(jnp.logical_not(underflow))
    def _():
        for c in range(2):
            acc_ref[c] = acc_ref[c] / l_tot[c]

    @pl.when(underflow)
    def _():
        m_ref[...] = jnp.full(m_ref.shape, -jnp.inf, F32)
        l_ref[...] = jnp.zeros(l_ref.shape, F32)
        acc_ref[...] = jnp.zeros(acc_ref.shape, F32)
        sweep(running_max_block)
        for c in range(2):
            acc_ref[c] = acc_ref[c] / l_ref[c]

    lp = lam_ref[...]
    lam = (jnp.exp(jnp.sum(lp[0:1] * lp[1:2], axis=-1, keepdims=True))
           - jnp.exp(jnp.sum(lp[2:3] * lp[3:4], axis=-1, keepdims=True)) + lam_init)
    o = acc_ref[0] - lam * acc_ref[1]
    ms = jnp.mean(o * o, axis=-1, keepdims=True)
    o = o * lax.rsqrt(ms + NORM_EPS) * g_ref[...] * (1.0 - lam_init)
    o_ref[...] = (o * _silu(z_ref[...].astype(F32))).astype(o_ref.dtype)


def _diff_attention(p_main, lam_params, norm_g, lam_init, b, s, t):
    m = b * s
    nq = s // t
    cq = MAIN_OFF[SEG_DA_Q] // LANE
    ck = MAIN_OFF[SEG_DA_K] // LANE
    cv = MAIN_OFF[SEG_DA_V] // LANE
    cz = MAIN_OFF[SEG_DA_Z] // LANE
    kern = functools.partial(_da_kernel, t=t, s_len=s, lam_init=lam_init)
    return pl.pallas_call(
        kern,
        grid=(b, DA_HEADS, nq),
        in_specs=[
            pl.BlockSpec((4, DA_HEAD_DIM), lambda bi, h, qi: (0, 0)),
            pl.BlockSpec((t, LANE), lambda bi, h, qi: (bi * nq + qi, cq + h)),
            pl.BlockSpec((s, LANE), lambda bi, h, qi: (bi, ck + h)),
            pl.BlockSpec((s, LANE), lambda bi, h, qi: (bi, cv + h)),
            pl.BlockSpec((t, LANE), lambda bi, h, qi: (bi * nq + qi, cz + h)),
            pl.BlockSpec((1, DA_V_DIM), lambda bi, h, qi: (0, 0)),
        ],
        out_specs=pl.BlockSpec((t, LANE), lambda bi, h, qi: (bi * nq + qi, h)),
        out_shape=jax.ShapeDtypeStruct((m, DA_HEADS * DA_V_DIM), BF16),
        scratch_shapes=[pltpu.VMEM((2, t, LANE), F32), pltpu.VMEM((2, t, LANE), F32),
                        pltpu.VMEM((2, t, DA_V_DIM), F32), pltpu.VMEM((2, 1, LANE), F32)],
        compiler_params=_cparams(("parallel", "parallel", "arbitrary")),
        name="diff_attention",
    )(lam_params, p_main, p_main, p_main, p_main, norm_g.reshape(1, DA_V_DIM))


def _split3(v):
    hi = v.astype(BF16)
    r1 = v - hi.astype(F32)
    mid = r1.astype(BF16)
    lo = (r1 - mid.astype(F32)).astype(BF16)
    return hi, mid, lo


def _log_sigmoid(v):
    return jnp.minimum(v, 0.0) - jnp.log(1.0 + jnp.exp(-jnp.abs(v)))


def _ml_kernel(qk_ref, v_ref, o_ref, z_ref, gif_ref, gift_ref, cw_ref, cb_ref, gb_ref, gbt_ref, ng_ref,
               y_ref, xbuf, c_st, n_st, m_st, *, L):
    ci = pl.program_id(1)
    pad = 8

    @pl.when(ci == 0)
    def _():
        xbuf[0:pad, :] = jnp.zeros((pad, ML_QK_WIDTH), F32)
        c_st[...] = jnp.zeros(c_st.shape, F32)
        n_st[...] = jnp.zeros(n_st.shape, F32)
        m_st[...] = jnp.zeros(m_st.shape, F32)

    xbuf[pad:pad + L, :] = qk_ref[...].astype(F32)
    cw = cw_ref[...]
    y = xbuf[pad:pad + L, :] * cw[ML_CONV - 1:ML_CONV, :]
    for j in range(ML_CONV - 1):
        sh = ML_CONV - 1 - j
        y = y + xbuf[pad - sh:pad - sh + L, :] * cw[j:j + 1, :]
    y = _silu(y + cb_ref[...])
    xbuf[0:pad, :] = xbuf[L:L + pad, :]

    g_c = gif_ref[...] + gb_ref[...]
    g_r = gift_ref[0] + gbt_ref[...]
    lf_c = _log_sigmoid(g_c)
    lf_r = _log_sigmoid(g_r)
    ti = lax.broadcasted_iota(I32, (L, L), 0)
    si = lax.broadcasted_iota(I32, (L, L), 1)
    causal = si <= ti
    tri = jnp.where(causal, 1.0, 0.0).astype(BF16)
    triu = jnp.where(ti <= si, 1.0, 0.0).astype(BF16)
    bc_all = sum(jnp.dot(tri, piece, preferred_element_type=F32) for piece in _split3(lf_c))
    br_all = sum(jnp.dot(piece, triu, preferred_element_type=F32) for piece in _split3(lf_r))

    for h in range(ML_HEADS):
        q = y[:, h * ML_QK_DIM:(h + 1) * ML_QK_DIM].astype(BF16)
        kf = y[:, ML_HEADS * ML_QK_DIM + h * ML_QK_DIM:ML_HEADS * ML_QK_DIM + (h + 1) * ML_QK_DIM] * (ML_QK_DIM ** -0.5)
        k = kf.astype(BF16)
        v = v_ref[:, h * ML_V_DIM:(h + 1) * ML_V_DIM]
        bc = bc_all[:, ML_HEADS + h:ML_HEADS + h + 1]
        br = br_all[ML_HEADS + h:ML_HEADS + h + 1, :]
        ig_c = g_c[:, h:h + 1]
        ig_r = g_r[h:h + 1, :]
        m_prev = m_st[h][:, 0:1]
        c_prev = c_st[h]
        n_prev = n_st[h]

        dmat = jnp.where(causal, bc - br + ig_r, -jnp.inf)
        inter = bc + m_prev
        m_t = jnp.maximum(jnp.max(dmat, axis=-1, keepdims=True), inter)
        w = jnp.exp(dmat - m_t) * _dot_nt(q, k)
        decay = jnp.exp(inter - m_t)
        num = (jnp.dot(w.astype(BF16), v, preferred_element_type=F32)
               + decay * jnp.dot(q, c_prev.astype(BF16), preferred_element_type=F32))
        qn = jnp.sum(q.astype(F32) * n_prev.astype(BF16).astype(F32), axis=-1, keepdims=True)
        den = jnp.sum(w, axis=-1, keepdims=True) + decay * qn
        hh = num / jnp.maximum(jnp.abs(den), jnp.exp(-m_t))

        b_last = br[:, L - 1:L]
        g_row = b_last - br + ig_r
        m_new = jnp.maximum(b_last + m_prev, jnp.max(g_row, axis=-1, keepdims=True))
        wk_c = jnp.exp(b_last - bc + ig_c - m_new)
        cd = jnp.exp(b_last + m_prev - m_new)
        kw = (kf * wk_c)
        c_st[h] = cd * c_prev + lax.dot_general(kw.astype(BF16), v, (((0,), (0,)), ((), ())),
                                                preferred_element_type=F32)
        n_st[h] = cd * n_prev + jnp.sum(kw, axis=0, keepdims=True)
        m_st[h] = jnp.broadcast_to(m_new, (1, LANE))

        ms = jnp.mean(hh * hh, axis=-1, keepdims=True)
        hn = hh * lax.rsqrt(ms + NORM_EPS) * ng_ref[...]
        sl = slice(h * ML_V_DIM, (h + 1) * ML_V_DIM)
        out = hn * _sigmoid(o_ref[:, sl].astype(F32)) * _silu(z_ref[:, sl].astype(F32))
        y_ref[:, sl] = out.astype(y_ref.dtype)


def _mlstm(p_main, gif, gif_t, conv_w, conv_b, gate_b, norm_g, b, s, L):
    m = b * s
    nc = s // L
    w1k = 1024
    c_qk = MAIN_OFF[SEG_ML_QK] // w1k
    c_v = MAIN_OFF[SEG_ML_V] // w1k
    c_o = MAIN_OFF[SEG_ML_O] // w1k
    c_z = MAIN_OFF[SEG_ML_Z] // w1k
    gb = gate_b.reshape(1, 2 * ML_HEADS)
    kern = functools.partial(_ml_kernel, L=L)
    full = lambda shp: pl.BlockSpec(shp, lambda bi, ci: (0,) * len(shp))
    return pl.pallas_call(
        kern,
        grid=(b, nc),
        in_specs=[
            pl.BlockSpec((L, w1k), lambda bi, ci: (bi * nc + ci, c_qk)),
            pl.BlockSpec((L, w1k), lambda bi, ci: (bi * nc + ci, c_v)),
            pl.BlockSpec((L, w1k), lambda bi, ci: (bi * nc + ci, c_o)),
            pl.BlockSpec((L, w1k), lambda bi, ci: (bi * nc + ci, c_z)),
            pl.BlockSpec((L, 2 * ML_HEADS), lambda bi, ci: (bi * nc + ci, 0)),
            pl.BlockSpec((1, 2 * ML_HEADS, L), lambda bi, ci: (bi, 0, ci)),
            full((ML_CONV, ML_QK_WIDTH)),
            full((1, ML_QK_WIDTH)),
            full((1, 2 * ML_HEADS)),
            full((2 * ML_HEADS, 1)),
            full((1, ML_V_DIM)),
        ],
        out_specs=pl.BlockSpec((L, w1k), lambda bi, ci: (bi * nc + ci, 0)),
        out_shape=jax.ShapeDtypeStruct((m, ML_HEADS * ML_V_DIM), BF16),
        scratch_shapes=[pltpu.VMEM((L + 8, ML_QK_WIDTH), F32),
                        pltpu.VMEM((ML_HEADS, ML_QK_DIM, ML_V_DIM), F32),
                        pltpu.VMEM((ML_HEADS, 1, ML_QK_DIM), F32),
                        pltpu.VMEM((ML_HEADS, 1, LANE), F32)],
        compiler_params=_cparams(("parallel", "arbitrary")),
        name="mlstm",
    )(p_main, p_main, p_main, p_main, gif, gif_t, conv_w, conv_b.reshape(1, ML_QK_WIDTH),
      gb, gb.reshape(2 * ML_HEADS, 1), norm_g.reshape(1, ML_V_DIM))


def _dsa_norm_kernel(p_ref, gkv_ref, gik_ref, c_ref, ik_ref):
    o_kv = SMALL_OFF[SEG_DSA_KV]
    o_ik = SMALL_OFF[SEG_IDX_K]
    ckv = p_ref[:, o_kv:o_kv + DSA_LATENT]
    ms = jnp.mean(ckv * ckv, axis=-1, keepdims=True)
    c_ref[...] = (ckv * lax.rsqrt(ms + NORM_EPS) * gkv_ref[...]).astype(c_ref.dtype)
    ik = p_ref[:, o_ik:o_ik + IDX_DIM]
    ms = jnp.mean(ik * ik, axis=-1, keepdims=True)
    ik_ref[...] = (ik * lax.rsqrt(ms + NORM_EPS) * gik_ref[...]).astype(ik_ref.dtype)


def _dsa_norms(p_small, g_kv, g_ik, tm):
    m = p_small.shape[0]
    return pl.pallas_call(
        _dsa_norm_kernel,
        grid=(m // tm,),
        in_specs=[pl.BlockSpec((tm, SMALL_WIDTH), lambda i: (i, 0)),
                  pl.BlockSpec((1, DSA_LATENT), lambda i: (0, 0)),
                  pl.BlockSpec((1, IDX_DIM), lambda i: (0, 0))],
        out_specs=[pl.BlockSpec((tm, DSA_LATENT), lambda i: (i, 0)),
                   pl.BlockSpec((tm, IDX_DIM), lambda i: (i, 0))],
        out_shape=[jax.ShapeDtypeStruct((m, DSA_LATENT), BF16),
                   jax.ShapeDtypeStruct((m, IDX_DIM), BF16)],
        compiler_params=_cparams(("parallel",)),
        name="dsa_norms",
    )(p_small, g_kv.reshape(1, DSA_LATENT), g_ik.reshape(1, IDX_DIM))


def _dsa_kernel(q_ref, iq_ref, iw_ref, z_ref, c_ref, ik_ref, wuv_ref, o_ref,
                key_ref, m_ref, l_ref, acc_ref, cut_ref, cn_ref, wb_ref, *, tq, tk, topk, nbits_col, s_len):
    qi = pl.program_id(1)
    nkc = (qi * tq + tq + tk - 1) // tk
    row = qi * tq + lax.broadcasted_iota(I32, (tq, tk), 0)
    col0 = lax.broadcasted_iota(I32, (tq, tk), 1)
    nlb = tk // LANE

    iw = iw_ref[...] * ((IDX_HEADS ** -0.5) * (IDX_DIM ** -0.5))
    for h in range(IDX_HEADS):
        wb_ref[h] = jnp.broadcast_to(iw[:, h:h + 1], (tq, LANE))
    iq = iq_ref[0]
    sub = 2 * LANE
    nsub = tk // sub

    def score_body(kc, carry):
        top1, top2 = carry
        off = pl.multiple_of(kc * tk, tk)
        parts = []
        for j in range(nsub):
            ikc = ik_ref[pl.ds(pl.multiple_of(off + j * sub, sub), sub), :]
            sc = _dot_nt(iq, ikc)
            part = None
            for h in range(IDX_HEADS):
                term = jnp.maximum(sc[h * tq:(h + 1) * tq, :], 0.0) * jnp.tile(wb_ref[h], (1, sub // LANE))
                part = term if part is None else part + term
            parts.append(part)
        acc = jnp.concatenate(parts, axis=1)
        bits = lax.bitcast_convert_type(acc, I32)
        skey = jnp.where(bits < 0, (bits ^ 0x7FFFFFFF) + 1, bits)
        skey = jnp.where(col0 + off <= row, skey, INT_MIN)
        key_ref[:, pl.ds(off, tk)] = skey
        for j in range(nlb):
            blk = skey[:, j * LANE:(j + 1) * LANE]
            top2 = jnp.maximum(top2, jnp.minimum(top1, blk))
            top1 = jnp.maximum(top1, blk)
        return top1, top2

    def score_group(kp, carry):
        for u in range(DSA_UNROLL):
            carry = score_body(kp * DSA_UNROLL + u, carry)
        return carry

    lowest = jnp.full((tq, LANE), INT_MIN, I32)
    tops = lax.fori_loop(0, nkc // DSA_UNROLL, score_group, (lowest, lowest))
    top1, top2 = lax.fori_loop((nkc // DSA_UNROLL) * DSA_UNROLL, nkc, score_body, tops)

    def count_ge(t_col):
        parts = []
        for r0 in range(0, tq, COUNT_ROWS):
            t_b = jnp.broadcast_to(t_col[r0:r0 + COUNT_ROWS], (COUNT_ROWS, tk))

            def chunk(kc, cnt, r0=r0, t_b=t_b):
                off = pl.multiple_of(kc * tk, tk)
                hit = key_ref[r0:r0 + COUNT_ROWS, pl.ds(off, tk)] >= t_b
                return cnt + _lane_fold(jnp.where(hit, 1, 0))

            def group(kp, cnt, chunk=chunk):
                for u in range(COUNT_UNROLL):
                    cnt = chunk(kp * COUNT_UNROLL + u, cnt)
                return cnt

            cnt = lax.fori_loop(0, nkc // COUNT_UNROLL, group, jnp.zeros((COUNT_ROWS, LANE), I32))
            parts.append(lax.fori_loop((nkc // COUNT_UNROLL) * COUNT_UNROLL, nkc, chunk, cnt))
        return jnp.sum(jnp.concatenate(parts, axis=0), axis=-1, keepdims=True)

    def count(pred):
        def chunk(kc, cnt):
            off = pl.multiple_of(kc * tk, tk)
            hit = pred(key_ref[:, pl.ds(off, tk)], col0 + off)
            return cnt + _lane_fold(jnp.where(hit, 1, 0))
        cnt = lax.fori_loop(0, nkc, chunk, jnp.zeros((tq, LANE), I32))
        return jnp.sum(cnt, axis=-1, keepdims=True)

    n_valid = row[:, 0:1] + 1
    lo0 = jnp.maximum(jnp.min(top2, axis=-1, keepdims=True), INT_MIN + 1)
    hi0 = jnp.maximum(jnp.max(top1, axis=-1, keepdims=True), lo0)
    searching0 = n_valid > topk

    def any_row(flag):
        return jnp.max(jnp.where(flag, 1, 0))

    def bis_cond(st):
        return st[0] > 0

    def bis_body(st):
        _, lo, hi, cnt_lo = st
        for _ in range(2):
            active = searching0 & (lo < hi) & (cnt_lo != topk)
            mid = (lo >> 1) + (hi >> 1) + ((lo | hi) & 1)
            cnt = count_ge(mid)
            up = active & (cnt >= topk)
            dn = active & (cnt < topk)
            lo = jnp.where(up, mid, lo)
            cnt_lo = jnp.where(up, cnt, cnt_lo)
            hi = jnp.where(dn, mid - 1, hi)
        return any_row(searching0 & (lo < hi) & (cnt_lo != topk)), lo, hi, cnt_lo

    st0 = (any_row(searching0 & (lo0 < hi0)), lo0, hi0, jnp.full((tq, 1), topk + 1, I32))
    _, thr, _, cnt_thr = lax.while_loop(bis_cond, bis_body, st0)
    thr = jnp.where(searching0, thr, INT_MIN + 1)
    has_ties = any_row(searching0 & (cnt_thr != topk)) > 0

    q = q_ref[...]
    qs = jnp.concatenate([q[:, h * DSA_LATENT:(h + 1) * DSA_LATENT] for h in range(DSA_HEADS)], axis=0)
    n_groups = 4
    grows = DSA_HEADS * tq // n_groups

    cut_ref[...] = jnp.full(cut_ref.shape, s_len, I32)

    @pl.when(has_ties)
    def _():
        need = topk - count(lambda k, c: k > thr)

        def cut_body(it, cut):
            cand = cut | jnp.left_shift(jnp.int32(1), nbits_col - 1 - it)
            cnt = count(lambda k, c: (k == thr) & (c < cand))
            return jnp.where(cnt < need, cand, cut)

        cut = lax.fori_loop(0, nbits_col, cut_body, jnp.zeros((tq, 1), I32))
        cut_ref[...] = jnp.broadcast_to(cut + 1, cut_ref.shape)

    tie_cut = cut_ref[:, 0:1]

    @pl.when(qi == 0)
    def _():
        def body(i, mx):
            cf = c_ref[pl.ds(pl.multiple_of(i * tk, tk), tk), :].astype(F32)
            return jnp.maximum(mx, jnp.max(jnp.sum(cf * cf, axis=-1, keepdims=True), axis=0, keepdims=True))
        mx = lax.fori_loop(0, s_len // tk, body, jnp.zeros((1, 1), F32))
        cn_ref[...] = jnp.broadcast_to(jnp.sqrt(mx), cn_ref.shape)

    def selection_bias(kc):
        off = pl.multiple_of(kc * tk, tk)
        keys = key_ref[:, pl.ds(off, tk)]
        sel = (keys > thr) | ((keys == thr) & (col0 + off < tie_cut))
        return off, jnp.where(sel, 0.0, NEG_BIG)

    def group_scores(g, cc, bias):
        s = _dot_nt(qs[g * grows:(g + 1) * grows, :], cc)
        return (s.reshape(grows // tq, tq, tk) + bias[None]).reshape(grows, tk)

    def finalize(l_col):
        z = z_ref[...].astype(F32)
        for h in range(DSA_HEADS):
            rs = slice(h * tq, (h + 1) * tq)
            ob = (acc_ref[rs, :] / l_col[rs, :]).astype(BF16)
            oh = jnp.dot(ob, wuv_ref[h], preferred_element_type=F32)
            sl = slice(h * DSA_V_DIM, (h + 1) * DSA_V_DIM)
            o_ref[:, sl] = (oh * _silu(z[:, sl])).astype(o_ref.dtype)

    qf = qs.astype(F32)
    m_ref[...] = jnp.broadcast_to(jnp.sqrt(jnp.sum(qf * qf, axis=-1, keepdims=True))
                                  * cn_ref[:, 0:1] * BOUND_SLACK, m_ref.shape)
    l_ref[...] = jnp.zeros(l_ref.shape, F32)
    acc_ref[...] = jnp.zeros(acc_ref.shape, F32)

    def bounded_chunk(kc):
        off, bias = selection_bias(kc)
        cc = c_ref[pl.ds(off, tk), :]
        for g in range(n_groups):
            rs = slice(g * grows, (g + 1) * grows)
            p = jnp.exp2(group_scores(g, cc, bias) - jnp.tile(m_ref[rs, :], (1, nlb)))
            l_ref[rs, :] = l_ref[rs, :] + _lane_fold(p)
            acc_ref[rs, :] = acc_ref[rs, :] + jnp.dot(p.astype(BF16), cc, preferred_element_type=F32)

    def bounded_group(kp, carry):
        for u in range(DSA_UNROLL):
            bounded_chunk(kp * DSA_UNROLL + u)
        return carry

    def bounded_tail(kc, carry):
        bounded_chunk(kc)
        return carry

    lax.fori_loop(0, nkc // DSA_UNROLL, bounded_group, 0)
    lax.fori_loop((nkc // DSA_UNROLL) * DSA_UNROLL, nkc, bounded_tail, 0)
    l_tot = jnp.sum(l_ref[...], axis=-1, keepdims=True)
    underflow = jnp.logical_not(jnp.min(l_tot) >= UNDERFLOW_GUARD)

    @pl.when(jnp.logical_not(underflow))
    def _():
        finalize(l_tot)

    @pl.when(underflow)
    def _():
        m_ref[...] = jnp.full(m_ref.shape, NEG_BIG, F32)
        l_ref[...] = jnp.zeros(l_ref.shape, F32)
        acc_ref[...] = jnp.zeros(acc_ref.shape, F32)

        def body(kc, carry):
            off, bias = selection_bias(kc)
            cc = c_ref[pl.ds(off, tk), :]
            for g in range(n_groups):
                rs = slice(g * grows, (g + 1) * grows)
                s = group_scores(g, cc, bias)
                m_prev = m_ref[rs, :]
                m_new = jnp.maximum(m_prev, jnp.max(s, axis=-1, keepdims=True))
                alpha = jnp.exp2(m_prev - m_new)
                p = jnp.exp2(s - jnp.tile(m_new, (1, nlb)))
                l_ref[rs, :] = alpha * l_ref[rs, :] + jnp.sum(p, axis=-1, keepdims=True)
                acc_ref[rs, :] = (jnp.tile(alpha, (1, DSA_LATENT // LANE)) * acc_ref[rs, :]
                                  + jnp.dot(p.astype(BF16), cc, preferred_element_type=F32))
                m_ref[rs, :] = m_new
            return carry

        lax.fori_loop(0, nkc, body, 0)
        finalize(l_ref[:, 0:1])


def _dsa_attention(p_main, iw, c_lat, ik_n, w_uv, b, s, tq, tk):
    m = b * s
    nq = s // tq
    topk = min(IDX_TOPK_MAX, s // 4)
    assert topk <= 2 * LANE, "the per-lane top-two start of the threshold search covers 2 * LANE keys"
    c_q = MAIN_OFF[SEG_DSA_Q] // (DSA_HEADS * DSA_LATENT)
    c_z = MAIN_OFF[SEG_DSA_Z] // 1024
    o_iq = MAIN_OFF[SEG_IDX_Q]
    iq_hm = p_main[:, o_iq:o_iq + IDX_HEADS * IDX_DIM].reshape(b * nq, tq, IDX_HEADS, IDX_DIM)
    iq_hm = jnp.transpose(iq_hm, (0, 2, 1, 3)).reshape(b * nq, IDX_HEADS * tq, IDX_DIM)
    nbits_col = max(1, int(math.ceil(math.log2(s))) + 1)
    kern = functools.partial(_dsa_kernel, tq=tq, tk=tk, topk=topk, nbits_col=nbits_col, s_len=s)
    return pl.pallas_call(
        kern,
        grid=(b, nq),
        in_specs=[
            pl.BlockSpec((tq, DSA_HEADS * DSA_LATENT), lambda bi, qi: (bi * nq + qi, c_q)),
            pl.BlockSpec((1, IDX_HEADS * tq, IDX_DIM), lambda bi, qi: (bi * nq + qi, 0, 0)),
            pl.BlockSpec((tq, IDX_HEADS), lambda bi, qi: (bi * nq + qi, 0)),
            pl.BlockSpec((tq, 1024), lambda bi, qi: (bi * nq + qi, c_z)),
            pl.BlockSpec((s, DSA_LATENT), lambda bi, qi: (bi, 0)),
            pl.BlockSpec((s, IDX_DIM), lambda bi, qi: (bi, 0)),
            pl.BlockSpec((DSA_HEADS, DSA_LATENT, DSA_V_DIM), lambda bi, qi: (0, 0, 0)),
        ],
        out_specs=pl.BlockSpec((tq, DSA_HEADS * DSA_V_DIM), lambda bi, qi: (bi * nq + qi, 0)),
        out_shape=jax.ShapeDtypeStruct((m, DSA_HEADS * DSA_V_DIM), BF16),
        scratch_shapes=[pltpu.VMEM((tq, s), I32),
                        pltpu.VMEM((DSA_HEADS * tq, LANE), F32),
                        pltpu.VMEM((DSA_HEADS * tq, LANE), F32),
                        pltpu.VMEM((DSA_HEADS * tq, DSA_LATENT), F32),
                        pltpu.VMEM((tq, LANE), I32),
                        pltpu.VMEM((1, LANE), F32),
                        pltpu.VMEM((IDX_HEADS, tq, LANE), F32)],
        compiler_params=_cparams(("parallel", "arbitrary")),
        name="dsa_attention",
    )(p_main, iq_hm, iw, p_main, c_lat, ik_n, w_uv)


def _out_kernel(x_ref, ya_ref, yb_ref, yc_ref, ga_ref, gb_ref, gc_ref, wb_ref, wo_ref, pg_ref, o_ref):
    mixed = None
    for i, (y_ref, g_ref) in enumerate(((ya_ref, ga_ref), (yb_ref, gb_ref), (yc_ref, gc_ref))):
        t = _sigmoid(g_ref[...].astype(F32)) * jnp.dot(y_ref[...], wb_ref[i], preferred_element_type=F32)
        mixed = t if mixed is None else mixed + t
    out = jnp.dot(mixed.astype(BF16), wo_ref[...], preferred_element_type=F32)
    ms = jnp.mean(out * out, axis=-1, keepdims=True)
    o_ref[...] = x_ref[...] + out * lax.rsqrt(ms + NORM_EPS) * pg_ref[...]


def _merge_out(x2, ya, yb, yc, p_main, w_branch, w_out, post_g, tm):
    m, d = x2.shape
    cg = MAIN_OFF[SEG_GATES] // d
    row = lambda i: (i, 0)
    return pl.pallas_call(
        _out_kernel,
        grid=(m // tm,),
        in_specs=[
            pl.BlockSpec((tm, d), row),
            pl.BlockSpec((tm, BRANCH_WIDTH), row),
            pl.BlockSpec((tm, BRANCH_WIDTH), row),
            pl.BlockSpec((tm, BRANCH_WIDTH), row),
            pl.BlockSpec((tm, d), lambda i: (i, cg)),
            pl.BlockSpec((tm, d), lambda i: (i, cg + 1)),
            pl.BlockSpec((tm, d), lambda i: (i, cg + 2)),
            pl.BlockSpec((N_BRANCHES, BRANCH_WIDTH, d), lambda i: (0, 0, 0), pipeline_mode=pl.Buffered(1)),
            pl.BlockSpec((d, d), lambda i: (0, 0), pipeline_mode=pl.Buffered(1)),
            pl.BlockSpec((1, d), lambda i: (0, 0)),
        ],
        out_specs=pl.BlockSpec((tm, d), row),
        out_shape=jax.ShapeDtypeStruct((m, d), F32),
        compiler_params=_cparams(("parallel",)),
        name="merge_out_proj",
    )(x2, ya, yb, yc, p_main, p_main, p_main, w_branch, w_out, post_g.reshape(1, d))


def _pick(total, want):
    t = min(total, want)
    while total % t:
        t //= 2
    return t


def kernel(x, pre_norm_g, w_in, da_lambda, da_norm_g, ml_conv_w, ml_conv_b, ml_gate_b, ml_norm_g,
           dsa_kv_norm_g, dsa_ik_norm_g, dsa_w_uv, w_branch, w_out, post_norm_g):
    b, s, d = x.shape
    m = b * s
    bounds = _seg_bounds()
    x2 = x.reshape(m, d)
    tm_proj = _pick(m, 1024)
    t_da = _pick(s, 512)
    l_ml = _pick(s, 256)
    tq_dsa = _pick(s, 128)
    tk_dsa = _pick(s, 512)
    o_if = SMALL_OFF[SEG_ML_IF]
    o_iw = SMALL_OFF[SEG_IDX_W]
    for l in range(DEPTH):
        w = w_in[l]
        col_scale = {SEG_DA_Q: DA_HEAD_DIM ** -0.5 * LOG2E, SEG_DSA_Q: DSA_LATENT ** -0.5 * LOG2E}
        w_main = jnp.concatenate([w[:, bounds[sg]:bounds[sg + 1]] * col_scale.get(sg, 1.0) for sg in MAIN_SEGS],
                                 axis=1).astype(BF16)
        w_small = jnp.concatenate([w[:, bounds[sg]:bounds[sg + 1]] for sg in SMALL_SEGS]
                                  + [jnp.zeros((d, SMALL_WIDTH - SMALL_USED), w.dtype)], axis=1).astype(BF16)
        p_main = _project(x2, pre_norm_g[l], w_main, BF16, tm_proj, _pick(MAIN_WIDTH, 2048))
        p_small = _project(x2, pre_norm_g[l], w_small, F32, tm_proj, SMALL_WIDTH)

        lam_init = 0.8 - 0.6 * math.exp(-0.3 * l)
        y_a = _diff_attention(p_main, da_lambda[l], da_norm_g[l], lam_init, b, s, t_da)

        gif = p_small[:, o_if:o_if + 2 * ML_HEADS]
        gif_t = jnp.transpose(gif.reshape(b, s, 2 * ML_HEADS), (0, 2, 1))
        y_b = _mlstm(p_main, gif, gif_t, ml_conv_w[l], ml_conv_b[l], ml_gate_b[l], ml_norm_g[l], b, s, l_ml)

        c_lat, ik_n = _dsa_norms(p_small, dsa_kv_norm_g[l], dsa_ik_norm_g[l], _pick(m, 2048))
        iw = p_small[:, o_iw:o_iw + IDX_HEADS]
        y_c = _dsa_attention(p_main, iw, c_lat, ik_n, dsa_w_uv[l].astype(BF16), b, s, tq_dsa, tk_dsa)

        x2 = _merge_out(x2, y_a, y_b, y_c, p_main, w_branch[l].astype(BF16), w_out[l].astype(BF16),
                        post_norm_g[l], _pick(m, 256))
    return x2.reshape(b, s, d)
```

```python
import functools
import math

import jax
import jax.numpy as jnp
from jax import lax
from jax.experimental import pallas as pl
from jax.experimental.pallas import tpu as pltpu

F32 = jnp.float32
BF16 = jnp.bfloat16
I32 = jnp.int32

D_MODEL = 2048
DEPTH = 2
BRANCH_WIDTH = 1024
N_BRANCHES = 3
NORM_EPS = 1e-6
DA_HEADS = 8
DA_HEAD_DIM = 64
DA_V_DIM = 128
ML_HEADS = 4
ML_QK_DIM = 128
ML_V_DIM = 256
ML_CONV = 4
ML_QK_WIDTH = 2 * ML_HEADS * ML_QK_DIM
DSA_HEADS = 8
DSA_LATENT = 256
DSA_V_DIM = 128
IDX_HEADS = 16
IDX_DIM = 64
IDX_TOPK_MAX = 256

IN_SPLITS = (
    1024, 1024, 1024, 1024,
    ML_QK_WIDTH, 1024, 2 * ML_HEADS,
    1024, 1024,
    DSA_HEADS * DSA_LATENT, DSA_LATENT,
    IDX_HEADS * IDX_DIM, IDX_DIM, IDX_HEADS,
    1024,
    N_BRANCHES * D_MODEL,
)
(SEG_DA_Q, SEG_DA_K, SEG_DA_V, SEG_DA_Z, SEG_ML_QK, SEG_ML_V, SEG_ML_IF, SEG_ML_O, SEG_ML_Z,
 SEG_DSA_Q, SEG_DSA_KV, SEG_IDX_Q, SEG_IDX_K, SEG_IDX_W, SEG_DSA_Z, SEG_GATES) = range(16)

MAIN_SEGS = (SEG_DA_Q, SEG_DA_K, SEG_DA_V, SEG_DA_Z, SEG_ML_QK, SEG_ML_V, SEG_ML_O, SEG_ML_Z,
             SEG_DSA_Q, SEG_IDX_Q, SEG_DSA_Z, SEG_GATES)
SMALL_SEGS = (SEG_DSA_KV, SEG_IDX_K, SEG_IDX_W, SEG_ML_IF)
SMALL_WIDTH = 384

LANE = 128
VMEM_LIMIT = 56 * 1024 * 1024
INT_MIN = -2147483648
NEG_BIG = -1e30
LOG2E = math.log2(math.e)
DA_UNROLL = 4
DSA_UNROLL = 4
COUNT_ROWS = 64
COUNT_UNROLL = 4
BOUND_SLACK = 1.01
UNDERFLOW_GUARD = 2.0 ** -100


def _seg_bounds():
    offs = [0]
    for n in IN_SPLITS:
        offs.append(offs[-1] + n)
    return offs


def _main_offsets():
    out, off = {}, 0
    for s in MAIN_SEGS:
        out[s] = off
        off += IN_SPLITS[s]
    return out, off


def _small_offsets():
    out, off = {}, 0
    for s in SMALL_SEGS:
        out[s] = off
        off += IN_SPLITS[s]
    return out, off


MAIN_OFF, MAIN_WIDTH = _main_offsets()
SMALL_OFF, SMALL_USED = _small_offsets()


def _cparams(sem):
    return pltpu.CompilerParams(dimension_semantics=sem, vmem_limit_bytes=VMEM_LIMIT)


def _silu(v):
    return v * (1.0 / (1.0 + jnp.exp(-v)))


def _sigmoid(v):
    return 1.0 / (1.0 + jnp.exp(-v))


def _dot_nt(a, b):
    return lax.dot_general(a, b, (((1,), (1,)), ((), ())), preferred_element_type=F32)


def _proj_kernel(x_ref, g_ref, w_ref, o_ref, h_ref):
    @pl.when(pl.program_id(1) == 0)
    def _():
        x = x_ref[...]
        ms = jnp.mean(x * x, axis=-1, keepdims=True)
        h_ref[...] = (x * lax.rsqrt(ms + NORM_EPS) * g_ref[...]).astype(BF16)

    o_ref[...] = jnp.dot(h_ref[...], w_ref[...], preferred_element_type=F32).astype(o_ref.dtype)


def _project(x2, g, w, out_dtype, tm, tn):
    m, d = x2.shape
    n = w.shape[1]
    return pl.pallas_call(
        _proj_kernel,
        grid=(m // tm, n // tn),
        in_specs=[pl.BlockSpec((tm, d), lambda i, j: (i, 0)),
                  pl.BlockSpec((1, d), lambda i, j: (0, 0)),
                  pl.BlockSpec((d, tn), lambda i, j: (0, j))],
        out_specs=pl.BlockSpec((tm, tn), lambda i, j: (i, j)),
        out_shape=jax.ShapeDtypeStruct((m, n), out_dtype),
        scratch_shapes=[pltpu.VMEM((tm, d), BF16)],
        compiler_params=_cparams(("parallel", "arbitrary")),
        name="rmsnorm_in_proj",
    )(x2, g.reshape(1, d), w)


def _lane_fold(x):
    part = x[:, 0:LANE]
    for j in range(1, x.shape[1] // LANE):
        part = part + x[:, j * LANE:(j + 1) * LANE]
    return part


def _da_kernel(lam_ref, q_ref, k_ref, v_ref, z_ref, g_ref, o_ref, m_ref, l_ref, acc_ref, kn_ref,
               *, t, s_len, lam_init):
    qi = pl.program_id(2)
    nrep = t // LANE
    half = [slice(c * DA_HEAD_DIM, (c + 1) * DA_HEAD_DIM) for c in range(2)]

    @pl.when(qi == 0)
    def _():
        def body(i, mx):
            kk = k_ref[pl.ds(pl.multiple_of(i * t, t), t), :].astype(F32)
            sq = kk * kk
            return tuple(jnp.maximum(mx[c], jnp.max(jnp.sum(sq[:, half[c]], axis=-1, keepdims=True),
                                                    axis=0, keepdims=True)) for c in range(2))
        mx = lax.fori_loop(0, s_len // t, body, (jnp.zeros((1, 1), F32), jnp.zeros((1, 1), F32)))
        for c in range(2):
            kn_ref[c] = jnp.broadcast_to(jnp.sqrt(mx[c]), (1, LANE))

    q = q_ref[...]
    qsq = q.astype(F32)
    qsq = qsq * qsq
    bound = [jnp.broadcast_to(jnp.sqrt(jnp.sum(qsq[:, half[c]], axis=-1, keepdims=True))
                              * kn_ref[c][:, 0:1] * BOUND_SLACK, (t, LANE)) for c in range(2)]

    def causal_keep():
        row = lax.broadcasted_iota(I32, (t, t), 0)
        col = lax.broadcasted_iota(I32, (t, t), 1)
        return col <= row

    def sweep(block):
        def body(kp, carry):
            for u in range(DA_UNROLL):
                block(pl.multiple_of(kp * (DA_UNROLL * t) + u * t, t), False)
            return carry
        lax.fori_loop(0, qi // DA_UNROLL, body, 0)

        def tail(ki, carry):
            block(pl.multiple_of(ki * t, t), False)
            return carry
        lax.fori_loop((qi // DA_UNROLL) * DA_UNROLL, qi, tail, 0)

        block(pl.multiple_of(qi * t, t), True)

    def bounded_block(off, masked):
        k = k_ref[pl.ds(off, t), :]
        v = v_ref[pl.ds(off, t), :]
        for c in range(2):
            s = _dot_nt(q[:, half[c]], k[:, half[c]])
            if masked:
                s = jnp.where(causal_keep(), s, -jnp.inf)
            p = jnp.exp2(s - jnp.tile(bound[c], (1, nrep)))
            l_ref[c] = l_ref[c] + _lane_fold(p)
            acc_ref[c] = acc_ref[c] + jnp.dot(p.astype(BF16), v, preferred_element_type=F32)

    def running_max_block(off, masked):
        k = k_ref[pl.ds(off, t), :]
        v = v_ref[pl.ds(off, t), :]
        for c in range(2):
            s = _dot_nt(q[:, half[c]], k[:, half[c]])
            if masked:
                s = jnp.where(causal_keep(), s, -jnp.inf)
            m_prev = m_ref[c]
            m_new = jnp.maximum(m_prev, jnp.max(s, axis=-1, keepdims=True))
            alpha = jnp.exp2(m_prev - m_new)
            p = jnp.exp2(s - jnp.tile(m_new, (1, nrep)))
            l_ref[c] = alpha * l_ref[c] + jnp.sum(p, axis=-1, keepdims=True)
            acc_ref[c] = alpha * acc_ref[c] + jnp.dot(p.astype(BF16), v, preferred_element_type=F32)
            m_ref[c] = m_new

    l_ref[...] = jnp.zeros(l_ref.shape, F32)
    acc_ref[...] = jnp.zeros(acc_ref.shape, F32)
    sweep(bounded_block)
    l_tot = [jnp.sum(l_ref[c], axis=-1, keepdims=True) for c in range(2)]
    underflow = jnp.logical_not(jnp.min(jnp.minimum(l_tot[0], l_tot[1])) >= UNDERFLOW_GUARD)

    @pl.when(jnp.logical_not(underflow))
    def _():
        for c in range(2):
            acc_ref[c] = acc_ref[c] / l_tot[c]

    @pl.when(underflow)
    def _():
        m_ref[...] = jnp.full(m_ref.shape, -jnp.inf, F32)
        l_ref[...] = jnp.zeros(l_ref.shape, F32)
        acc_ref[...] = jnp.zeros(acc_ref.shape, F32)
        sweep(running_max_block)
        for c in range(2):
            acc_ref[c] = acc_ref[c] / l_ref[c]

    lp = lam_ref[...]
    lam = (jnp.exp(jnp.sum(lp[0:1] * lp[1:2], axis=-1, keepdims=True))
           - jnp.exp(jnp.sum(lp[2:3] * lp[3:4], axis=-1, keepdims=True)) + lam_init)
    o = acc_ref[0] - lam * acc_ref[1]
    ms = jnp.mean(o * o, axis=-1, keepdims=True)
    o = o * lax.rsqrt(ms + NORM_EPS) * g_ref[...] * (1.0 - lam_init)
    o_ref[...] = (o * _silu(z_ref[...].astype(F32))).astype(o_ref.dtype)


def _diff_attention(p_main, lam_params, norm_g, lam_init, b, s, t):
    m = b * s
    nq = s // t
    cq = MAIN_OFF[SEG_DA_Q] // LANE
    ck = MAIN_OFF[SEG_DA_K] // LANE
    cv = MAIN_OFF[SEG_DA_V] // LANE
    cz = MAIN_OFF[SEG_DA_Z] // LANE
    kern = functools.partial(_da_kernel, t=t, s_len=s, lam_init=lam_init)
    return pl.pallas_call(
        kern,
        grid=(b, DA_HEADS, nq),
        in_specs=[
            pl.BlockSpec((4, DA_HEAD_DIM), lambda bi, h, qi: (0, 0)),
            pl.BlockSpec((t, LANE), lambda bi, h, qi: (bi * nq + qi, cq + h)),
            pl.BlockSpec((s, LANE), lambda bi, h, qi: (bi, ck + h)),
            pl.BlockSpec((s, LANE), lambda bi, h, qi: (bi, cv + h)),
            pl.BlockSpec((t, LANE), lambda bi, h, qi: (bi * nq + qi, cz + h)),
            pl.BlockSpec((1, DA_V_DIM), lambda bi, h, qi: (0, 0)),
        ],
        out_specs=pl.BlockSpec((t, LANE), lambda bi, h, qi: (bi * nq + qi, h)),
        out_shape=jax.ShapeDtypeStruct((m, DA_HEADS * DA_V_DIM), BF16),
        scratch_shapes=[pltpu.VMEM((2, t, LANE), F32), pltpu.VMEM((2, t, LANE), F32),
                        pltpu.VMEM((2, t, DA_V_DIM), F32), pltpu.VMEM((2, 1, LANE), F32)],
        compiler_params=_cparams(("parallel", "parallel", "arbitrary")),
        name="diff_attention",
    )(lam_params, p_main, p_main, p_main, p_main, norm_g.reshape(1, DA_V_DIM))


def _split3(v):
    hi = v.astype(BF16)
    r1 = v - hi.astype(F32)
    mid = r1.astype(BF16)
    lo = (r1 - mid.astype(F32)).astype(BF16)
    return hi, mid, lo


def _log_sigmoid(v):
    return jnp.minimum(v, 0.0) - jnp.log(1.0 + jnp.exp(-jnp.abs(v)))


def _ml_kernel(qk_ref, v_ref, o_ref, z_ref, gif_ref, gift_ref, cw_ref, cb_ref, gb_ref, gbt_ref, ng_ref,
               y_ref, xbuf, c_st, n_st, m_st, *, L):
    ci = pl.program_id(1)
    pad = 8

    @pl.when(ci == 0)
    def _():
        xbuf[0:pad, :] = jnp.zeros((pad, ML_QK_WIDTH), F32)
        c_st[...] = jnp.zeros(c_st.shape, F32)
        n_st[...] = jnp.zeros(n_st.shape, F32)
        m_st[...] = jnp.zeros(m_st.shape, F32)

    xbuf[pad:pad + L, :] = qk_ref[...].astype(F32)
    cw = cw_ref[...]
    y = xbuf[pad:pad + L, :] * cw[ML_CONV - 1:ML_CONV, :]
    for j in range(ML_CONV - 1):
        sh = ML_CONV - 1 - j
        y = y + xbuf[pad - sh:pad - sh + L, :] * cw[j:j + 1, :]
    y = _silu(y + cb_ref[...])
    xbuf[0:pad, :] = xbuf[L:L + pad, :]

    g_c = gif_ref[...] + gb_ref[...]
    g_r = gift_ref[0] + gbt_ref[...]
    lf_c = _log_sigmoid(g_c)
    lf_r = _log_sigmoid(g_r)
    ti = lax.broadcasted_iota(I32, (L, L), 0)
    si = lax.broadcasted_iota(I32, (L, L), 1)
    causal = si <= ti
    tri = jnp.where(causal, 1.0, 0.0).astype(BF16)
    triu = jnp.where(ti <= si, 1.0, 0.0).astype(BF16)
    bc_all = sum(jnp.dot(tri, piece, preferred_element_type=F32) for piece in _split3(lf_c))
    br_all = sum(jnp.dot(piece, triu, preferred_element_type=F32) for piece in _split3(lf_r))

    for h in range(ML_HEADS):
        q = y[:, h * ML_QK_DIM:(h + 1) * ML_QK_DIM].astype(BF16)
        kf = y[:, ML_HEADS * ML_QK_DIM + h * ML_QK_DIM:ML_HEADS * ML_QK_DIM + (h + 1) * ML_QK_DIM] * (ML_QK_DIM ** -0.5)
        k = kf.astype(BF16)
        v = v_ref[:, h * ML_V_DIM:(h + 1) * ML_V_DIM]
        bc = bc_all[:, ML_HEADS + h:ML_HEADS + h + 1]
        br = br_all[ML_HEADS + h:ML_HEADS + h + 1, :]
        ig_c = g_c[:, h:h + 1]
        ig_r = g_r[h:h + 1, :]
        m_prev = m_st[h][:, 0:1]
        c_prev = c_st[h]
        n_prev = n_st[h]

        dmat = jnp.where(causal, bc - br + ig_r, -jnp.inf)
        inter = bc + m_prev
        m_t = jnp.maximum(jnp.max(dmat, axis=-1, keepdims=True), inter)
        w = jnp.exp(dmat - m_t) * _dot_nt(q, k)
        decay = jnp.exp(inter - m_t)
        num = (jnp.dot(w.astype(BF16), v, preferred_element_type=F32)
               + decay * jnp.dot(q, c_prev.astype(BF16), preferred_element_type=F32))
        qn = jnp.sum(q.astype(F32) * n_prev.astype(BF16).astype(F32), axis=-1, keepdims=True)
        den = jnp.sum(w, axis=-1, keepdims=True) + decay * qn
        hh = num / jnp.maximum(jnp.abs(den), jnp.exp(-m_t))

        b_last = br[:, L - 1:L]
        g_row = b_last - br + ig_r
        m_new = jnp.maximum(b_last + m_prev, jnp.max(g_row, axis=-1, keepdims=True))
        wk_c = jnp.exp(b_last - bc + ig_c - m_new)
        cd = jnp.exp(b_last + m_prev - m_new)
        kw = (kf * wk_c)
        c_st[h] = cd * c_prev + lax.dot_general(kw.astype(BF16), v, (((0,), (0,)), ((), ())),
                                                preferred_element_type=F32)
        n_st[h] = cd * n_prev + jnp.sum(kw, axis=0, keepdims=True)
        m_st[h] = jnp.broadcast_to(m_new, (1, LANE))

        ms = jnp.mean(hh * hh, axis=-1, keepdims=True)
        hn = hh * lax.rsqrt(ms + NORM_EPS) * ng_ref[...]
        sl = slice(h * ML_V_DIM, (h + 1) * ML_V_DIM)
        out = hn * _sigmoid(o_ref[:, sl].astype(F32)) * _silu(z_ref[:, sl].astype(F32))
        y_ref[:, sl] = out.astype(y_ref.dtype)


def _mlstm(p_main, gif, gif_t, conv_w, conv_b, gate_b, norm_g, b, s, L):
    m = b * s
    nc = s // L
    w1k = 1024
    c_qk = MAIN_OFF[SEG_ML_QK] // w1k
    c_v = MAIN_OFF[SEG_ML_V] // w1k
    c_o = MAIN_OFF[SEG_ML_O] // w1k
    c_z = MAIN_OFF[SEG_ML_Z] // w1k
    gb = gate_b.reshape(1, 2 * ML_HEADS)
    kern = functools.partial(_ml_kernel, L=L)
    full = lambda shp: pl.BlockSpec(shp, lambda bi, ci: (0,) * len(shp))
    return pl.pallas_call(
        kern,
        grid=(b, nc),
        in_specs=[
            pl.BlockSpec((L, w1k), lambda bi, ci: (bi * nc + ci, c_qk)),
            pl.BlockSpec((L, w1k), lambda bi, ci: (bi * nc + ci, c_v)),
            pl.BlockSpec((L, w1k), lambda bi, ci: (bi * nc + ci, c_o)),
            pl.BlockSpec((L, w1k), lambda bi, ci: (bi * nc + ci, c_z)),
            pl.BlockSpec((L, 2 * ML_HEADS), lambda bi, ci: (bi * nc + ci, 0)),
            pl.BlockSpec((1, 2 * ML_HEADS, L), lambda bi, ci: (bi, 0, ci)),
            full((ML_CONV, ML_QK_WIDTH)),
            full((1, ML_QK_WIDTH)),
            full((1, 2 * ML_HEADS)),
            full((2 * ML_HEADS, 1)),
            full((1, ML_V_DIM)),
        ],
        out_specs=pl.BlockSpec((L, w1k), lambda bi, ci: (bi * nc + ci, 0)),
        out_shape=jax.ShapeDtypeStruct((m, ML_HEADS * ML_V_DIM), BF16),
        scratch_shapes=[pltpu.VMEM((L + 8, ML_QK_WIDTH), F32),
                        pltpu.VMEM((ML_HEADS, ML_QK_DIM, ML_V_DIM), F32),
                        pltpu.VMEM((ML_HEADS, 1, ML_QK_DIM), F32),
                        pltpu.VMEM((ML_HEADS, 1, LANE), F32)],
        compiler_params=_cparams(("parallel", "arbitrary")),
        name="mlstm",
    )(p_main, p_main, p_main, p_main, gif, gif_t, conv_w, conv_b.reshape(1, ML_QK_WIDTH),
      gb, gb.reshape(2 * ML_HEADS, 1), norm_g.reshape(1, ML_V_DIM))


def _dsa_norm_kernel(p_ref, gkv_ref, gik_ref, c_ref, ik_ref):
    o_kv = SMALL_OFF[SEG_DSA_KV]
    o_ik = SMALL_OFF[SEG_IDX_K]
    ckv = p_ref[:, o_kv:o_kv + DSA_LATENT]
    ms = jnp.mean(ckv * ckv, axis=-1, keepdims=True)
    c_ref[...] = (ckv * lax.rsqrt(ms + NORM_EPS) * gkv_ref[...]).astype(c_ref.dtype)
    ik = p_ref[:, o_ik:o_ik + IDX_DIM]
    ms = jnp.mean(ik * ik, axis=-1, keepdims=True)
    ik_ref[...] = (ik * lax.rsqrt(ms + NORM_EPS) * gik_ref[...]).astype(ik_ref.dtype)


def _dsa_norms(p_small, g_kv, g_ik, tm):
    m = p_small.shape[0]
    return pl.pallas_call(
        _dsa_norm_kernel,
        grid=(m // tm,),
        in_specs=[pl.BlockSpec((tm, SMALL_WIDTH), lambda i: (i, 0)),
                  pl.BlockSpec((1, DSA_LATENT), lambda i: (0, 0)),
                  pl.BlockSpec((1, IDX_DIM), lambda i: (0, 0))],
        out_specs=[pl.BlockSpec((tm, DSA_LATENT), lambda i: (i, 0)),
                   pl.BlockSpec((tm, IDX_DIM), lambda i: (i, 0))],
        out_shape=[jax.ShapeDtypeStruct((m, DSA_LATENT), BF16),
                   jax.ShapeDtypeStruct((m, IDX_DIM), BF16)],
        compiler_params=_cparams(("parallel",)),
        name="dsa_norms",
    )(p_small, g_kv.reshape(1, DSA_LATENT), g_ik.reshape(1, IDX_DIM))


def _dsa_kernel(q_ref, iq_ref, iw_ref, z_ref, c_ref, ik_ref, wuv_ref, o_ref,
                key_ref, m_ref, l_ref, acc_ref, cut_ref, cn_ref, wb_ref, *, tq, tk, topk, nbits_col, s_len):
    qi = pl.program_id(1)
    nkc = (qi * tq + tq + tk - 1) // tk
    row = qi * tq + lax.broadcasted_iota(I32, (tq, tk), 0)
    col0 = lax.broadcasted_iota(I32, (tq, tk), 1)
    nlb = tk // LANE

    iw = iw_ref[...] * ((IDX_HEADS ** -0.5) * (IDX_DIM ** -0.5))
    for h in range(IDX_HEADS):
        wb_ref[h] = jnp.broadcast_to(iw[:, h:h + 1], (tq, LANE))
    iq = iq_ref[0]
    sub = 2 * LANE
    nsub = tk // sub

    def score_body(kc, carry):
        top1, top2 = carry
        off = pl.multiple_of(kc * tk, tk)
        parts = []
        for j in range(nsub):
            ikc = ik_ref[pl.ds(pl.multiple_of(off + j * sub, sub), sub), :]
            sc = _dot_nt(iq, ikc)
            part = None
            for h in range(IDX_HEADS):
                term = jnp.maximum(sc[h * tq:(h + 1) * tq, :], 0.0) * jnp.tile(wb_ref[h], (1, sub // LANE))
                part = term if part is None else part + term
            parts.append(part)
        acc = jnp.concatenate(parts, axis=1)
        bits = lax.bitcast_convert_type(acc, I32)
        skey = jnp.where(bits < 0, (bits ^ 0x7FFFFFFF) + 1, bits)
        skey = jnp.where(col0 + off <= row, skey, INT_MIN)
        key_ref[:, pl.ds(off, tk)] = skey
        for j in range(nlb):
            blk = skey[:, j * LANE:(j + 1) * LANE]
            top2 = jnp.maximum(top2, jnp.minimum(top1, blk))
            top1 = jnp.maximum(top1, blk)
        return top1, top2

    def score_group(kp, carry):
        for u in range(DSA_UNROLL):
            carry = score_body(kp * DSA_UNROLL + u, carry)
        return carry

    lowest = jnp.full((tq, LANE), INT_MIN, I32)
    tops = lax.fori_loop(0, nkc // DSA_UNROLL, score_group, (lowest, lowest))
    top1, top2 = lax.fori_loop((nkc // DSA_UNROLL) * DSA_UNROLL, nkc, score_body, tops)

    def count_ge(t_rep):
        parts = []
        for r0 in range(0, tq, COUNT_ROWS):
            t_b = jnp.tile(t_rep[r0:r0 + COUNT_ROWS], (1, nlb))

            def chunk(kc, cnt, r0=r0, t_b=t_b):
                off = pl.multiple_of(kc * tk, tk)
                hit = key_ref[r0:r0 + COUNT_ROWS, pl.ds(off, tk)] >= t_b
                return cnt + _lane_fold(jnp.where(hit, 1, 0))

            def group(kp, cnt, chunk=chunk):
                for u in range(COUNT_UNROLL):
                    cnt = chunk(kp * COUNT_UNROLL + u, cnt)
                return cnt

            cnt = lax.fori_loop(0, nkc // COUNT_UNROLL, group, jnp.zeros((COUNT_ROWS, LANE), I32))
            parts.append(lax.fori_loop((nkc // COUNT_UNROLL) * COUNT_UNROLL, nkc, chunk, cnt))
        return jnp.sum(jnp.concatenate(parts, axis=0), axis=-1, keepdims=True)

    def count(pred):
        def chunk(kc, cnt):
            off = pl.multiple_of(kc * tk, tk)
            hit = pred(key_ref[:, pl.ds(off, tk)], col0 + off)
            return cnt + _lane_fold(jnp.where(hit, 1, 0))
        cnt = lax.fori_loop(0, nkc, chunk, jnp.zeros((tq, LANE), I32))
        return jnp.sum(cnt, axis=-1, keepdims=True)

    rep = lambda v: jnp.broadcast_to(v, (tq, LANE))
    n_valid = row[:, 0:LANE] + 1
    lo0 = rep(jnp.maximum(jnp.min(top2, axis=-1, keepdims=True), INT_MIN + 1))
    hi0 = jnp.maximum(rep(jnp.max(top2 if topk > LANE + 1 else top1, axis=-1, keepdims=True)), lo0)
    searching0 = n_valid > topk

    def any_row(flag):
        return jnp.max(jnp.where(flag, 1, 0))

    def bis_cond(st):
        return st[0] > 0

    def bis_body(st):
        _, lo, hi, cnt_lo = st
        for _ in range(2):
            active = searching0 & (lo < hi) & (cnt_lo != topk)
            mid = (lo >> 1) + (hi >> 1) + ((lo | hi) & 1)
            cnt = rep(count_ge(mid))
            up = active & (cnt >= topk)
            dn = active & (cnt < topk)
            lo = jnp.where(up, mid, lo)
            cnt_lo = jnp.where(up, cnt, cnt_lo)
            hi = jnp.where(dn, mid - 1, hi)
        return any_row(searching0 & (lo < hi) & (cnt_lo != topk)), lo, hi, cnt_lo

    st0 = (any_row(searching0 & (lo0 < hi0)), lo0, hi0, jnp.full((tq, LANE), topk + 1, I32))
    _, thr, _, cnt_thr = lax.while_loop(bis_cond, bis_body, st0)
    has_ties = any_row(searching0 & (cnt_thr != topk)) > 0
    searching0 = searching0[:, 0:1]
    thr = jnp.where(searching0, thr[:, 0:1], INT_MIN + 1)

    q = q_ref[...]
    qs = jnp.concatenate([q[:, h * DSA_LATENT:(h + 1) * DSA_LATENT] for h in range(DSA_HEADS)], axis=0)
    n_groups = 4
    grows = DSA_HEADS * tq // n_groups

    cut_ref[...] = jnp.full(cut_ref.shape, s_len, I32)

    @pl.when(has_ties)
    def _():
        need = topk - count(lambda k, c: k > thr)

        def cut_body(it, cut):
            cand = cut | jnp.left_shift(jnp.int32(1), nbits_col - 1 - it)
            cnt = count(lambda k, c: (k == thr) & (c < cand))
            return jnp.where(cnt < need, cand, cut)

        cut = lax.fori_loop(0, nbits_col, cut_body, jnp.zeros((tq, 1), I32))
        cut_ref[...] = jnp.broadcast_to(cut + 1, cut_ref.shape)

    tie_cut = cut_ref[:, 0:1]

    @pl.when(qi == 0)
    def _():
        def body(i, mx):
            cf = c_ref[pl.ds(pl.multiple_of(i * tk, tk), tk), :].astype(F32)
            return jnp.maximum(mx, jnp.max(jnp.sum(cf * cf, axis=-1, keepdims=True), axis=0, keepdims=True))
        mx = lax.fori_loop(0, s_len // tk, body, jnp.zeros((1, 1), F32))
        cn_ref[...] = jnp.broadcast_to(jnp.sqrt(mx), cn_ref.shape)

    def selection_bias(kc):
        off = pl.multiple_of(kc * tk, tk)
        keys = key_ref[:, pl.ds(off, tk)]
        sel = (keys > thr) | ((keys == thr) & (col0 + off < tie_cut))
        return off, jnp.where(sel, 0.0, NEG_BIG)

    def group_scores(g, cc, bias):
        s = _dot_nt(qs[g * grows:(g + 1) * grows, :], cc)
        return (s.reshape(grows // tq, tq, tk) + bias[None]).reshape(grows, tk)

    def finalize(l_col):
        z = z_ref[...].astype(F32)
        for h in range(DSA_HEADS):
            rs = slice(h * tq, (h + 1) * tq)
            ob = (acc_ref[rs, :] / l_col[rs, :]).astype(BF16)
            oh = jnp.dot(ob, wuv_ref[h], preferred_element_type=F32)
            sl = slice(h * DSA_V_DIM, (h + 1) * DSA_V_DIM)
            o_ref[:, sl] = (oh * _silu(z[:, sl])).astype(o_ref.dtype)

    qf = qs.astype(F32)
    m_ref[...] = jnp.broadcast_to(jnp.sqrt(jnp.sum(qf * qf, axis=-1, keepdims=True))
                                  * cn_ref[:, 0:1] * BOUND_SLACK, m_ref.shape)
    l_ref[...] = jnp.zeros(l_ref.shape, F32)
    acc_ref[...] = jnp.zeros(acc_ref.shape, F32)

    def bounded_chunk(kc):
        off, bias = selection_bias(kc)
        cc = c_ref[pl.ds(off, tk), :]
        for g in range(n_groups):
            rs = slice(g * grows, (g + 1) * grows)
            p = jnp.exp2(group_scores(g, cc, bias) - jnp.tile(m_ref[rs, :], (1, nlb)))
            l_ref[rs, :] = l_ref[rs, :] + _lane_fold(p)
            acc_ref[rs, :] = acc_ref[rs, :] + jnp.dot(p.astype(BF16), cc, preferred_element_type=F32)

    def bounded_group(kp, carry):
        for u in range(DSA_UNROLL):
            bounded_chunk(kp * DSA_UNROLL + u)
        return carry

    def bounded_tail(kc, carry):
        bounded_chunk(kc)
        return carry

    lax.fori_loop(0, nkc // DSA_UNROLL, bounded_group, 0)
    lax.fori_loop((nkc // DSA_UNROLL) * DSA_UNROLL, nkc, bounded_tail, 0)
    l_tot = jnp.sum(l_ref[...], axis=-1, keepdims=True)
    underflow = jnp.logical_not(jnp.min(l_tot) >= UNDERFLOW_GUARD)

    @pl.when(jnp.logical_not(underflow))
    def _():
        finalize(l_tot)

    @pl.when(underflow)
    def _():
        m_ref[...] = jnp.full(m_ref.shape, NEG_BIG, F32)
        l_ref[...] = jnp.zeros(l_ref.shape, F32)
        acc_ref[...] = jnp.zeros(acc_ref.shape, F32)

        def body(kc, carry):
            off, bias = selection_bias(kc)
            cc = c_ref[pl.ds(off, tk), :]
            for g in range(n_groups):
                rs = slice(g * grows, (g + 1) * grows)
                s = group_scores(g, cc, bias)
                m_prev = m_ref[rs, :]
                m_new = jnp.maximum(m_prev, jnp.max(s, axis=-1, keepdims=True))
                alpha = jnp.exp2(m_prev - m_new)
                p = jnp.exp2(s - jnp.tile(m_new, (1, nlb)))
                l_ref[rs, :] = alpha * l_ref[rs, :] + jnp.sum(p, axis=-1, keepdims=True)
                acc_ref[rs, :] = (jnp.tile(alpha, (1, DSA_LATENT // LANE)) * acc_ref[rs, :]
                                  + jnp.dot(p.astype(BF16), cc, preferred_element_type=F32))
                m_ref[rs, :] = m_new
            return carry

        lax.fori_loop(0, nkc, body, 0)
        finalize(l_ref[:, 0:1])


def _dsa_attention(p_main, iw, c_lat, ik_n, w_uv, b, s, tq, tk):
    m = b * s
    nq = s // tq
    topk = min(IDX_TOPK_MAX, s // 4)
    assert topk <= 2 * LANE, "the per-lane top-two start of the threshold search covers 2 * LANE keys"
    c_q = MAIN_OFF[SEG_DSA_Q] // (DSA_HEADS * DSA_LATENT)
    c_z = MAIN_OFF[SEG_DSA_Z] // 1024
    o_iq = MAIN_OFF[SEG_IDX_Q]
    iq_hm = p_main[:, o_iq:o_iq + IDX_HEADS * IDX_DIM].reshape(b * nq, tq, IDX_HEADS, IDX_DIM)
    iq_hm = jnp.transpose(iq_hm, (0, 2, 1, 3)).reshape(b * nq, IDX_HEADS * tq, IDX_DIM)
    nbits_col = max(1, int(math.ceil(math.log2(s))) + 1)
    kern = functools.partial(_dsa_kernel, tq=tq, tk=tk, topk=topk, nbits_col=nbits_col, s_len=s)
    return pl.pallas_call(
        kern,
        grid=(b, nq),
        in_specs=[
            pl.BlockSpec((tq, DSA_HEADS * DSA_LATENT), lambda bi, qi: (bi * nq + qi, c_q)),
            pl.BlockSpec((1, IDX_HEADS * tq, IDX_DIM), lambda bi, qi: (bi * nq + qi, 0, 0)),
            pl.BlockSpec((tq, IDX_HEADS), lambda bi, qi: (bi * nq + qi, 0)),
            pl.BlockSpec((tq, 1024), lambda bi, qi: (bi * nq + qi, c_z)),
            pl.BlockSpec((s, DSA_LATENT), lambda bi, qi: (bi, 0), pipeline_mode=pl.Buffered(1)),
            pl.BlockSpec((s, IDX_DIM), lambda bi, qi: (bi, 0), pipeline_mode=pl.Buffered(1)),
            pl.BlockSpec((DSA_HEADS, DSA_LATENT, DSA_V_DIM), lambda bi, qi: (0, 0, 0)),
        ],
        out_specs=pl.BlockSpec((tq, DSA_HEADS * DSA_V_DIM), lambda bi, qi: (bi * nq + qi, 0)),
        out_shape=jax.ShapeDtypeStruct((m, DSA_HEADS * DSA_V_DIM), BF16),
        scratch_shapes=[pltpu.VMEM((tq, s), I32),
                        pltpu.VMEM((DSA_HEADS * tq, LANE), F32),
                        pltpu.VMEM((DSA_HEADS * tq, LANE), F32),
                        pltpu.VMEM((DSA_HEADS * tq, DSA_LATENT), F32),
                        pltpu.VMEM((tq, LANE), I32),
                        pltpu.VMEM((1, LANE), F32),
                        pltpu.VMEM((IDX_HEADS, tq, LANE), F32)],
        compiler_params=_cparams(("parallel", "arbitrary")),
        name="dsa_attention",
    )(p_main, iq_hm, iw, p_main, c_lat, ik_n, w_uv)


def _out_kernel(x_ref, ya_ref, yb_ref, yc_ref, ga_ref, gb_ref, gc_ref, wb_ref, wo_ref, pg_ref, o_ref):
    mixed = None
    for i, (y_ref, g_ref) in enumerate(((ya_ref, ga_ref), (yb_ref, gb_ref), (yc_ref, gc_ref))):
        t = _sigmoid(g_ref[...].astype(F32)) * jnp.dot(y_ref[...], wb_ref[i], preferred_element_type=F32)
        mixed = t if mixed is None else mixed + t
    out = jnp.dot(mixed.astype(BF16), wo_ref[...], preferred_element_type=F32)
    ms = jnp.mean(out * out, axis=-1, keepdims=True)
    o_ref[...] = x_ref[...] + out * lax.rsqrt(ms + NORM_EPS) * pg_ref[...]


def _merge_out(x2, ya, yb, yc, p_main, w_branch, w_out, post_g, tm):
    m, d = x2.shape
    cg = MAIN_OFF[SEG_GATES] // d
    row = lambda i: (i, 0)
    return pl.pallas_call(
        _out_kernel,
        grid=(m // tm,),
        in_specs=[
            pl.BlockSpec((tm, d), row),
            pl.BlockSpec((tm, BRANCH_WIDTH), row),
            pl.BlockSpec((tm, BRANCH_WIDTH), row),
            pl.BlockSpec((tm, BRANCH_WIDTH), row),
            pl.BlockSpec((tm, d), lambda i: (i, cg)),
            pl.BlockSpec((tm, d), lambda i: (i, cg + 1)),
            pl.BlockSpec((tm, d), lambda i: (i, cg + 2)),
            pl.BlockSpec((N_BRANCHES, BRANCH_WIDTH, d), lambda i: (0, 0, 0), pipeline_mode=pl.Buffered(1)),
            pl.BlockSpec((d, d), lambda i: (0, 0), pipeline_mode=pl.Buffered(1)),
            pl.BlockSpec((1, d), lambda i: (0, 0)),
        ],
        out_specs=pl.BlockSpec((tm, d), row),
        out_shape=jax.ShapeDtypeStruct((m, d), F32),
        compiler_params=_cparams(("parallel",)),
        name="merge_out_proj",
    )(x2, ya, yb, yc, p_main, p_main, p_main, w_branch, w_out, post_g.reshape(1, d))


def _pick(total, want):
    t = min(total, want)
    while total % t:
        t //= 2
    return t


def kernel(x, pre_norm_g, w_in, da_lambda, da_norm_g, ml_conv_w, ml_conv_b, ml_gate_b, ml_norm_g,
           dsa_kv_norm_g, dsa_ik_norm_g, dsa_w_uv, w_branch, w_out, post_norm_g):
    b, s, d = x.shape
    m = b * s
    bounds = _seg_bounds()
    x2 = x.reshape(m, d)
    tm_proj = _pick(m, 1024)
    t_da = _pick(s, 512)
    l_ml = _pick(s, 256)
    tq_dsa = _pick(s, 256)
    tk_dsa = _pick(s, 512)
    o_if = SMALL_OFF[SEG_ML_IF]
    o_iw = SMALL_OFF[SEG_IDX_W]
    for l in range(DEPTH):
        w = w_in[l]
        col_scale = {SEG_DA_Q: DA_HEAD_DIM ** -0.5 * LOG2E, SEG_DSA_Q: DSA_LATENT ** -0.5 * LOG2E}
        w_main = jnp.concatenate([w[:, bounds[sg]:bounds[sg + 1]] * col_scale.get(sg, 1.0) for sg in MAIN_SEGS],
                                 axis=1).astype(BF16)
        w_small = jnp.concatenate([w[:, bounds[sg]:bounds[sg + 1]] for sg in SMALL_SEGS]
                                  + [jnp.zeros((d, SMALL_WIDTH - SMALL_USED), w.dtype)], axis=1).astype(BF16)
        p_main = _project(x2, pre_norm_g[l], w_main, BF16, tm_proj, _pick(MAIN_WIDTH, 2048))
        p_small = _project(x2, pre_norm_g[l], w_small, F32, tm_proj, SMALL_WIDTH)

        lam_init = 0.8 - 0.6 * math.exp(-0.3 * l)
        y_a = _diff_attention(p_main, da_lambda[l], da_norm_g[l], lam_init, b, s, t_da)

        gif = p_small[:, o_if:o_if + 2 * ML_HEADS]
        gif_t = jnp.transpose(gif.reshape(b, s, 2 * ML_HEADS), (0, 2, 1))
        y_b = _mlstm(p_main, gif, gif_t, ml_conv_w[l], ml_conv_b[l], ml_gate_b[l], ml_norm_g[l], b, s, l_ml)

        c_lat, ik_n = _dsa_norms(p_small, dsa_kv_norm_g[l], dsa_ik_norm_g[l], _pick(m, 2048))
        iw = p_small[:, o_iw:o_iw + IDX_HEADS]
        y_c = _dsa_attention(p_main, iw, c_lat, ik_n, dsa_w_uv[l].astype(BF16), b, s, tq_dsa, tk_dsa)

        x2 = _merge_out(x2, y_a, y_b, y_c, p_main, w_branch[l].astype(BF16), w_out[l].astype(BF16),
                        post_norm_g[l], _pick(m, 256))
    return x2.reshape(b, s, d)
```

```python
import functools
import math

import jax
import jax.numpy as jnp
from jax import lax
from jax.experimental import pallas as pl
from jax.experimental.pallas import tpu as pltpu

F32 = jnp.float32
BF16 = jnp.bfloat16
I32 = jnp.int32

D_MODEL = 2048
DEPTH = 2
BRANCH_WIDTH = 1024
N_BRANCHES = 3
NORM_EPS = 1e-6
DA_HEADS = 8
DA_HEAD_DIM = 64
DA_V_DIM = 128
ML_HEADS = 4
ML_QK_DIM = 128
ML_V_DIM = 256
ML_CONV = 4
ML_QK_WIDTH = 2 * ML_HEADS * ML_QK_DIM
DSA_HEADS = 8
DSA_LATENT = 256
DSA_V_DIM = 128
IDX_HEADS = 16
IDX_DIM = 64
IDX_TOPK_MAX = 256

IN_SPLITS = (
    1024, 1024, 1024, 1024,
    ML_QK_WIDTH, 1024, 2 * ML_HEADS,
    1024, 1024,
    DSA_HEADS * DSA_LATENT, DSA_LATENT,
    IDX_HEADS * IDX_DIM, IDX_DIM, IDX_HEADS,
    1024,
    N_BRANCHES * D_MODEL,
)
(SEG_DA_Q, SEG_DA_K, SEG_DA_V, SEG_DA_Z, SEG_ML_QK, SEG_ML_V, SEG_ML_IF, SEG_ML_O, SEG_ML_Z,
 SEG_DSA_Q, SEG_DSA_KV, SEG_IDX_Q, SEG_IDX_K, SEG_IDX_W, SEG_DSA_Z, SEG_GATES) = range(16)

MAIN_SEGS = (SEG_DA_Q, SEG_DA_K, SEG_DA_V, SEG_DA_Z, SEG_ML_QK, SEG_ML_V, SEG_ML_O, SEG_ML_Z,
             SEG_DSA_Q, SEG_IDX_Q, SEG_DSA_Z, SEG_GATES)
SMALL_SEGS = (SEG_DSA_KV, SEG_IDX_K, SEG_IDX_W, SEG_ML_IF)
SMALL_WIDTH = 384

LANE = 128
VMEM_LIMIT = 56 * 1024 * 1024
INT_MIN = -2147483648
NEG_BIG = -1e30
LOG2E = math.log2(math.e)
DA_UNROLL = 4
DSA_UNROLL = 4
COUNT_ROWS = 64
COUNT_UNROLL = 4
BOUND_SLACK = 1.01
UNDERFLOW_GUARD = 2.0 ** -100


def _seg_bounds():
    offs = [0]
    for n in IN_SPLITS:
        offs.append(offs[-1] + n)
    return offs


def _main_offsets():
    out, off = {}, 0
    for s in MAIN_SEGS:
        out[s] = off
        off += IN_SPLITS[s]
    return out, off


def _small_offsets():
    out, off = {}, 0
    for s in SMALL_SEGS:
        out[s] = off
        off += IN_SPLITS[s]
    return out, off


MAIN_OFF, MAIN_WIDTH = _main_offsets()
SMALL_OFF, SMALL_USED = _small_offsets()


def _cparams(sem):
    return pltpu.CompilerParams(dimension_semantics=sem, vmem_limit_bytes=VMEM_LIMIT)


def _silu(v):
    return v * (1.0 / (1.0 + jnp.exp(-v)))


def _sigmoid(v):
    return 1.0 / (1.0 + jnp.exp(-v))


def _dot_nt(a, b):
    return lax.dot_general(a, b, (((1,), (1,)), ((), ())), preferred_element_type=F32)


def _proj_kernel(x_ref, g_ref, w_ref, ws_ref, o_ref, os_ref, h_ref):
    @pl.when(pl.program_id(1) == 0)
    def _():
        x = x_ref[...]
        ms = jnp.mean(x * x, axis=-1, keepdims=True)
        h_ref[...] = (x * lax.rsqrt(ms + NORM_EPS) * g_ref[...]).astype(BF16)
        os_ref[...] = jnp.dot(h_ref[...], ws_ref[...], preferred_element_type=F32)

    o_ref[...] = jnp.dot(h_ref[...], w_ref[...], preferred_element_type=F32).astype(o_ref.dtype)


def _project(x2, g, w, w_small, tm, tn):
    m, d = x2.shape
    n = w.shape[1]
    ns = w_small.shape[1]
    return pl.pallas_call(
        _proj_kernel,
        grid=(m // tm, n // tn),
        in_specs=[pl.BlockSpec((tm, d), lambda i, j: (i, 0)),
                  pl.BlockSpec((1, d), lambda i, j: (0, 0)),
                  pl.BlockSpec((d, tn), lambda i, j: (0, j)),
                  pl.BlockSpec((d, ns), lambda i, j: (0, 0))],
        out_specs=[pl.BlockSpec((tm, tn), lambda i, j: (i, j)),
                   pl.BlockSpec((tm, ns), lambda i, j: (i, 0))],
        out_shape=[jax.ShapeDtypeStruct((m, n), BF16), jax.ShapeDtypeStruct((m, ns), F32)],
        scratch_shapes=[pltpu.VMEM((tm, d), BF16)],
        compiler_params=_cparams(("parallel", "arbitrary")),
        name="rmsnorm_in_proj",
    )(x2, g.reshape(1, d), w, w_small)


def _lane_fold(x):
    part = x[:, 0:LANE]
    for j in range(1, x.shape[1] // LANE):
        part = part + x[:, j * LANE:(j + 1) * LANE]
    return part


def _da_kernel(lam_ref, q_ref, k_ref, v_ref, z_ref, g_ref, o_ref, m_ref, l_ref, acc_ref, kn_ref,
               *, t, s_len, lam_init):
    qi = pl.program_id(2)
    nrep = t // LANE
    half = [slice(c * DA_HEAD_DIM, (c + 1) * DA_HEAD_DIM) for c in range(2)]

    @pl.when(qi == 0)
    def _():
        def body(i, mx):
            kk = k_ref[pl.ds(pl.multiple_of(i * t, t), t), :].astype(F32)
            sq = kk * kk
            return tuple(jnp.maximum(mx[c], jnp.max(jnp.sum(sq[:, half[c]], axis=-1, keepdims=True),
                                                    axis=0, keepdims=True)) for c in range(2))
        mx = lax.fori_loop(0, s_len // t, body, (jnp.zeros((1, 1), F32), jnp.zeros((1, 1), F32)))
        for c in range(2):
            kn_ref[c] = jnp.broadcast_to(jnp.sqrt(mx[c]), (1, LANE))

    q = q_ref[...]
    qsq = q.astype(F32)
    qsq = qsq * qsq
    bound = [jnp.broadcast_to(jnp.sqrt(jnp.sum(qsq[:, half[c]], axis=-1, keepdims=True))
                              * kn_ref[c][:, 0:1] * BOUND_SLACK, (t, LANE)) for c in range(2)]

    def causal_keep():
        row = lax.broadcasted_iota(I32, (t, t), 0)
        col = lax.broadcasted_iota(I32, (t, t), 1)
        return col <= row

    def sweep(block):
        def body(kp, carry):
            for u in range(DA_UNROLL):
                block(pl.multiple_of(kp * (DA_UNROLL * t) + u * t, t), False)
            return carry
        lax.fori_loop(0, qi // DA_UNROLL, body, 0)

        def tail(ki, carry):
            block(pl.multiple_of(ki * t, t), False)
            return carry
        lax.fori_loop((qi // DA_UNROLL) * DA_UNROLL, qi, tail, 0)

        block(pl.multiple_of(qi * t, t), True)

    def bounded_block(off, masked):
        k = k_ref[pl.ds(off, t), :]
        v = v_ref[pl.ds(off, t), :]
        for c in range(2):
            s = _dot_nt(q[:, half[c]], k[:, half[c]])
            if masked:
                s = jnp.where(causal_keep(), s, -jnp.inf)
            p = jnp.exp2(s - jnp.tile(bound[c], (1, nrep)))
            l_ref[c] = l_ref[c] + _lane_fold(p)
            acc_ref[c] = acc_ref[c] + jnp.dot(p.astype(BF16), v, preferred_element_type=F32)

    def running_max_block(off, masked):
        k = k_ref[pl.ds(off, t), :]
        v = v_ref[pl.ds(off, t), :]
        for c in range(2):
            s = _dot_nt(q[:, half[c]], k[:, half[c]])
            if masked:
                s = jnp.where(causal_keep(), s, -jnp.inf)
            m_prev = m_ref[c]
            m_new = jnp.maximum(m_prev, jnp.max(s, axis=-1, keepdims=True))
            alpha = jnp.exp2(m_prev - m_new)
            p = jnp.exp2(s - jnp.tile(m_new, (1, nrep)))
            l_ref[c] = alpha * l_ref[c] + jnp.sum(p, axis=-1, keepdims=True)
            acc_ref[c] = alpha * acc_ref[c] + jnp.dot(p.astype(BF16), v, preferred_element_type=F32)
            m_ref[c] = m_new

    l_ref[...] = jnp.zeros(l_ref.shape, F32)
    acc_ref[...] = jnp.zeros(acc_ref.shape, F32)
    sweep(bounded_block)
    l_tot = [jnp.sum(l_ref[c], axis=-1, keepdims=True) for c in range(2)]
    underflow = jnp.logical_not(jnp.min(jnp.minimum(l_tot[0], l_tot[1])) >= UNDERFLOW_GUARD)

    @pl.when(jnp.logical_not(underflow))
    def _():
        for c in range(2):
            acc_ref[c] = acc_ref[c] / l_tot[c]

    @pl.when(underflow)
    def _():
        m_ref[...] = jnp.full(m_ref.shape, -jnp.inf, F32)
        l_ref[...] = jnp.zeros(l_ref.shape, F32)
        acc_ref[...] = jnp.zeros(acc_ref.shape, F32)
        sweep(running_max_block)
        for c in range(2):
            acc_ref[c] = acc_ref[c] / l_ref[c]

    lp = lam_ref[...]
    lam = (jnp.exp(jnp.sum(lp[0:1] * lp[1:2], axis=-1, keepdims=True))
           - jnp.exp(jnp.sum(lp[2:3] * lp[3:4], axis=-1, keepdims=True)) + lam_init)
    o = acc_ref[0] - lam * acc_ref[1]
    ms = jnp.mean(o * o, axis=-1, keepdims=True)
    o = o * lax.rsqrt(ms + NORM_EPS) * g_ref[...] * (1.0 - lam_init)
    o_ref[...] = (o * _silu(z_ref[...].astype(F32))).astype(o_ref.dtype)


def _diff_attention(p_main, lam_params, norm_g, lam_init, b, s, t):
    m = b * s
    nq = s // t
    cq = MAIN_OFF[SEG_DA_Q] // LANE
    ck = MAIN_OFF[SEG_DA_K] // LANE
    cv = MAIN_OFF[SEG_DA_V] // LANE
    cz = MAIN_OFF[SEG_DA_Z] // LANE
    kern = functools.partial(_da_kernel, t=t, s_len=s, lam_init=lam_init)
    return pl.pallas_call(
        kern,
        grid=(b, DA_HEADS, nq),
        in_specs=[
            pl.BlockSpec((4, DA_HEAD_DIM), lambda bi, h, qi: (0, 0)),
            pl.BlockSpec((t, LANE), lambda bi, h, qi: (bi * nq + qi, cq + h)),
            pl.BlockSpec((s, LANE), lambda bi, h, qi: (bi, ck + h)),
            pl.BlockSpec((s, LANE), lambda bi, h, qi: (bi, cv + h)),
            pl.BlockSpec((t, LANE), lambda bi, h, qi: (bi * nq + qi, cz + h)),
            pl.BlockSpec((1, DA_V_DIM), lambda bi, h, qi: (0, 0)),
        ],
        out_specs=pl.BlockSpec((t, LANE), lambda bi, h, qi: (bi * nq + qi, h)),
        out_shape=jax.ShapeDtypeStruct((m, DA_HEADS * DA_V_DIM), BF16),
        scratch_shapes=[pltpu.VMEM((2, t, LANE), F32), pltpu.VMEM((2, t, LANE), F32),
                        pltpu.VMEM((2, t, DA_V_DIM), F32), pltpu.VMEM((2, 1, LANE), F32)],
        compiler_params=_cparams(("parallel", "parallel", "arbitrary")),
        name="diff_attention",
    )(lam_params, p_main, p_main, p_main, p_main, norm_g.reshape(1, DA_V_DIM))


def _split3(v):
    hi = v.astype(BF16)
    r1 = v - hi.astype(F32)
    mid = r1.astype(BF16)
    lo = (r1 - mid.astype(F32)).astype(BF16)
    return hi, mid, lo


def _log_sigmoid(v):
    return jnp.minimum(v, 0.0) - jnp.log(1.0 + jnp.exp(-jnp.abs(v)))


def _ml_kernel(qk_ref, v_ref, o_ref, z_ref, gif_ref, gift_ref, cw_ref, cb_ref, gb_ref, gbt_ref, ng_ref,
               y_ref, xbuf, c_st, n_st, m_st, *, L):
    ci = pl.program_id(1)
    pad = 8

    @pl.when(ci == 0)
    def _():
        xbuf[0:pad, :] = jnp.zeros((pad, ML_QK_WIDTH), F32)
        c_st[...] = jnp.zeros(c_st.shape, F32)
        n_st[...] = jnp.zeros(n_st.shape, F32)
        m_st[...] = jnp.zeros(m_st.shape, F32)

    xbuf[pad:pad + L, :] = qk_ref[...].astype(F32)
    cw = cw_ref[...]
    y = xbuf[pad:pad + L, :] * cw[ML_CONV - 1:ML_CONV, :]
    for j in range(ML_CONV - 1):
        sh = ML_CONV - 1 - j
        y = y + xbuf[pad - sh:pad - sh + L, :] * cw[j:j + 1, :]
    y = _silu(y + cb_ref[...])
    xbuf[0:pad, :] = xbuf[L:L + pad, :]

    g_c = gif_ref[...] + gb_ref[...]
    g_r = gift_ref[0] + gbt_ref[...]
    lf_c = _log_sigmoid(g_c)
    lf_r = _log_sigmoid(g_r)
    ti = lax.broadcasted_iota(I32, (L, L), 0)
    si = lax.broadcasted_iota(I32, (L, L), 1)
    causal = si <= ti
    tri = jnp.where(causal, 1.0, 0.0).astype(BF16)
    triu = jnp.where(ti <= si, 1.0, 0.0).astype(BF16)
    bc_all = sum(jnp.dot(tri, piece, preferred_element_type=F32) for piece in _split3(lf_c))
    br_all = sum(jnp.dot(piece, triu, preferred_element_type=F32) for piece in _split3(lf_r))

    for h in range(ML_HEADS):
        q = y[:, h * ML_QK_DIM:(h + 1) * ML_QK_DIM].astype(BF16)
        kf = y[:, ML_HEADS * ML_QK_DIM + h * ML_QK_DIM:ML_HEADS * ML_QK_DIM + (h + 1) * ML_QK_DIM] * (ML_QK_DIM ** -0.5)
        k = kf.astype(BF16)
        v = v_ref[:, h * ML_V_DIM:(h + 1) * ML_V_DIM]
        bc = bc_all[:, ML_HEADS + h:ML_HEADS + h + 1]
        br = br_all[ML_HEADS + h:ML_HEADS + h + 1, :]
        ig_c = g_c[:, h:h + 1]
        ig_r = g_r[h:h + 1, :]
        m_prev = m_st[h][:, 0:1]
        c_prev = c_st[h]
        n_prev = n_st[h]

        dmat = jnp.where(causal, bc - br + ig_r, -jnp.inf)
        inter = bc + m_prev
        m_t = jnp.maximum(jnp.max(dmat, axis=-1, keepdims=True), inter)
        w = jnp.exp(dmat - m_t) * _dot_nt(q, k)
        decay = jnp.exp(inter - m_t)
        num = (jnp.dot(w.astype(BF16), v, preferred_element_type=F32)
               + decay * jnp.dot(q, c_prev.astype(BF16), preferred_element_type=F32))
        qn = jnp.sum(q.astype(F32) * n_prev.astype(BF16).astype(F32), axis=-1, keepdims=True)
        den = jnp.sum(w, axis=-1, keepdims=True) + decay * qn
        hh = num / jnp.maximum(jnp.abs(den), jnp.exp(-m_t))

        b_last = br[:, L - 1:L]
        g_row = b_last - br + ig_r
        m_new = jnp.maximum(b_last + m_prev, jnp.max(g_row, axis=-1, keepdims=True))
        wk_c = jnp.exp(b_last - bc + ig_c - m_new)
        cd = jnp.exp(b_last + m_prev - m_new)
        kw = (kf * wk_c)
        c_st[h] = cd * c_prev + lax.dot_general(kw.astype(BF16), v, (((0,), (0,)), ((), ())),
                                                preferred_element_type=F32)
        n_st[h] = cd * n_prev + jnp.sum(kw, axis=0, keepdims=True)
        m_st[h] = jnp.broadcast_to(m_new, (1, LANE))

        ms = jnp.mean(hh * hh, axis=-1, keepdims=True)
        hn = hh * lax.rsqrt(ms + NORM_EPS) * ng_ref[...]
        sl = slice(h * ML_V_DIM, (h + 1) * ML_V_DIM)
        out = hn * _sigmoid(o_ref[:, sl].astype(F32)) * _silu(z_ref[:, sl].astype(F32))
        y_ref[:, sl] = out.astype(y_ref.dtype)


def _mlstm(p_main, gif, gif_t, conv_w, conv_b, gate_b, norm_g, b, s, L):
    m = b * s
    nc = s // L
    w1k = 1024
    c_qk = MAIN_OFF[SEG_ML_QK] // w1k
    c_v = MAIN_OFF[SEG_ML_V] // w1k
    c_o = MAIN_OFF[SEG_ML_O] // w1k
    c_z = MAIN_OFF[SEG_ML_Z] // w1k
    gb = gate_b.reshape(1, 2 * ML_HEADS)
    kern = functools.partial(_ml_kernel, L=L)
    full = lambda shp: pl.BlockSpec(shp, lambda bi, ci: (0,) * len(shp))
    return pl.pallas_call(
        kern,
        grid=(b, nc),
        in_specs=[
            pl.BlockSpec((L, w1k), lambda bi, ci: (bi * nc + ci, c_qk)),
            pl.BlockSpec((L, w1k), lambda bi, ci: (bi * nc + ci, c_v)),
            pl.BlockSpec((L, w1k), lambda bi, ci: (bi * nc + ci, c_o)),
            pl.BlockSpec((L, w1k), lambda bi, ci: (bi * nc + ci, c_z)),
            pl.BlockSpec((L, 2 * ML_HEADS), lambda bi, ci: (bi * nc + ci, 0)),
            pl.BlockSpec((1, 2 * ML_HEADS, L), lambda bi, ci: (bi, 0, ci)),
            full((ML_CONV, ML_QK_WIDTH)),
            full((1, ML_QK_WIDTH)),
            full((1, 2 * ML_HEADS)),
            full((2 * ML_HEADS, 1)),
            full((1, ML_V_DIM)),
        ],
        out_specs=pl.BlockSpec((L, w1k), lambda bi, ci: (bi * nc + ci, 0)),
        out_shape=jax.ShapeDtypeStruct((m, ML_HEADS * ML_V_DIM), BF16),
        scratch_shapes=[pltpu.VMEM((L + 8, ML_QK_WIDTH), F32),
                        pltpu.VMEM((ML_HEADS, ML_QK_DIM, ML_V_DIM), F32),
                        pltpu.VMEM((ML_HEADS, 1, ML_QK_DIM), F32),
                        pltpu.VMEM((ML_HEADS, 1, LANE), F32)],
        compiler_params=_cparams(("parallel", "arbitrary")),
        name="mlstm",
    )(p_main, p_main, p_main, p_main, gif, gif_t, conv_w, conv_b.reshape(1, ML_QK_WIDTH),
      gb, gb.reshape(2 * ML_HEADS, 1), norm_g.reshape(1, ML_V_DIM))


def _dsa_norm_kernel(p_ref, gkv_ref, gik_ref, c_ref, ik_ref):
    o_kv = SMALL_OFF[SEG_DSA_KV]
    o_ik = SMALL_OFF[SEG_IDX_K]
    ckv = p_ref[:, o_kv:o_kv + DSA_LATENT]
    ms = jnp.mean(ckv * ckv, axis=-1, keepdims=True)
    c_ref[...] = (ckv * lax.rsqrt(ms + NORM_EPS) * gkv_ref[...]).astype(c_ref.dtype)
    ik = p_ref[:, o_ik:o_ik + IDX_DIM]
    ms = jnp.mean(ik * ik, axis=-1, keepdims=True)
    ik_ref[...] = (ik * lax.rsqrt(ms + NORM_EPS) * gik_ref[...]).astype(ik_ref.dtype)


def _dsa_norms(p_small, g_kv, g_ik, tm):
    m = p_small.shape[0]
    return pl.pallas_call(
        _dsa_norm_kernel,
        grid=(m // tm,),
        in_specs=[pl.BlockSpec((tm, SMALL_WIDTH), lambda i: (i, 0)),
                  pl.BlockSpec((1, DSA_LATENT), lambda i: (0, 0)),
                  pl.BlockSpec((1, IDX_DIM), lambda i: (0, 0))],
        out_specs=[pl.BlockSpec((tm, DSA_LATENT), lambda i: (i, 0)),
                   pl.BlockSpec((tm, IDX_DIM), lambda i: (i, 0))],
        out_shape=[jax.ShapeDtypeStruct((m, DSA_LATENT), BF16),
                   jax.ShapeDtypeStruct((m, IDX_DIM), BF16)],
        compiler_params=_cparams(("parallel",)),
        name="dsa_norms",
    )(p_small, g_kv.reshape(1, DSA_LATENT), g_ik.reshape(1, IDX_DIM))


def _dsa_kernel(q_ref, iq_ref, iw_ref, z_ref, c_ref, ik_ref, wuv_ref, o_ref,
                key_ref, m_ref, l_ref, acc_ref, cut_ref, cn_ref, wb_ref, *, tq, tk, topk, nbits_col, s_len):
    qi = pl.program_id(1)
    nkc = (qi * tq + tq + tk - 1) // tk
    row = qi * tq + lax.broadcasted_iota(I32, (tq, tk), 0)
    col0 = lax.broadcasted_iota(I32, (tq, tk), 1)
    nlb = tk // LANE

    iw = iw_ref[...] * ((IDX_HEADS ** -0.5) * (IDX_DIM ** -0.5))
    for h in range(IDX_HEADS):
        wb_ref[h] = jnp.broadcast_to(iw[:, h:h + 1], (tq, LANE))
    iq = iq_ref[0]
    sub = 2 * LANE
    nsub = tk // sub

    def score_body(kc, carry):
        top1, top2 = carry
        off = pl.multiple_of(kc * tk, tk)
        parts = []
        for j in range(nsub):
            ikc = ik_ref[pl.ds(pl.multiple_of(off + j * sub, sub), sub), :]
            sc = _dot_nt(iq, ikc)
            part = None
            for h in range(IDX_HEADS):
                term = jnp.maximum(sc[h * tq:(h + 1) * tq, :], 0.0) * jnp.tile(wb_ref[h], (1, sub // LANE))
                part = term if part is None else part + term
            parts.append(part)
        acc = jnp.concatenate(parts, axis=1)
        bits = lax.bitcast_convert_type(acc, I32)
        skey = jnp.where(bits < 0, (bits ^ 0x7FFFFFFF) + 1, bits)
        skey = jnp.where(col0 + off <= row, skey, INT_MIN)
        key_ref[:, pl.ds(off, tk)] = skey
        for j in range(nlb):
            blk = skey[:, j * LANE:(j + 1) * LANE]
            top2 = jnp.maximum(top2, jnp.minimum(top1, blk))
            top1 = jnp.maximum(top1, blk)
        return top1, top2

    def score_group(kp, carry):
        for u in range(DSA_UNROLL):
            carry = score_body(kp * DSA_UNROLL + u, carry)
        return carry

    lowest = jnp.full((tq, LANE), INT_MIN, I32)
    tops = lax.fori_loop(0, nkc // DSA_UNROLL, score_group, (lowest, lowest))
    top1, top2 = lax.fori_loop((nkc // DSA_UNROLL) * DSA_UNROLL, nkc, score_body, tops)

    n_blocks = tq // COUNT_ROWS

    def count_ge(t_rep, wanted):
        parts = []
        for blk in range(n_blocks):
            r0 = blk * COUNT_ROWS
            t_b = jnp.tile(t_rep[r0:r0 + COUNT_ROWS], (1, nlb))
            n_sweep = jnp.where(wanted[blk] > 0, nkc, 0)

            def chunk(kc, cnt, r0=r0, t_b=t_b):
                off = pl.multiple_of(kc * tk, tk)
                hit = key_ref[r0:r0 + COUNT_ROWS, pl.ds(off, tk)] >= t_b
                return cnt + _lane_fold(jnp.where(hit, 1, 0))

            def group(kp, cnt, chunk=chunk):
                for u in range(COUNT_UNROLL):
                    cnt = chunk(kp * COUNT_UNROLL + u, cnt)
                return cnt

            cnt = lax.fori_loop(0, n_sweep // COUNT_UNROLL, group, jnp.zeros((COUNT_ROWS, LANE), I32))
            parts.append(lax.fori_loop((n_sweep // COUNT_UNROLL) * COUNT_UNROLL, n_sweep, chunk, cnt))
        return jnp.sum(jnp.concatenate(parts, axis=0), axis=-1, keepdims=True)

    def count(pred):
        def chunk(kc, cnt):
            off = pl.multiple_of(kc * tk, tk)
            hit = pred(key_ref[:, pl.ds(off, tk)], col0 + off)
            return cnt + _lane_fold(jnp.where(hit, 1, 0))
        cnt = lax.fori_loop(0, nkc, chunk, jnp.zeros((tq, LANE), I32))
        return jnp.sum(cnt, axis=-1, keepdims=True)

    rep = lambda v: jnp.broadcast_to(v, (tq, LANE))
    n_valid = row[:, 0:LANE] + 1
    lo0 = rep(jnp.maximum(jnp.min(top2, axis=-1, keepdims=True), INT_MIN + 1))
    hi0 = jnp.maximum(rep(jnp.max(top2 if topk > LANE + 1 else top1, axis=-1, keepdims=True)), lo0)
    searching0 = n_valid > topk

    def any_row(flag):
        return jnp.max(jnp.where(flag, 1, 0))

    def block_flags(active):
        return tuple(any_row(active[blk * COUNT_ROWS:(blk + 1) * COUNT_ROWS]) for blk in range(n_blocks))

    def bis_cond(st):
        flags = st[0]
        busy = flags[0]
        for f in flags[1:]:
            busy = jnp.maximum(busy, f)
        return busy > 0

    def bis_body(st):
        flags, lo, hi, cnt_lo = st
        for _ in range(2):
            active = searching0 & (lo < hi) & (cnt_lo != topk)
            mid = (lo >> 1) + (hi >> 1) + ((lo | hi) & 1)
            cnt = rep(count_ge(mid, flags))
            up = active & (cnt >= topk)
            dn = active & (cnt < topk)
            lo = jnp.where(up, mid, lo)
            cnt_lo = jnp.where(up, cnt, cnt_lo)
            hi = jnp.where(dn, mid - 1, hi)
        return block_flags(searching0 & (lo < hi) & (cnt_lo != topk)), lo, hi, cnt_lo

    st0 = (block_flags(searching0 & (lo0 < hi0)), lo0, hi0, jnp.full((tq, LANE), topk + 1, I32))
    _, thr, _, cnt_thr = lax.while_loop(bis_cond, bis_body, st0)
    has_ties = any_row(searching0 & (cnt_thr != topk)) > 0
    searching0 = searching0[:, 0:1]
    thr = jnp.where(searching0, thr[:, 0:1], INT_MIN + 1)

    q = q_ref[...]
    qs = jnp.concatenate([q[:, h * DSA_LATENT:(h + 1) * DSA_LATENT] for h in range(DSA_HEADS)], axis=0)
    n_groups = 4
    grows = DSA_HEADS * tq // n_groups

    cut_ref[...] = jnp.full(cut_ref.shape, s_len, I32)

    @pl.when(has_ties)
    def _():
        need = topk - count(lambda k, c: k > thr)

        def cut_body(it, cut):
            cand = cut | jnp.left_shift(jnp.int32(1), nbits_col - 1 - it)
            cnt = count(lambda k, c: (k == thr) & (c < cand))
            return jnp.where(cnt < need, cand, cut)

        cut = lax.fori_loop(0, nbits_col, cut_body, jnp.zeros((tq, 1), I32))
        cut_ref[...] = jnp.broadcast_to(cut + 1, cut_ref.shape)

    tie_cut = cut_ref[:, 0:1]

    @pl.when(qi == 0)
    def _():
        def body(i, mx):
            cf = c_ref[pl.ds(pl.multiple_of(i * tk, tk), tk), :].astype(F32)
            return jnp.maximum(mx, jnp.max(jnp.sum(cf * cf, axis=-1, keepdims=True), axis=0, keepdims=True))
        mx = lax.fori_loop(0, s_len // tk, body, jnp.zeros((1, 1), F32))
        cn_ref[...] = jnp.broadcast_to(jnp.sqrt(mx), cn_ref.shape)

    def selection_bias(kc):
        off = pl.multiple_of(kc * tk, tk)
        keys = key_ref[:, pl.ds(off, tk)]
        sel = (keys > thr) | ((keys == thr) & (col0 + off < tie_cut))
        return off, jnp.where(sel, 0.0, NEG_BIG)

    def group_scores(g, cc, bias):
        s = _dot_nt(qs[g * grows:(g + 1) * grows, :], cc)
        return (s.reshape(grows // tq, tq, tk) + bias[None]).reshape(grows, tk)

    def finalize(l_col):
        z = z_ref[...].astype(F32)
        for h in range(DSA_HEADS):
            rs = slice(h * tq, (h + 1) * tq)
            ob = (acc_ref[rs, :] / l_col[rs, :]).astype(BF16)
            oh = jnp.dot(ob, wuv_ref[h], preferred_element_type=F32)
            sl = slice(h * DSA_V_DIM, (h + 1) * DSA_V_DIM)
            o_ref[:, sl] = (oh * _silu(z[:, sl])).astype(o_ref.dtype)

    qf = qs.astype(F32)
    m_ref[...] = jnp.broadcast_to(jnp.sqrt(jnp.sum(qf * qf, axis=-1, keepdims=True))
                                  * cn_ref[:, 0:1] * BOUND_SLACK, m_ref.shape)
    l_ref[...] = jnp.zeros(l_ref.shape, F32)
    acc_ref[...] = jnp.zeros(acc_ref.shape, F32)

    def bounded_chunk(kc):
        off, bias = selection_bias(kc)
        cc = c_ref[pl.ds(off, tk), :]
        for g in range(n_groups):
            rs = slice(g * grows, (g + 1) * grows)
            p = jnp.exp2(group_scores(g, cc, bias) - jnp.tile(m_ref[rs, :], (1, nlb)))
            l_ref[rs, :] = l_ref[rs, :] + _lane_fold(p)
            acc_ref[rs, :] = acc_ref[rs, :] + jnp.dot(p.astype(BF16), cc, preferred_element_type=F32)

    def bounded_group(kp, carry):
        for u in range(DSA_UNROLL):
            bounded_chunk(kp * DSA_UNROLL + u)
        return carry

    def bounded_tail(kc, carry):
        bounded_chunk(kc)
        return carry

    lax.fori_loop(0, nkc // DSA_UNROLL, bounded_group, 0)
    lax.fori_loop((nkc // DSA_UNROLL) * DSA_UNROLL, nkc, bounded_tail, 0)
    l_tot = jnp.sum(l_ref[...], axis=-1, keepdims=True)
    underflow = jnp.logical_not(jnp.min(l_tot) >= UNDERFLOW_GUARD)

    @pl.when(jnp.logical_not(underflow))
    def _():
        finalize(l_tot)

    @pl.when(underflow)
    def _():
        m_ref[...] = jnp.full(m_ref.shape, NEG_BIG, F32)
        l_ref[...] = jnp.zeros(l_ref.shape, F32)
        acc_ref[...] = jnp.zeros(acc_ref.shape, F32)

        def body(kc, carry):
            off, bias = selection_bias(kc)
            cc = c_ref[pl.ds(off, tk), :]
            for g in range(n_groups):
                rs = slice(g * grows, (g + 1) * grows)
                s = group_scores(g, cc, bias)
                m_prev = m_ref[rs, :]
                m_new = jnp.maximum(m_prev, jnp.max(s, axis=-1, keepdims=True))
                alpha = jnp.exp2(m_prev - m_new)
                p = jnp.exp2(s - jnp.tile(m_new, (1, nlb)))
                l_ref[rs, :] = alpha * l_ref[rs, :] + jnp.sum(p, axis=-1, keepdims=True)
                acc_ref[rs, :] = (jnp.tile(alpha, (1, DSA_LATENT // LANE)) * acc_ref[rs, :]
                                  + jnp.dot(p.astype(BF16), cc, preferred_element_type=F32))
                m_ref[rs, :] = m_new
            return carry

        lax.fori_loop(0, nkc, body, 0)
        finalize(l_ref[:, 0:1])


def _dsa_attention(p_main, iw, c_lat, ik_n, w_uv, b, s, tq, tk):
    m = b * s
    nq = s // tq
    topk = min(IDX_TOPK_MAX, s // 4)
    assert topk <= 2 * LANE, "the per-lane top-two start of the threshold search covers 2 * LANE keys"
    c_q = MAIN_OFF[SEG_DSA_Q] // (DSA_HEADS * DSA_LATENT)
    c_z = MAIN_OFF[SEG_DSA_Z] // 1024
    o_iq = MAIN_OFF[SEG_IDX_Q]
    iq_hm = p_main[:, o_iq:o_iq + IDX_HEADS * IDX_DIM].reshape(b * nq, tq, IDX_HEADS, IDX_DIM)
    iq_hm = jnp.transpose(iq_hm, (0, 2, 1, 3)).reshape(b * nq, IDX_HEADS * tq, IDX_DIM)
    nbits_col = max(1, int(math.ceil(math.log2(s))) + 1)
    kern = functools.partial(_dsa_kernel, tq=tq, tk=tk, topk=topk, nbits_col=nbits_col, s_len=s)
    return pl.pallas_call(
        kern,
        grid=(b, nq),
        in_specs=[
            pl.BlockSpec((tq, DSA_HEADS * DSA_LATENT), lambda bi, qi: (bi * nq + qi, c_q)),
            pl.BlockSpec((1, IDX_HEADS * tq, IDX_DIM), lambda bi, qi: (bi * nq + qi, 0, 0)),
            pl.BlockSpec((tq, IDX_HEADS), lambda bi, qi: (bi * nq + qi, 0)),
            pl.BlockSpec((tq, 1024), lambda bi, qi: (bi * nq + qi, c_z)),
            pl.BlockSpec((s, DSA_LATENT), lambda bi, qi: (bi, 0), pipeline_mode=pl.Buffered(1)),
            pl.BlockSpec((s, IDX_DIM), lambda bi, qi: (bi, 0), pipeline_mode=pl.Buffered(1)),
            pl.BlockSpec((DSA_HEADS, DSA_LATENT, DSA_V_DIM), lambda bi, qi: (0, 0, 0)),
        ],
        out_specs=pl.BlockSpec((tq, DSA_HEADS * DSA_V_DIM), lambda bi, qi: (bi * nq + qi, 0)),
        out_shape=jax.ShapeDtypeStruct((m, DSA_HEADS * DSA_V_DIM), BF16),
        scratch_shapes=[pltpu.VMEM((tq, s), I32),
                        pltpu.VMEM((DSA_HEADS * tq, LANE), F32),
                        pltpu.VMEM((DSA_HEADS * tq, LANE), F32),
                        pltpu.VMEM((DSA_HEADS * tq, DSA_LATENT), F32),
                        pltpu.VMEM((tq, LANE), I32),
                        pltpu.VMEM((1, LANE), F32),
                        pltpu.VMEM((IDX_HEADS, tq, LANE), F32)],
        compiler_params=_cparams(("parallel", "arbitrary")),
        name="dsa_attention",
    )(p_main, iq_hm, iw, p_main, c_lat, ik_n, w_uv)


def _out_kernel(x_ref, ya_ref, yb_ref, yc_ref, ga_ref, gb_ref, gc_ref, wb_ref, wo_ref, pg_ref, o_ref):
    mixed = None
    for i, (y_ref, g_ref) in enumerate(((ya_ref, ga_ref), (yb_ref, gb_ref), (yc_ref, gc_ref))):
        t = _sigmoid(g_ref[...].astype(F32)) * jnp.dot(y_ref[...], wb_ref[i], preferred_element_type=F32)
        mixed = t if mixed is None else mixed + t
    out = jnp.dot(mixed.astype(BF16), wo_ref[...], preferred_element_type=F32)
    ms = jnp.mean(out * out, axis=-1, keepdims=True)
    o_ref[...] = x_ref[...] + out * lax.rsqrt(ms + NORM_EPS) * pg_ref[...]


def _merge_out(x2, ya, yb, yc, p_main, w_branch, w_out, post_g, tm):
    m, d = x2.shape
    cg = MAIN_OFF[SEG_GATES] // d
    row = lambda i: (i, 0)
    return pl.pallas_call(
        _out_kernel,
        grid=(m // tm,),
        in_specs=[
            pl.BlockSpec((tm, d), row),
            pl.BlockSpec((tm, BRANCH_WIDTH), row),
            pl.BlockSpec((tm, BRANCH_WIDTH), row),
            pl.BlockSpec((tm, BRANCH_WIDTH), row),
            pl.BlockSpec((tm, d), lambda i: (i, cg)),
            pl.BlockSpec((tm, d), lambda i: (i, cg + 1)),
            pl.BlockSpec((tm, d), lambda i: (i, cg + 2)),
            pl.BlockSpec((N_BRANCHES, BRANCH_WIDTH, d), lambda i: (0, 0, 0), pipeline_mode=pl.Buffered(1)),
            pl.BlockSpec((d, d), lambda i: (0, 0), pipeline_mode=pl.Buffered(1)),
            pl.BlockSpec((1, d), lambda i: (0, 0)),
        ],
        out_specs=pl.BlockSpec((tm, d), row),
        out_shape=jax.ShapeDtypeStruct((m, d), F32),
        compiler_params=_cparams(("parallel",)),
        name="merge_out_proj",
    )(x2, ya, yb, yc, p_main, p_main, p_main, w_branch, w_out, post_g.reshape(1, d))


def _pick(total, want):
    t = min(total, want)
    while total % t:
        t //= 2
    return t


def kernel(x, pre_norm_g, w_in, da_lambda, da_norm_g, ml_conv_w, ml_conv_b, ml_gate_b, ml_norm_g,
           dsa_kv_norm_g, dsa_ik_norm_g, dsa_w_uv, w_branch, w_out, post_norm_g):
    b, s, d = x.shape
    m = b * s
    bounds = _seg_bounds()
    x2 = x.reshape(m, d)
    tm_proj = _pick(m, 1024)
    t_da = _pick(s, 512)
    l_ml = _pick(s, 256)
    tq_dsa = _pick(s, 256)
    tk_dsa = _pick(s, 512)
    o_if = SMALL_OFF[SEG_ML_IF]
    o_iw = SMALL_OFF[SEG_IDX_W]
    for l in range(DEPTH):
        w = w_in[l]
        col_scale = {SEG_DA_Q: DA_HEAD_DIM ** -0.5 * LOG2E, SEG_DSA_Q: DSA_LATENT ** -0.5 * LOG2E}
        w_main = jnp.concatenate([w[:, bounds[sg]:bounds[sg + 1]] * col_scale.get(sg, 1.0) for sg in MAIN_SEGS],
                                 axis=1).astype(BF16)
        w_small = jnp.concatenate([w[:, bounds[sg]:bounds[sg + 1]] for sg in SMALL_SEGS]
                                  + [jnp.zeros((d, SMALL_WIDTH - SMALL_USED), w.dtype)], axis=1).astype(BF16)
        p_main, p_small = _project(x2, pre_norm_g[l], w_main, w_small, tm_proj, _pick(MAIN_WIDTH, 2048))

        lam_init = 0.8 - 0.6 * math.exp(-0.3 * l)
        y_a = _diff_attention(p_main, da_lambda[l], da_norm_g[l], lam_init, b, s, t_da)

        gif = p_small[:, o_if:o_if + 2 * ML_HEADS]
        gif_t = jnp.transpose(gif.reshape(b, s, 2 * ML_HEADS), (0, 2, 1))
        y_b = _mlstm(p_main, gif, gif_t, ml_conv_w[l], ml_conv_b[l], ml_gate_b[l], ml_norm_g[l], b, s, l_ml)

        c_lat, ik_n = _dsa_norms(p_small, dsa_kv_norm_g[l], dsa_ik_norm_g[l], _pick(m, 2048))
        iw = p_small[:, o_iw:o_iw + IDX_HEADS]
        y_c = _dsa_attention(p_main, iw, c_lat, ik_n, dsa_w_uv[l].astype(BF16), b, s, tq_dsa, tk_dsa)

        x2 = _merge_out(x2, y_a, y_b, y_c, p_main, w_branch[l].astype(BF16), w_out[l].astype(BF16),
                        post_norm_g[l], _pick(m, 256))
    return x2.reshape(b, s, d)
```

```python
import functools
import math

import jax
import jax.numpy as jnp
from jax import lax
from jax.experimental import pallas as pl
from jax.experimental.pallas import tpu as pltpu

F32 = jnp.float32
BF16 = jnp.bfloat16
I32 = jnp.int32

D_MODEL = 2048
DEPTH = 2
BRANCH_WIDTH = 1024
N_BRANCHES = 3
NORM_EPS = 1e-6
DA_HEADS = 8
DA_HEAD_DIM = 64
DA_V_DIM = 128
ML_HEADS = 4
ML_QK_DIM = 128
ML_V_DIM = 256
ML_CONV = 4
ML_QK_WIDTH = 2 * ML_HEADS * ML_QK_DIM
DSA_HEADS = 8
DSA_LATENT = 256
DSA_V_DIM = 128
IDX_HEADS = 16
IDX_DIM = 64
IDX_TOPK_MAX = 256

IN_SPLITS = (
    DA_HEADS * 2 * DA_HEAD_DIM, DA_HEADS * 2 * DA_HEAD_DIM, DA_HEADS * DA_V_DIM, BRANCH_WIDTH,
    ML_QK_WIDTH, ML_HEADS * ML_V_DIM, 2 * ML_HEADS,
    BRANCH_WIDTH, BRANCH_WIDTH,
    DSA_HEADS * DSA_LATENT, DSA_LATENT,
    IDX_HEADS * IDX_DIM, IDX_DIM, IDX_HEADS,
    BRANCH_WIDTH,
    N_BRANCHES * D_MODEL,
)
(SEG_DA_Q, SEG_DA_K, SEG_DA_V, SEG_DA_Z, SEG_ML_QK, SEG_ML_V, SEG_ML_IF, SEG_ML_O, SEG_ML_Z,
 SEG_DSA_Q, SEG_DSA_KV, SEG_IDX_Q, SEG_IDX_K, SEG_IDX_W, SEG_DSA_Z, SEG_GATES) = range(16)

MAIN_SEGS = (SEG_DA_Q, SEG_DA_K, SEG_DA_V, SEG_DA_Z, SEG_ML_QK, SEG_ML_V, SEG_ML_O, SEG_ML_Z,
             SEG_DSA_Q, SEG_IDX_Q, SEG_DSA_Z, SEG_GATES)
SMALL_SEGS = (SEG_DSA_KV, SEG_IDX_K, SEG_IDX_W, SEG_ML_IF)
SMALL_WIDTH = 384

LANE = 128
SUBLANE = 8
V7X_VMEM_BYTES = 64 * 1024 * 1024
VMEM_LIMIT = V7X_VMEM_BYTES * 7 // 8
INT_MIN = -2147483648
NEG_BIG = -1e30
LOG2E = math.log2(math.e)
DA_UNROLL = 4
DSA_UNROLL = 4
COUNT_ROWS = 64
COUNT_UNROLL = 4
BOUND_SLACK = 1.01
UNDERFLOW_GUARD = 2.0 ** -100


def _seg_bounds():
    offs = [0]
    for n in IN_SPLITS:
        offs.append(offs[-1] + n)
    return offs


def _main_offsets():
    out, off = {}, 0
    for s in MAIN_SEGS:
        out[s] = off
        off += IN_SPLITS[s]
    return out, off


def _small_offsets():
    out, off = {}, 0
    for s in SMALL_SEGS:
        out[s] = off
        off += IN_SPLITS[s]
    return out, off


MAIN_OFF, MAIN_WIDTH = _main_offsets()
SMALL_OFF, SMALL_USED = _small_offsets()


def _cparams(sem):
    return pltpu.CompilerParams(dimension_semantics=sem, vmem_limit_bytes=VMEM_LIMIT)


def _silu(v):
    return v * (1.0 / (1.0 + jnp.exp(-v)))


def _sigmoid(v):
    return 1.0 / (1.0 + jnp.exp(-v))


def _dot_nt(a, b):
    return lax.dot_general(a, b, (((1,), (1,)), ((), ())), preferred_element_type=F32)


def _proj_kernel(x_ref, g_ref, w_ref, ws_ref, o_ref, os_ref, h_ref):
    @pl.when(pl.program_id(1) == 0)
    def _():
        x = x_ref[...]
        ms = jnp.mean(x * x, axis=-1, keepdims=True)
        h_ref[...] = (x * lax.rsqrt(ms + NORM_EPS) * g_ref[...]).astype(BF16)
        os_ref[...] = jnp.dot(h_ref[...], ws_ref[...], preferred_element_type=F32)

    o_ref[...] = jnp.dot(h_ref[...], w_ref[...], preferred_element_type=F32).astype(o_ref.dtype)


def _project(x2, g, w, w_small, tm, tn):
    m, d = x2.shape
    n = w.shape[1]
    ns = w_small.shape[1]
    return pl.pallas_call(
        _proj_kernel,
        grid=(m // tm, n // tn),
        in_specs=[pl.BlockSpec((tm, d), lambda i, j: (i, 0)),
                  pl.BlockSpec((1, d), lambda i, j: (0, 0)),
                  pl.BlockSpec((d, tn), lambda i, j: (0, j)),
                  pl.BlockSpec((d, ns), lambda i, j: (0, 0))],
        out_specs=[pl.BlockSpec((tm, tn), lambda i, j: (i, j)),
                   pl.BlockSpec((tm, ns), lambda i, j: (i, 0))],
        out_shape=[jax.ShapeDtypeStruct((m, n), BF16), jax.ShapeDtypeStruct((m, ns), F32)],
        scratch_shapes=[pltpu.VMEM((tm, d), BF16)],
        compiler_params=_cparams(("parallel", "arbitrary")),
        name="rmsnorm_in_proj",
    )(x2, g.reshape(1, d), w, w_small)


def _lane_fold(x):
    part = x[:, 0:LANE]
    for j in range(1, x.shape[1] // LANE):
        part = part + x[:, j * LANE:(j + 1) * LANE]
    return part


def _da_kernel(lam_ref, q_ref, k_ref, v_ref, z_ref, g_ref, o_ref, m_ref, l_ref, acc_ref, kn_ref,
               *, t, s_len, lam_init):
    qi = pl.program_id(2)
    nrep = t // LANE
    half = [slice(c * DA_HEAD_DIM, (c + 1) * DA_HEAD_DIM) for c in range(2)]

    @pl.when(qi == 0)
    def _():
        def body(i, mx):
            kk = k_ref[pl.ds(pl.multiple_of(i * t, t), t), :].astype(F32)
            sq = kk * kk
            return tuple(jnp.maximum(mx[c], jnp.max(jnp.sum(sq[:, half[c]], axis=-1, keepdims=True),
                                                    axis=0, keepdims=True)) for c in range(2))
        mx = lax.fori_loop(0, s_len // t, body, (jnp.zeros((1, 1), F32), jnp.zeros((1, 1), F32)))
        for c in range(2):
            kn_ref[c] = jnp.broadcast_to(jnp.sqrt(mx[c]), (1, LANE))

    q = q_ref[...]
    qsq = q.astype(F32)
    qsq = qsq * qsq
    bound = [jnp.broadcast_to(jnp.sqrt(jnp.sum(qsq[:, half[c]], axis=-1, keepdims=True))
                              * kn_ref[c][:, 0:1] * BOUND_SLACK, (t, LANE)) for c in range(2)]

    def causal_keep():
        row = lax.broadcasted_iota(I32, (t, t), 0)
        col = lax.broadcasted_iota(I32, (t, t), 1)
        return col <= row

    def sweep(block):
        def body(kp, carry):
            for u in range(DA_UNROLL):
                block(pl.multiple_of(kp * (DA_UNROLL * t) + u * t, t), False)
            return carry
        lax.fori_loop(0, qi // DA_UNROLL, body, 0)

        def tail(ki, carry):
            block(pl.multiple_of(ki * t, t), False)
            return carry
        lax.fori_loop((qi // DA_UNROLL) * DA_UNROLL, qi, tail, 0)

        block(pl.multiple_of(qi * t, t), True)

    def bounded_block(off, masked):
        k = k_ref[pl.ds(off, t), :]
        v = v_ref[pl.ds(off, t), :]
        for c in range(2):
            s = _dot_nt(q[:, half[c]], k[:, half[c]])
            if masked:
                s = jnp.where(causal_keep(), s, -jnp.inf)
            p = jnp.exp2(s - jnp.tile(bound[c], (1, nrep)))
            l_ref[c] = l_ref[c] + _lane_fold(p)
            acc_ref[c] = acc_ref[c] + jnp.dot(p.astype(BF16), v, preferred_element_type=F32)

    def running_max_block(off, masked):
        k = k_ref[pl.ds(off, t), :]
        v = v_ref[pl.ds(off, t), :]
        for c in range(2):
            s = _dot_nt(q[:, half[c]], k[:, half[c]])
            if masked:
                s = jnp.where(causal_keep(), s, -jnp.inf)
            m_prev = m_ref[c]
            m_new = jnp.maximum(m_prev, jnp.max(s, axis=-1, keepdims=True))
            alpha = jnp.exp2(m_prev - m_new)
            p = jnp.exp2(s - jnp.tile(m_new, (1, nrep)))
            l_ref[c] = alpha * l_ref[c] + jnp.sum(p, axis=-1, keepdims=True)
            acc_ref[c] = alpha * acc_ref[c] + jnp.dot(p.astype(BF16), v, preferred_element_type=F32)
            m_ref[c] = m_new

    l_ref[...] = jnp.zeros(l_ref.shape, F32)
    acc_ref[...] = jnp.zeros(acc_ref.shape, F32)
    sweep(bounded_block)
    l_tot = [jnp.sum(l_ref[c], axis=-1, keepdims=True) for c in range(2)]
    underflow = jnp.logical_not(jnp.min(jnp.minimum(l_tot[0], l_tot[1])) >= UNDERFLOW_GUARD)

    @pl.when(jnp.logical_not(underflow))
    def _():
        for c in range(2):
            acc_ref[c] = acc_ref[c] / l_tot[c]

    @pl.when(underflow)
    def _():
        m_ref[...] = jnp.full(m_ref.shape, -jnp.inf, F32)
        l_ref[...] = jnp.zeros(l_ref.shape, F32)
        acc_ref[...] = jnp.zeros(acc_ref.shape, F32)
        sweep(running_max_block)
        for c in range(2):
            acc_ref[c] = acc_ref[c] / l_ref[c]

    lp = lam_ref[...]
    lam = (jnp.exp(jnp.sum(lp[0:1] * lp[1:2], axis=-1, keepdims=True))
           - jnp.exp(jnp.sum(lp[2:3] * lp[3:4], axis=-1, keepdims=True)) + lam_init)
    o = acc_ref[0] - lam * acc_ref[1]
    ms = jnp.mean(o * o, axis=-1, keepdims=True)
    o = o * lax.rsqrt(ms + NORM_EPS) * g_ref[...] * (1.0 - lam_init)
    o_ref[...] = (o * _silu(z_ref[...].astype(F32))).astype(o_ref.dtype)


def _diff_attention(p_main, lam_params, norm_g, lam_init, b, s, t):
    m = b * s
    nq = s // t
    cq = MAIN_OFF[SEG_DA_Q] // LANE
    ck = MAIN_OFF[SEG_DA_K] // LANE
    cv = MAIN_OFF[SEG_DA_V] // LANE
    cz = MAIN_OFF[SEG_DA_Z] // LANE
    kern = functools.partial(_da_kernel, t=t, s_len=s, lam_init=lam_init)
    return pl.pallas_call(
        kern,
        grid=(b, DA_HEADS, nq),
        in_specs=[
            pl.BlockSpec((4, DA_HEAD_DIM), lambda bi, h, qi: (0, 0)),
            pl.BlockSpec((t, LANE), lambda bi, h, qi: (bi * nq + qi, cq + h)),
            pl.BlockSpec((s, LANE), lambda bi, h, qi: (bi, ck + h)),
            pl.BlockSpec((s, LANE), lambda bi, h, qi: (bi, cv + h)),
            pl.BlockSpec((t, LANE), lambda bi, h, qi: (bi * nq + qi, cz + h)),
            pl.BlockSpec((1, DA_V_DIM), lambda bi, h, qi: (0, 0)),
        ],
        out_specs=pl.BlockSpec((t, LANE), lambda bi, h, qi: (bi * nq + qi, h)),
        out_shape=jax.ShapeDtypeStruct((m, DA_HEADS * DA_V_DIM), BF16),
        scratch_shapes=[pltpu.VMEM((2, t, LANE), F32), pltpu.VMEM((2, t, LANE), F32),
                        pltpu.VMEM((2, t, DA_V_DIM), F32), pltpu.VMEM((2, 1, LANE), F32)],
        compiler_params=_cparams(("parallel", "parallel", "arbitrary")),
        name="diff_attention",
    )(lam_params, p_main, p_main, p_main, p_main, norm_g.reshape(1, DA_V_DIM))


def _split3(v):
    hi = v.astype(BF16)
    r1 = v - hi.astype(F32)
    mid = r1.astype(BF16)
    lo = (r1 - mid.astype(F32)).astype(BF16)
    return hi, mid, lo


def _log_sigmoid(v):
    return jnp.minimum(v, 0.0) - jnp.log(1.0 + jnp.exp(-jnp.abs(v)))


def _ml_kernel(qk_ref, v_ref, o_ref, z_ref, gif_ref, gift_ref, cw_ref, cb_ref, gb_ref, gbt_ref, ng_ref,
               y_ref, xbuf, c_st, n_st, m_st, *, L):
    ci = pl.program_id(1)
    pad = SUBLANE

    @pl.when(ci == 0)
    def _():
        xbuf[0:pad, :] = jnp.zeros((pad, ML_QK_WIDTH), F32)
        c_st[...] = jnp.zeros(c_st.shape, F32)
        n_st[...] = jnp.zeros(n_st.shape, F32)
        m_st[...] = jnp.zeros(m_st.shape, F32)

    xbuf[pad:pad + L, :] = qk_ref[...].astype(F32)
    cw = cw_ref[...]
    y = xbuf[pad:pad + L, :] * cw[ML_CONV - 1:ML_CONV, :]
    for j in range(ML_CONV - 1):
        sh = ML_CONV - 1 - j
        y = y + xbuf[pad - sh:pad - sh + L, :] * cw[j:j + 1, :]
    y = _silu(y + cb_ref[...])
    xbuf[0:pad, :] = xbuf[L:L + pad, :]

    g_c = gif_ref[...] + gb_ref[...]
    g_r = gift_ref[0] + gbt_ref[...]
    lf_c = _log_sigmoid(g_c)
    lf_r = _log_sigmoid(g_r)
    ti = lax.broadcasted_iota(I32, (L, L), 0)
    si = lax.broadcasted_iota(I32, (L, L), 1)
    causal = si <= ti
    tri = jnp.where(causal, 1.0, 0.0).astype(BF16)
    triu = jnp.where(ti <= si, 1.0, 0.0).astype(BF16)
    bc_all = sum(jnp.dot(tri, piece, preferred_element_type=F32) for piece in _split3(lf_c))
    br_all = sum(jnp.dot(piece, triu, preferred_element_type=F32) for piece in _split3(lf_r))

    for h in range(ML_HEADS):
        q = y[:, h * ML_QK_DIM:(h + 1) * ML_QK_DIM].astype(BF16)
        kf = y[:, ML_HEADS * ML_QK_DIM + h * ML_QK_DIM:ML_HEADS * ML_QK_DIM + (h + 1) * ML_QK_DIM] * (ML_QK_DIM ** -0.5)
        k = kf.astype(BF16)
        v = v_ref[:, h * ML_V_DIM:(h + 1) * ML_V_DIM]
        bc = bc_all[:, ML_HEADS + h:ML_HEADS + h + 1]
        br = br_all[ML_HEADS + h:ML_HEADS + h + 1, :]
        ig_c = g_c[:, h:h + 1]
        ig_r = g_r[h:h + 1, :]
        m_prev = m_st[h][:, 0:1]
        c_prev = c_st[h]
        n_prev = n_st[h]

        dmat = jnp.where(causal, bc - br + ig_r, -jnp.inf)
        inter = bc + m_prev
        m_t = jnp.maximum(jnp.max(dmat, axis=-1, keepdims=True), inter)
        w = jnp.exp(dmat - m_t) * _dot_nt(q, k)
        decay = jnp.exp(inter - m_t)
        num = (jnp.dot(w.astype(BF16), v, preferred_element_type=F32)
               + decay * jnp.dot(q, c_prev.astype(BF16), preferred_element_type=F32))
        qn = jnp.sum(q.astype(F32) * n_prev.astype(BF16).astype(F32), axis=-1, keepdims=True)
        den = jnp.sum(w, axis=-1, keepdims=True) + decay * qn
        hh = num / jnp.maximum(jnp.abs(den), jnp.exp(-m_t))

        b_last = br[:, L - 1:L]
        g_row = b_last - br + ig_r
        m_new = jnp.maximum(b_last + m_prev, jnp.max(g_row, axis=-1, keepdims=True))
        wk_c = jnp.exp(b_last - bc + ig_c - m_new)
        cd = jnp.exp(b_last + m_prev - m_new)
        kw = (kf * wk_c)
        c_st[h] = cd * c_prev + lax.dot_general(kw.astype(BF16), v, (((0,), (0,)), ((), ())),
                                                preferred_element_type=F32)
        n_st[h] = cd * n_prev + jnp.sum(kw, axis=0, keepdims=True)
        m_st[h] = jnp.broadcast_to(m_new, (1, LANE))

        ms = jnp.mean(hh * hh, axis=-1, keepdims=True)
        hn = hh * lax.rsqrt(ms + NORM_EPS) * ng_ref[...]
        sl = slice(h * ML_V_DIM, (h + 1) * ML_V_DIM)
        out = hn * _sigmoid(o_ref[:, sl].astype(F32)) * _silu(z_ref[:, sl].astype(F32))
        y_ref[:, sl] = out.astype(y_ref.dtype)


def _mlstm(p_main, gif, gif_t, conv_w, conv_b, gate_b, norm_g, b, s, L):
    m = b * s
    nc = s // L
    w1k = BRANCH_WIDTH
    c_qk = MAIN_OFF[SEG_ML_QK] // w1k
    c_v = MAIN_OFF[SEG_ML_V] // w1k
    c_o = MAIN_OFF[SEG_ML_O] // w1k
    c_z = MAIN_OFF[SEG_ML_Z] // w1k
    gb = gate_b.reshape(1, 2 * ML_HEADS)
    kern = functools.partial(_ml_kernel, L=L)
    full = lambda shp: pl.BlockSpec(shp, lambda bi, ci: (0,) * len(shp))
    return pl.pallas_call(
        kern,
        grid=(b, nc),
        in_specs=[
            pl.BlockSpec((L, w1k), lambda bi, ci: (bi * nc + ci, c_qk)),
            pl.BlockSpec((L, w1k), lambda bi, ci: (bi * nc + ci, c_v)),
            pl.BlockSpec((L, w1k), lambda bi, ci: (bi * nc + ci, c_o)),
            pl.BlockSpec((L, w1k), lambda bi, ci: (bi * nc + ci, c_z)),
            pl.BlockSpec((L, 2 * ML_HEADS), lambda bi, ci: (bi * nc + ci, 0)),
            pl.BlockSpec((1, 2 * ML_HEADS, L), lambda bi, ci: (bi, 0, ci)),
            full((ML_CONV, ML_QK_WIDTH)),
            full((1, ML_QK_WIDTH)),
            full((1, 2 * ML_HEADS)),
            full((2 * ML_HEADS, 1)),
            full((1, ML_V_DIM)),
        ],
        out_specs=pl.BlockSpec((L, w1k), lambda bi, ci: (bi * nc + ci, 0)),
        out_shape=jax.ShapeDtypeStruct((m, ML_HEADS * ML_V_DIM), BF16),
        scratch_shapes=[pltpu.VMEM((L + SUBLANE, ML_QK_WIDTH), F32),
                        pltpu.VMEM((ML_HEADS, ML_QK_DIM, ML_V_DIM), F32),
                        pltpu.VMEM((ML_HEADS, 1, ML_QK_DIM), F32),
                        pltpu.VMEM((ML_HEADS, 1, LANE), F32)],
        compiler_params=_cparams(("parallel", "arbitrary")),
        name="mlstm",
    )(p_main, p_main, p_main, p_main, gif, gif_t, conv_w, conv_b.reshape(1, ML_QK_WIDTH),
      gb, gb.reshape(2 * ML_HEADS, 1), norm_g.reshape(1, ML_V_DIM))


def _dsa_norm_kernel(p_ref, gkv_ref, gik_ref, c_ref, ik_ref):
    o_kv = SMALL_OFF[SEG_DSA_KV]
    o_ik = SMALL_OFF[SEG_IDX_K]
    ckv = p_ref[:, o_kv:o_kv + DSA_LATENT]
    ms = jnp.mean(ckv * ckv, axis=-1, keepdims=True)
    c_ref[...] = (ckv * lax.rsqrt(ms + NORM_EPS) * gkv_ref[...]).astype(c_ref.dtype)
    ik = p_ref[:, o_ik:o_ik + IDX_DIM]
    ms = jnp.mean(ik * ik, axis=-1, keepdims=True)
    ik_ref[...] = (ik * lax.rsqrt(ms + NORM_EPS) * gik_ref[...]).astype(ik_ref.dtype)


def _dsa_norms(p_small, g_kv, g_ik, tm):
    m = p_small.shape[0]
    return pl.pallas_call(
        _dsa_norm_kernel,
        grid=(m // tm,),
        in_specs=[pl.BlockSpec((tm, SMALL_WIDTH), lambda i: (i, 0)),
                  pl.BlockSpec((1, DSA_LATENT), lambda i: (0, 0)),
                  pl.BlockSpec((1, IDX_DIM), lambda i: (0, 0))],
        out_specs=[pl.BlockSpec((tm, DSA_LATENT), lambda i: (i, 0)),
                   pl.BlockSpec((tm, IDX_DIM), lambda i: (i, 0))],
        out_shape=[jax.ShapeDtypeStruct((m, DSA_LATENT), BF16),
                   jax.ShapeDtypeStruct((m, IDX_DIM), BF16)],
        compiler_params=_cparams(("parallel",)),
        name="dsa_norms",
    )(p_small, g_kv.reshape(1, DSA_LATENT), g_ik.reshape(1, IDX_DIM))


def _dsa_kernel(q_ref, iq_ref, iw_ref, z_ref, c_ref, ik_ref, wuv_ref, o_ref,
                key_ref, m_ref, l_ref, acc_ref, cut_ref, cn_ref, wb_ref, *, tq, tk, topk, nbits_col, s_len):
    qi = pl.program_id(1)
    nkc = (qi * tq + tq + tk - 1) // tk
    row = qi * tq + lax.broadcasted_iota(I32, (tq, tk), 0)
    col0 = lax.broadcasted_iota(I32, (tq, tk), 1)
    nlb = tk // LANE

    iw = iw_ref[...] * ((IDX_HEADS ** -0.5) * (IDX_DIM ** -0.5))
    for h in range(IDX_HEADS):
        wb_ref[h] = jnp.broadcast_to(iw[:, h:h + 1], (tq, LANE))
    iq_blk = iq_ref[...]
    iq = jnp.concatenate([iq_blk[:, h * IDX_DIM:(h + 1) * IDX_DIM] for h in range(IDX_HEADS)], axis=0)
    sub = 2 * LANE
    nsub = tk // sub

    def score_body(kc, carry):
        top1, top2 = carry
        off = pl.multiple_of(kc * tk, tk)
        parts = []
        for j in range(nsub):
            ikc = ik_ref[pl.ds(pl.multiple_of(off + j * sub, sub), sub), :]
            sc = _dot_nt(iq, ikc)
            part = None
            for h in range(IDX_HEADS):
                term = jnp.maximum(sc[h * tq:(h + 1) * tq, :], 0.0) * jnp.tile(wb_ref[h], (1, sub // LANE))
                part = term if part is None else part + term
            parts.append(part)
        acc = jnp.concatenate(parts, axis=1)
        bits = lax.bitcast_convert_type(acc, I32)
        skey = jnp.where(bits < 0, (bits ^ 0x7FFFFFFF) + 1, bits)
        skey = jnp.where(col0 + off <= row, skey, INT_MIN)
        key_ref[:, pl.ds(off, tk)] = skey
        for j in range(nlb):
            blk = skey[:, j * LANE:(j + 1) * LANE]
            top2 = jnp.maximum(top2, jnp.minimum(top1, blk))
            top1 = jnp.maximum(top1, blk)
        return top1, top2

    def score_group(kp, carry):
        for u in range(DSA_UNROLL):
            carry = score_body(kp * DSA_UNROLL + u, carry)
        return carry

    lowest = jnp.full((tq, LANE), INT_MIN, I32)
    tops = lax.fori_loop(0, nkc // DSA_UNROLL, score_group, (lowest, lowest))
    top1, top2 = lax.fori_loop((nkc // DSA_UNROLL) * DSA_UNROLL, nkc, score_body, tops)

    n_blocks = tq // COUNT_ROWS

    def count_ge(t_rep, wanted):
        parts = []
        for blk in range(n_blocks):
            r0 = blk * COUNT_ROWS
            t_b = jnp.tile(t_rep[r0:r0 + COUNT_ROWS], (1, nlb))
            n_sweep = jnp.where(wanted[blk] > 0, nkc, 0)

            def chunk(kc, cnt, r0=r0, t_b=t_b):
                off = pl.multiple_of(kc * tk, tk)
                hit = key_ref[r0:r0 + COUNT_ROWS, pl.ds(off, tk)] >= t_b
                return cnt + _lane_fold(jnp.where(hit, 1, 0))

            def group(kp, cnt, chunk=chunk):
                for u in range(COUNT_UNROLL):
                    cnt = chunk(kp * COUNT_UNROLL + u, cnt)
                return cnt

            cnt = lax.fori_loop(0, n_sweep // COUNT_UNROLL, group, jnp.zeros((COUNT_ROWS, LANE), I32))
            parts.append(lax.fori_loop((n_sweep // COUNT_UNROLL) * COUNT_UNROLL, n_sweep, chunk, cnt))
        return jnp.sum(jnp.concatenate(parts, axis=0), axis=-1, keepdims=True)

    def count(pred):
        def chunk(kc, cnt):
            off = pl.multiple_of(kc * tk, tk)
            hit = pred(key_ref[:, pl.ds(off, tk)], col0 + off)
            return cnt + _lane_fold(jnp.where(hit, 1, 0))
        cnt = lax.fori_loop(0, nkc, chunk, jnp.zeros((tq, LANE), I32))
        return jnp.sum(cnt, axis=-1, keepdims=True)

    rep = lambda v: jnp.broadcast_to(v, (tq, LANE))
    n_valid = row[:, 0:LANE] + 1
    lo0 = rep(jnp.maximum(jnp.min(top2, axis=-1, keepdims=True), INT_MIN + 1))
    hi0 = jnp.maximum(rep(jnp.max(top2 if topk > LANE + 1 else top1, axis=-1, keepdims=True)), lo0)
    searching0 = n_valid > topk

    def any_row(flag):
        return jnp.max(jnp.where(flag, 1, 0))

    def block_flags(active):
        return tuple(any_row(active[blk * COUNT_ROWS:(blk + 1) * COUNT_ROWS]) for blk in range(n_blocks))

    def bis_cond(st):
        flags = st[0]
        busy = flags[0]
        for f in flags[1:]:
            busy = jnp.maximum(busy, f)
        return busy > 0

    def bis_body(st):
        flags, lo, hi, cnt_lo = st
        for _ in range(2):
            active = searching0 & (lo < hi) & (cnt_lo != topk)
            mid = (lo >> 1) + (hi >> 1) + ((lo | hi) & 1)
            cnt = rep(count_ge(mid, flags))
            up = active & (cnt >= topk)
            dn = active & (cnt < topk)
            lo = jnp.where(up, mid, lo)
            cnt_lo = jnp.where(up, cnt, cnt_lo)
            hi = jnp.where(dn, mid - 1, hi)
        return block_flags(searching0 & (lo < hi) & (cnt_lo != topk)), lo, hi, cnt_lo

    st0 = (block_flags(searching0 & (lo0 < hi0)), lo0, hi0, jnp.full((tq, LANE), topk + 1, I32))
    _, thr, _, cnt_thr = lax.while_loop(bis_cond, bis_body, st0)
    has_ties = any_row(searching0 & (cnt_thr != topk)) > 0
    searching0 = searching0[:, 0:1]
    thr = jnp.where(searching0, thr[:, 0:1], INT_MIN + 1)

    q = q_ref[...]
    qs = jnp.concatenate([q[:, h * DSA_LATENT:(h + 1) * DSA_LATENT] for h in range(DSA_HEADS)], axis=0)
    n_groups = 4
    grows = DSA_HEADS * tq // n_groups

    cut_ref[...] = jnp.full(cut_ref.shape, s_len, I32)

    @pl.when(has_ties)
    def _():
        need = topk - count(lambda k, c: k > thr)

        def cut_body(it, cut):
            cand = cut | jnp.left_shift(jnp.int32(1), nbits_col - 1 - it)
            cnt = count(lambda k, c: (k == thr) & (c < cand))
            return jnp.where(cnt < need, cand, cut)

        cut = lax.fori_loop(0, nbits_col, cut_body, jnp.zeros((tq, 1), I32))
        cut_ref[...] = jnp.broadcast_to(cut + 1, cut_ref.shape)

    tie_cut = cut_ref[:, 0:1]

    @pl.when(qi == 0)
    def _():
        def body(i, mx):
            cf = c_ref[pl.ds(pl.multiple_of(i * tk, tk), tk), :].astype(F32)
            return jnp.maximum(mx, jnp.max(jnp.sum(cf * cf, axis=-1, keepdims=True), axis=0, keepdims=True))
        mx = lax.fori_loop(0, s_len // tk, body, jnp.zeros((1, 1), F32))
        cn_ref[...] = jnp.broadcast_to(jnp.sqrt(mx), cn_ref.shape)

    def selection_bias(kc):
        off = pl.multiple_of(kc * tk, tk)
        keys = key_ref[:, pl.ds(off, tk)]
        sel = (keys > thr) | ((keys == thr) & (col0 + off < tie_cut))
        return off, jnp.where(sel, 0.0, NEG_BIG)

    def group_scores(g, cc, bias):
        s = _dot_nt(qs[g * grows:(g + 1) * grows, :], cc)
        return (s.reshape(grows // tq, tq, tk) + bias[None]).reshape(grows, tk)

    def finalize(l_col):
        z = z_ref[...].astype(F32)
        for h in range(DSA_HEADS):
            rs = slice(h * tq, (h + 1) * tq)
            ob = (acc_ref[rs, :] / l_col[rs, :]).astype(BF16)
            oh = jnp.dot(ob, wuv_ref[h], preferred_element_type=F32)
            sl = slice(h * DSA_V_DIM, (h + 1) * DSA_V_DIM)
            o_ref[:, sl] = (oh * _silu(z[:, sl])).astype(o_ref.dtype)

    qf = qs.astype(F32)
    m_ref[...] = jnp.broadcast_to(jnp.sqrt(jnp.sum(qf * qf, axis=-1, keepdims=True))
                                  * cn_ref[:, 0:1] * BOUND_SLACK, m_ref.shape)
    l_ref[...] = jnp.zeros(l_ref.shape, F32)
    acc_ref[...] = jnp.zeros(acc_ref.shape, F32)

    def bounded_chunk(kc):
        off, bias = selection_bias(kc)
        cc = c_ref[pl.ds(off, tk), :]
        for g in range(n_groups):
            rs = slice(g * grows, (g + 1) * grows)
            p = jnp.exp2(group_scores(g, cc, bias) - jnp.tile(m_ref[rs, :], (1, nlb)))
            l_ref[rs, :] = l_ref[rs, :] + _lane_fold(p)
            acc_ref[rs, :] = acc_ref[rs, :] + jnp.dot(p.astype(BF16), cc, preferred_element_type=F32)

    def bounded_group(kp, carry):
        for u in range(DSA_UNROLL):
            bounded_chunk(kp * DSA_UNROLL + u)
        return carry

    def bounded_tail(kc, carry):
        bounded_chunk(kc)
        return carry

    lax.fori_loop(0, nkc // DSA_UNROLL, bounded_group, 0)
    lax.fori_loop((nkc // DSA_UNROLL) * DSA_UNROLL, nkc, bounded_tail, 0)
    l_tot = jnp.sum(l_ref[...], axis=-1, keepdims=True)
    underflow = jnp.logical_not(jnp.min(l_tot) >= UNDERFLOW_GUARD)

    @pl.when(jnp.logical_not(underflow))
    def _():
        finalize(l_tot)

    @pl.when(underflow)
    def _():
        m_ref[...] = jnp.full(m_ref.shape, NEG_BIG, F32)
        l_ref[...] = jnp.zeros(l_ref.shape, F32)
        acc_ref[...] = jnp.zeros(acc_ref.shape, F32)

        def body(kc, carry):
            off, bias = selection_bias(kc)
            cc = c_ref[pl.ds(off, tk), :]
            for g in range(n_groups):
                rs = slice(g * grows, (g + 1) * grows)
                s = group_scores(g, cc, bias)
                m_prev = m_ref[rs, :]
                m_new = jnp.maximum(m_prev, jnp.max(s, axis=-1, keepdims=True))
                alpha = jnp.exp2(m_prev - m_new)
                p = jnp.exp2(s - jnp.tile(m_new, (1, nlb)))
                l_ref[rs, :] = alpha * l_ref[rs, :] + jnp.sum(p, axis=-1, keepdims=True)
                acc_ref[rs, :] = (jnp.tile(alpha, (1, DSA_LATENT // LANE)) * acc_ref[rs, :]
                                  + jnp.dot(p.astype(BF16), cc, preferred_element_type=F32))
                m_ref[rs, :] = m_new
            return carry

        lax.fori_loop(0, nkc, body, 0)
        finalize(l_ref[:, 0:1])


def _dsa_attention(p_main, iw, c_lat, ik_n, w_uv, b, s, tq, tk):
    m = b * s
    nq = s // tq
    topk = min(IDX_TOPK_MAX, s // 4)
    assert topk <= 2 * LANE, "the per-lane top-two start of the threshold search covers 2 * LANE keys"
    c_q = MAIN_OFF[SEG_DSA_Q] // (DSA_HEADS * DSA_LATENT)
    c_z = MAIN_OFF[SEG_DSA_Z] // BRANCH_WIDTH
    c_iq = MAIN_OFF[SEG_IDX_Q] // (IDX_HEADS * IDX_DIM)
    nbits_col = max(1, int(math.ceil(math.log2(s))) + 1)
    kern = functools.partial(_dsa_kernel, tq=tq, tk=tk, topk=topk, nbits_col=nbits_col, s_len=s)
    return pl.pallas_call(
        kern,
        grid=(b, nq),
        in_specs=[
            pl.BlockSpec((tq, DSA_HEADS * DSA_LATENT), lambda bi, qi: (bi * nq + qi, c_q)),
            pl.BlockSpec((tq, IDX_HEADS * IDX_DIM), lambda bi, qi: (bi * nq + qi, c_iq)),
            pl.BlockSpec((tq, IDX_HEADS), lambda bi, qi: (bi * nq + qi, 0)),
            pl.BlockSpec((tq, BRANCH_WIDTH), lambda bi, qi: (bi * nq + qi, c_z)),
            pl.BlockSpec((s, DSA_LATENT), lambda bi, qi: (bi, 0), pipeline_mode=pl.Buffered(1)),
            pl.BlockSpec((s, IDX_DIM), lambda bi, qi: (bi, 0), pipeline_mode=pl.Buffered(1)),
            pl.BlockSpec((DSA_HEADS, DSA_LATENT, DSA_V_DIM), lambda bi, qi: (0, 0, 0)),
        ],
        out_specs=pl.BlockSpec((tq, DSA_HEADS * DSA_V_DIM), lambda bi, qi: (bi * nq + qi, 0)),
        out_shape=jax.ShapeDtypeStruct((m, DSA_HEADS * DSA_V_DIM), BF16),
        scratch_shapes=[pltpu.VMEM((tq, s), I32),
                        pltpu.VMEM((DSA_HEADS * tq, LANE), F32),
                        pltpu.VMEM((DSA_HEADS * tq, LANE), F32),
                        pltpu.VMEM((DSA_HEADS * tq, DSA_LATENT), F32),
                        pltpu.VMEM((tq, LANE), I32),
                        pltpu.VMEM((1, LANE), F32),
                        pltpu.VMEM((IDX_HEADS, tq, LANE), F32)],
        compiler_params=_cparams(("parallel", "arbitrary")),
        name="dsa_attention",
    )(p_main, p_main, iw, p_main, c_lat, ik_n, w_uv)


def _out_kernel(x_ref, ya_ref, yb_ref, yc_ref, ga_ref, gb_ref, gc_ref, wb_ref, wo_ref, pg_ref, o_ref):
    mixed = None
    for i, (y_ref, g_ref) in enumerate(((ya_ref, ga_ref), (yb_ref, gb_ref), (yc_ref, gc_ref))):
        t = _sigmoid(g_ref[...].astype(F32)) * jnp.dot(y_ref[...], wb_ref[i], preferred_element_type=F32)
        mixed = t if mixed is None else mixed + t
    out = jnp.dot(mixed.astype(BF16), wo_ref[...], preferred_element_type=F32)
    ms = jnp.mean(out * out, axis=-1, keepdims=True)
    o_ref[...] = x_ref[...] + out * lax.rsqrt(ms + NORM_EPS) * pg_ref[...]


def _merge_out(x2, ya, yb, yc, p_main, w_branch, w_out, post_g, tm):
    m, d = x2.shape
    cg = MAIN_OFF[SEG_GATES] // d
    row = lambda i: (i, 0)
    return pl.pallas_call(
        _out_kernel,
        grid=(m // tm,),
        in_specs=[
            pl.BlockSpec((tm, d), row),
            pl.BlockSpec((tm, BRANCH_WIDTH), row),
            pl.BlockSpec((tm, BRANCH_WIDTH), row),
            pl.BlockSpec((tm, BRANCH_WIDTH), row),
            pl.BlockSpec((tm, d), lambda i: (i, cg)),
            pl.BlockSpec((tm, d), lambda i: (i, cg + 1)),
            pl.BlockSpec((tm, d), lambda i: (i, cg + 2)),
            pl.BlockSpec((N_BRANCHES, BRANCH_WIDTH, d), lambda i: (0, 0, 0), pipeline_mode=pl.Buffered(1)),
            pl.BlockSpec((d, d), lambda i: (0, 0), pipeline_mode=pl.Buffered(1)),
            pl.BlockSpec((1, d), lambda i: (0, 0)),
        ],
        out_specs=pl.BlockSpec((tm, d), row),
        out_shape=jax.ShapeDtypeStruct((m, d), F32),
        compiler_params=_cparams(("parallel",)),
        name="merge_out_proj",
    )(x2, ya, yb, yc, p_main, p_main, p_main, w_branch, w_out, post_g.reshape(1, d))


def _pick(total, want):
    t = min(total, want)
    while total % t:
        t //= 2
    return t


def kernel(x, pre_norm_g, w_in, da_lambda, da_norm_g, ml_conv_w, ml_conv_b, ml_gate_b, ml_norm_g,
           dsa_kv_norm_g, dsa_ik_norm_g, dsa_w_uv, w_branch, w_out, post_norm_g):
    b, s, d = x.shape
    m = b * s
    bounds = _seg_bounds()
    x2 = x.reshape(m, d)
    tm_proj = _pick(m, 1024)
    t_da = _pick(s, 512)
    l_ml = _pick(s, 256)
    tq_dsa = _pick(s, 256)
    tk_dsa = _pick(s, 512)
    o_if = SMALL_OFF[SEG_ML_IF]
    o_iw = SMALL_OFF[SEG_IDX_W]
    for l in range(DEPTH):
        w = w_in[l]
        col_scale = {SEG_DA_Q: DA_HEAD_DIM ** -0.5 * LOG2E, SEG_DSA_Q: DSA_LATENT ** -0.5 * LOG2E}
        w_main = jnp.concatenate([w[:, bounds[sg]:bounds[sg + 1]] * col_scale.get(sg, 1.0) for sg in MAIN_SEGS],
                                 axis=1).astype(BF16)
        w_small = jnp.concatenate([w[:, bounds[sg]:bounds[sg + 1]] for sg in SMALL_SEGS]
                                  + [jnp.zeros((d, SMALL_WIDTH - SMALL_USED), w.dtype)], axis=1).astype(BF16)
        p_main, p_small = _project(x2, pre_norm_g[l], w_main, w_small, tm_proj, _pick(MAIN_WIDTH, 2048))

        lam_init = 0.8 - 0.6 * math.exp(-0.3 * l)
        y_a = _diff_attention(p_main, da_lambda[l], da_norm_g[l], lam_init, b, s, t_da)

        gif = p_small[:, o_if:o_if + 2 * ML_HEADS]
        gif_t = jnp.transpose(gif.reshape(b, s, 2 * ML_HEADS), (0, 2, 1))
        y_b = _mlstm(p_main, gif, gif_t, ml_conv_w[l], ml_conv_b[l], ml_gate_b[l], ml_norm_g[l], b, s, l_ml)

        c_lat, ik_n = _dsa_norms(p_small, dsa_kv_norm_g[l], dsa_ik_norm_g[l], _pick(m, 2048))
        iw = p_small[:, o_iw:o_iw + IDX_HEADS]
        y_c = _dsa_attention(p_main, iw, c_lat, ik_n, dsa_w_uv[l].astype(BF16), b, s, tq_dsa, tk_dsa)

        x2 = _merge_out(x2, y_a, y_b, y_c, p_main, w_branch[l].astype(BF16), w_out[l].astype(BF16),
                        post_norm_g[l], _pick(m, 256))
    return x2.reshape(b, s, d)
```

```python
import functools
import math

import jax
import jax.numpy as jnp
from jax import lax
from jax.experimental import pallas as pl
from jax.experimental.pallas import tpu as pltpu

F32 = jnp.float32
BF16 = jnp.bfloat16
I32 = jnp.int32

D_MODEL = 2048
DEPTH = 2
BRANCH_WIDTH = 1024
N_BRANCHES = 3
NORM_EPS = 1e-6
DA_HEADS = 8
DA_HEAD_DIM = 64
DA_V_DIM = 128
ML_HEADS = 4
ML_QK_DIM = 128
ML_V_DIM = 256
ML_CONV = 4
ML_QK_WIDTH = 2 * ML_HEADS * ML_QK_DIM
DSA_HEADS = 8
DSA_LATENT = 256
DSA_V_DIM = 128
IDX_HEADS = 16
IDX_DIM = 64
IDX_TOPK_MAX = 256

IN_SPLITS = (
    DA_HEADS * 2 * DA_HEAD_DIM, DA_HEADS * 2 * DA_HEAD_DIM, DA_HEADS * DA_V_DIM, BRANCH_WIDTH,
    ML_QK_WIDTH, ML_HEADS * ML_V_DIM, 2 * ML_HEADS,
    BRANCH_WIDTH, BRANCH_WIDTH,
    DSA_HEADS * DSA_LATENT, DSA_LATENT,
    IDX_HEADS * IDX_DIM, IDX_DIM, IDX_HEADS,
    BRANCH_WIDTH,
    N_BRANCHES * D_MODEL,
)
(SEG_DA_Q, SEG_DA_K, SEG_DA_V, SEG_DA_Z, SEG_ML_QK, SEG_ML_V, SEG_ML_IF, SEG_ML_O, SEG_ML_Z,
 SEG_DSA_Q, SEG_DSA_KV, SEG_IDX_Q, SEG_IDX_K, SEG_IDX_W, SEG_DSA_Z, SEG_GATES) = range(16)

MAIN_SEGS = (SEG_DA_Q, SEG_DA_K, SEG_DA_V, SEG_DA_Z, SEG_ML_QK, SEG_ML_V, SEG_ML_O, SEG_ML_Z,
             SEG_DSA_Q, SEG_IDX_Q, SEG_DSA_Z, SEG_GATES)
SMALL_SEGS = (SEG_DSA_KV, SEG_IDX_K, SEG_IDX_W, SEG_ML_IF)
SMALL_WIDTH = 384

LANE = 128
SUBLANE = 8
V7X_VMEM_BYTES = 64 * 1024 * 1024
VMEM_LIMIT = V7X_VMEM_BYTES * 7 // 8
INT_MIN = -2147483648
NEG_BIG = -1e30
LOG2E = math.log2(math.e)
DA_HEAD_GROUP = 2
DA_UNROLL = 2
DSA_UNROLL = 4
COUNT_ROWS = 64
COUNT_UNROLL = 4
BOUND_SLACK = 1.01
UNDERFLOW_GUARD = 2.0 ** -100


def _seg_bounds():
    offs = [0]
    for n in IN_SPLITS:
        offs.append(offs[-1] + n)
    return offs


def _main_offsets():
    out, off = {}, 0
    for s in MAIN_SEGS:
        out[s] = off
        off += IN_SPLITS[s]
    return out, off


def _small_offsets():
    out, off = {}, 0
    for s in SMALL_SEGS:
        out[s] = off
        off += IN_SPLITS[s]
    return out, off


MAIN_OFF, MAIN_WIDTH = _main_offsets()
SMALL_OFF, SMALL_USED = _small_offsets()


def _cparams(sem):
    return pltpu.CompilerParams(dimension_semantics=sem, vmem_limit_bytes=VMEM_LIMIT)


def _silu(v):
    return v * (1.0 / (1.0 + jnp.exp(-v)))


def _sigmoid(v):
    return 1.0 / (1.0 + jnp.exp(-v))


def _dot_nt(a, b):
    return lax.dot_general(a, b, (((1,), (1,)), ((), ())), preferred_element_type=F32)


def _proj_kernel(x_ref, g_ref, w_ref, ws_ref, o_ref, os_ref, h_ref):
    @pl.when(pl.program_id(1) == 0)
    def _():
        x = x_ref[...]
        ms = jnp.mean(x * x, axis=-1, keepdims=True)
        h_ref[...] = (x * lax.rsqrt(ms + NORM_EPS) * g_ref[...]).astype(BF16)
        os_ref[...] = jnp.dot(h_ref[...], ws_ref[...], preferred_element_type=F32)

    o_ref[...] = jnp.dot(h_ref[...], w_ref[...], preferred_element_type=F32).astype(o_ref.dtype)


def _project(x2, g, w, w_small, tm, tn):
    m, d = x2.shape
    n = w.shape[1]
    ns = w_small.shape[1]
    return pl.pallas_call(
        _proj_kernel,
        grid=(m // tm, n // tn),
        in_specs=[pl.BlockSpec((tm, d), lambda i, j: (i, 0)),
                  pl.BlockSpec((1, d), lambda i, j: (0, 0)),
                  pl.BlockSpec((d, tn), lambda i, j: (0, j)),
                  pl.BlockSpec((d, ns), lambda i, j: (0, 0))],
        out_specs=[pl.BlockSpec((tm, tn), lambda i, j: (i, j)),
                   pl.BlockSpec((tm, ns), lambda i, j: (i, 0))],
        out_shape=[jax.ShapeDtypeStruct((m, n), BF16), jax.ShapeDtypeStruct((m, ns), F32)],
        scratch_shapes=[pltpu.VMEM((tm, d), BF16)],
        compiler_params=_cparams(("parallel", "arbitrary")),
        name="rmsnorm_in_proj",
    )(x2, g.reshape(1, d), w, w_small)


def _lane_fold(x):
    part = x[:, 0:LANE]
    for j in range(1, x.shape[1] // LANE):
        part = part + x[:, j * LANE:(j + 1) * LANE]
    return part


def _da_kernel(lam_ref, q_ref, k_ref, v_ref, z_ref, g_ref, o_ref, m_ref, l_ref, acc_ref, kn_ref,
               *, t, s_len, lam_init):
    qi = pl.program_id(2)
    nrep = t // LANE
    nbr = 2 * DA_HEAD_GROUP
    br = [slice(j * DA_HEAD_DIM, (j + 1) * DA_HEAD_DIM) for j in range(nbr)]
    vcols = [slice((j // 2) * DA_V_DIM, (j // 2 + 1) * DA_V_DIM) for j in range(nbr)]

    @pl.when(qi == 0)
    def _():
        def body(i, mx):
            kk = k_ref[pl.ds(pl.multiple_of(i * t, t), t), :].astype(F32)
            sq = kk * kk
            return tuple(jnp.maximum(mx[j], jnp.max(jnp.sum(sq[:, br[j]], axis=-1, keepdims=True),
                                                    axis=0, keepdims=True)) for j in range(nbr))
        mx = lax.fori_loop(0, s_len // t, body, tuple(jnp.zeros((1, 1), F32) for _ in range(nbr)))
        for j in range(nbr):
            kn_ref[j] = jnp.broadcast_to(jnp.sqrt(mx[j]), (1, LANE))

    q = q_ref[...]
    qsq = q.astype(F32)
    qsq = qsq * qsq
    bound = [jnp.broadcast_to(jnp.sqrt(jnp.sum(qsq[:, br[j]], axis=-1, keepdims=True))
                              * kn_ref[j][:, 0:1] * BOUND_SLACK, (t, LANE)) for j in range(nbr)]

    def causal_keep():
        row = lax.broadcasted_iota(I32, (t, t), 0)
        col = lax.broadcasted_iota(I32, (t, t), 1)
        return col <= row

    def sweep(block):
        def body(kp, carry):
            for u in range(DA_UNROLL):
                block(pl.multiple_of(kp * (DA_UNROLL * t) + u * t, t), False)
            return carry
        lax.fori_loop(0, qi // DA_UNROLL, body, 0)

        def tail(ki, carry):
            block(pl.multiple_of(ki * t, t), False)
            return carry
        lax.fori_loop((qi // DA_UNROLL) * DA_UNROLL, qi, tail, 0)

        block(pl.multiple_of(qi * t, t), True)

    def bounded_block(off, masked):
        k = k_ref[pl.ds(off, t), :]
        v = v_ref[pl.ds(off, t), :]
        for j in range(nbr):
            s = _dot_nt(q[:, br[j]], k[:, br[j]])
            if masked:
                s = jnp.where(causal_keep(), s, -jnp.inf)
            p = jnp.exp2(s - jnp.tile(bound[j], (1, nrep)))
            l_ref[j] = l_ref[j] + _lane_fold(p)
            acc_ref[j] = acc_ref[j] + jnp.dot(p.astype(BF16), v[:, vcols[j]], preferred_element_type=F32)

    def running_max_block(off, masked):
        k = k_ref[pl.ds(off, t), :]
        v = v_ref[pl.ds(off, t), :]
        for j in range(nbr):
            s = _dot_nt(q[:, br[j]], k[:, br[j]])
            if masked:
                s = jnp.where(causal_keep(), s, -jnp.inf)
            m_prev = m_ref[j]
            m_new = jnp.maximum(m_prev, jnp.max(s, axis=-1, keepdims=True))
            alpha = jnp.exp2(m_prev - m_new)
            p = jnp.exp2(s - jnp.tile(m_new, (1, nrep)))
            l_ref[j] = alpha * l_ref[j] + jnp.sum(p, axis=-1, keepdims=True)
            acc_ref[j] = alpha * acc_ref[j] + jnp.dot(p.astype(BF16), v[:, vcols[j]],
                                                      preferred_element_type=F32)
            m_ref[j] = m_new

    l_ref[...] = jnp.zeros(l_ref.shape, F32)
    acc_ref[...] = jnp.zeros(acc_ref.shape, F32)
    sweep(bounded_block)
    l_tot = [jnp.sum(l_ref[j], axis=-1, keepdims=True) for j in range(nbr)]
    l_min = l_tot[0]
    for j in range(1, nbr):
        l_min = jnp.minimum(l_min, l_tot[j])
    underflow = jnp.logical_not(jnp.min(l_min) >= UNDERFLOW_GUARD)

    @pl.when(jnp.logical_not(underflow))
    def _():
        for j in range(nbr):
            acc_ref[j] = acc_ref[j] / l_tot[j]

    @pl.when(underflow)
    def _():
        m_ref[...] = jnp.full(m_ref.shape, -jnp.inf, F32)
        l_ref[...] = jnp.zeros(l_ref.shape, F32)
        acc_ref[...] = jnp.zeros(acc_ref.shape, F32)
        sweep(running_max_block)
        for j in range(nbr):
            acc_ref[j] = acc_ref[j] / l_ref[j]

    lp = lam_ref[...]
    lam = (jnp.exp(jnp.sum(lp[0:1] * lp[1:2], axis=-1, keepdims=True))
           - jnp.exp(jnp.sum(lp[2:3] * lp[3:4], axis=-1, keepdims=True)) + lam_init)
    for hd in range(DA_HEAD_GROUP):
        cols = slice(hd * DA_V_DIM, (hd + 1) * DA_V_DIM)
        o = acc_ref[2 * hd] - lam * acc_ref[2 * hd + 1]
        ms = jnp.mean(o * o, axis=-1, keepdims=True)
        o = o * lax.rsqrt(ms + NORM_EPS) * g_ref[...] * (1.0 - lam_init)
        o_ref[:, cols] = (o * _silu(z_ref[:, cols].astype(F32))).astype(o_ref.dtype)


def _diff_attention(p_main, lam_params, norm_g, lam_init, b, s, t):
    m = b * s
    nq = s // t
    wg = DA_HEAD_GROUP * LANE
    cq = MAIN_OFF[SEG_DA_Q] // wg
    ck = MAIN_OFF[SEG_DA_K] // wg
    cv = MAIN_OFF[SEG_DA_V] // wg
    cz = MAIN_OFF[SEG_DA_Z] // wg
    nbr = 2 * DA_HEAD_GROUP
    kern = functools.partial(_da_kernel, t=t, s_len=s, lam_init=lam_init)
    return pl.pallas_call(
        kern,
        grid=(b, DA_HEADS // DA_HEAD_GROUP, nq),
        in_specs=[
            pl.BlockSpec((4, DA_HEAD_DIM), lambda bi, h, qi: (0, 0)),
            pl.BlockSpec((t, wg), lambda bi, h, qi: (bi * nq + qi, cq + h)),
            pl.BlockSpec((s, wg), lambda bi, h, qi: (bi, ck + h)),
            pl.BlockSpec((s, wg), lambda bi, h, qi: (bi, cv + h)),
            pl.BlockSpec((t, wg), lambda bi, h, qi: (bi * nq + qi, cz + h)),
            pl.BlockSpec((1, DA_V_DIM), lambda bi, h, qi: (0, 0)),
        ],
        out_specs=pl.BlockSpec((t, wg), lambda bi, h, qi: (bi * nq + qi, h)),
        out_shape=jax.ShapeDtypeStruct((m, DA_HEADS * DA_V_DIM), BF16),
        scratch_shapes=[pltpu.VMEM((nbr, t, LANE), F32), pltpu.VMEM((nbr, t, LANE), F32),
                        pltpu.VMEM((nbr, t, DA_V_DIM), F32), pltpu.VMEM((nbr, 1, LANE), F32)],
        compiler_params=_cparams(("parallel", "parallel", "arbitrary")),
        name="diff_attention",
    )(lam_params, p_main, p_main, p_main, p_main, norm_g.reshape(1, DA_V_DIM))


def _split3(v):
    hi = v.astype(BF16)
    r1 = v - hi.astype(F32)
    mid = r1.astype(BF16)
    lo = (r1 - mid.astype(F32)).astype(BF16)
    return hi, mid, lo


def _log_sigmoid(v):
    return jnp.minimum(v, 0.0) - jnp.log(1.0 + jnp.exp(-jnp.abs(v)))


def _ml_kernel(qk_ref, v_ref, o_ref, z_ref, gif_ref, gift_ref, cw_ref, cb_ref, gb_ref, gbt_ref, ng_ref,
               y_ref, xbuf, c_st, n_st, m_st, *, L):
    ci = pl.program_id(1)
    pad = SUBLANE

    @pl.when(ci == 0)
    def _():
        xbuf[0:pad, :] = jnp.zeros((pad, ML_QK_WIDTH), F32)
        c_st[...] = jnp.zeros(c_st.shape, F32)
        n_st[...] = jnp.zeros(n_st.shape, F32)
        m_st[...] = jnp.zeros(m_st.shape, F32)

    xbuf[pad:pad + L, :] = qk_ref[...].astype(F32)
    cw = cw_ref[...]
    y = xbuf[pad:pad + L, :] * cw[ML_CONV - 1:ML_CONV, :]
    for j in range(ML_CONV - 1):
        sh = ML_CONV - 1 - j
        y = y + xbuf[pad - sh:pad - sh + L, :] * cw[j:j + 1, :]
    y = _silu(y + cb_ref[...])
    xbuf[0:pad, :] = xbuf[L:L + pad, :]

    g_c = gif_ref[...] + gb_ref[...]
    g_r = gift_ref[0] + gbt_ref[...]
    lf_c = _log_sigmoid(g_c)
    lf_r = _log_sigmoid(g_r)
    ti = lax.broadcasted_iota(I32, (L, L), 0)
    si = lax.broadcasted_iota(I32, (L, L), 1)
    causal = si <= ti
    tri = jnp.where(causal, 1.0, 0.0).astype(BF16)
    triu = jnp.where(ti <= si, 1.0, 0.0).astype(BF16)
    bc_all = sum(jnp.dot(tri, piece, preferred_element_type=F32) for piece in _split3(lf_c))
    br_all = sum(jnp.dot(piece, triu, preferred_element_type=F32) for piece in _split3(lf_r))

    for h in range(ML_HEADS):
        q = y[:, h * ML_QK_DIM:(h + 1) * ML_QK_DIM].astype(BF16)
        kf = y[:, ML_HEADS * ML_QK_DIM + h * ML_QK_DIM:ML_HEADS * ML_QK_DIM + (h + 1) * ML_QK_DIM] * (ML_QK_DIM ** -0.5)
        k = kf.astype(BF16)
        v = v_ref[:, h * ML_V_DIM:(h + 1) * ML_V_DIM]
        bc = bc_all[:, ML_HEADS + h:ML_HEADS + h + 1]
        br = br_all[ML_HEADS + h:ML_HEADS + h + 1, :]
        ig_c = g_c[:, h:h + 1]
        ig_r = g_r[h:h + 1, :]
        m_prev = m_st[h][:, 0:1]
        c_prev = c_st[h]
        n_prev = n_st[h]

        dmat = jnp.where(causal, bc - br + ig_r, -jnp.inf)
        inter = bc + m_prev
        m_t = jnp.maximum(jnp.max(dmat, axis=-1, keepdims=True), inter)
        w = jnp.exp(dmat - m_t) * _dot_nt(q, k)
        decay = jnp.exp(inter - m_t)
        num = (jnp.dot(w.astype(BF16), v, preferred_element_type=F32)
               + decay * jnp.dot(q, c_prev.astype(BF16), preferred_element_type=F32))
        qn = jnp.sum(q.astype(F32) * n_prev.astype(BF16).astype(F32), axis=-1, keepdims=True)
        den = jnp.sum(w, axis=-1, keepdims=True) + decay * qn
        hh = num / jnp.maximum(jnp.abs(den), jnp.exp(-m_t))

        b_last = br[:, L - 1:L]
        g_row = b_last - br + ig_r
        m_new = jnp.maximum(b_last + m_prev, jnp.max(g_row, axis=-1, keepdims=True))
        wk_c = jnp.exp(b_last - bc + ig_c - m_new)
        cd = jnp.exp(b_last + m_prev - m_new)
        kw = (kf * wk_c)
        c_st[h] = cd * c_prev + lax.dot_general(kw.astype(BF16), v, (((0,), (0,)), ((), ())),
                                                preferred_element_type=F32)
        n_st[h] = cd * n_prev + jnp.sum(kw, axis=0, keepdims=True)
        m_st[h] = jnp.broadcast_to(m_new, (1, LANE))

        ms = jnp.mean(hh * hh, axis=-1, keepdims=True)
        hn = hh * lax.rsqrt(ms + NORM_EPS) * ng_ref[...]
        sl = slice(h * ML_V_DIM, (h + 1) * ML_V_DIM)
        out = hn * _sigmoid(o_ref[:, sl].astype(F32)) * _silu(z_ref[:, sl].astype(F32))
        y_ref[:, sl] = out.astype(y_ref.dtype)


def _mlstm(p_main, gif, gif_t, conv_w, conv_b, gate_b, norm_g, b, s, L):
    m = b * s
    nc = s // L
    w1k = BRANCH_WIDTH
    c_qk = MAIN_OFF[SEG_ML_QK] // w1k
    c_v = MAIN_OFF[SEG_ML_V] // w1k
    c_o = MAIN_OFF[SEG_ML_O] // w1k
    c_z = MAIN_OFF[SEG_ML_Z] // w1k
    gb = gate_b.reshape(1, 2 * ML_HEADS)
    kern = functools.partial(_ml_kernel, L=L)
    full = lambda shp: pl.BlockSpec(shp, lambda bi, ci: (0,) * len(shp))
    return pl.pallas_call(
        kern,
        grid=(b, nc),
        in_specs=[
            pl.BlockSpec((L, w1k), lambda bi, ci: (bi * nc + ci, c_qk)),
            pl.BlockSpec((L, w1k), lambda bi, ci: (bi * nc + ci, c_v)),
            pl.BlockSpec((L, w1k), lambda bi, ci: (bi * nc + ci, c_o)),
            pl.BlockSpec((L, w1k), lambda bi, ci: (bi * nc + ci, c_z)),
            pl.BlockSpec((L, 2 * ML_HEADS), lambda bi, ci: (bi * nc + ci, 0)),
            pl.BlockSpec((1, 2 * ML_HEADS, L), lambda bi, ci: (bi, 0, ci)),
            full((ML_CONV, ML_QK_WIDTH)),
            full((1, ML_QK_WIDTH)),
            full((1, 2 * ML_HEADS)),
            full((2 * ML_HEADS, 1)),
            full((1, ML_V_DIM)),
        ],
        out_specs=pl.BlockSpec((L, w1k), lambda bi, ci: (bi * nc + ci, 0)),
        out_shape=jax.ShapeDtypeStruct((m, ML_HEADS * ML_V_DIM), BF16),
        scratch_shapes=[pltpu.VMEM((L + SUBLANE, ML_QK_WIDTH), F32),
                        pltpu.VMEM((ML_HEADS, ML_QK_DIM, ML_V_DIM), F32),
                        pltpu.VMEM((ML_HEADS, 1, ML_QK_DIM), F32),
                        pltpu.VMEM((ML_HEADS, 1, LANE), F32)],
        compiler_params=_cparams(("parallel", "arbitrary")),
        name="mlstm",
    )(p_main, p_main, p_main, p_main, gif, gif_t, conv_w, conv_b.reshape(1, ML_QK_WIDTH),
      gb, gb.reshape(2 * ML_HEADS, 1), norm_g.reshape(1, ML_V_DIM))


def _dsa_norm_kernel(p_ref, gkv_ref, gik_ref, c_ref, ik_ref):
    o_kv = SMALL_OFF[SEG_DSA_KV]
    o_ik = SMALL_OFF[SEG_IDX_K]
    ckv = p_ref[:, o_kv:o_kv + DSA_LATENT]
    ms = jnp.mean(ckv * ckv, axis=-1, keepdims=True)
    c_ref[...] = (ckv * lax.rsqrt(ms + NORM_EPS) * gkv_ref[...]).astype(c_ref.dtype)
    ik = p_ref[:, o_ik:o_ik + IDX_DIM]
    ms = jnp.mean(ik * ik, axis=-1, keepdims=True)
    ik_ref[...] = (ik * lax.rsqrt(ms + NORM_EPS) * gik_ref[...]).astype(ik_ref.dtype)


def _dsa_norms(p_small, g_kv, g_ik, tm):
    m = p_small.shape[0]
    return pl.pallas_call(
        _dsa_norm_kernel,
        grid=(m // tm,),
        in_specs=[pl.BlockSpec((tm, SMALL_WIDTH), lambda i: (i, 0)),
                  pl.BlockSpec((1, DSA_LATENT), lambda i: (0, 0)),
                  pl.BlockSpec((1, IDX_DIM), lambda i: (0, 0))],
        out_specs=[pl.BlockSpec((tm, DSA_LATENT), lambda i: (i, 0)),
                   pl.BlockSpec((tm, IDX_DIM), lambda i: (i, 0))],
        out_shape=[jax.ShapeDtypeStruct((m, DSA_LATENT), BF16),
                   jax.ShapeDtypeStruct((m, IDX_DIM), BF16)],
        compiler_params=_cparams(("parallel",)),
        name="dsa_norms",
    )(p_small, g_kv.reshape(1, DSA_LATENT), g_ik.reshape(1, IDX_DIM))


def _dsa_kernel(q_ref, iq_ref, iw_ref, z_ref, c_ref, ik_ref, wuv_ref, o_ref,
                key_ref, m_ref, l_ref, acc_ref, cut_ref, cn_ref, wb_ref, *, tq, tk, topk, nbits_col, s_len):
    qi = pl.program_id(1)
    nkc = (qi * tq + tq + tk - 1) // tk
    row = qi * tq + lax.broadcasted_iota(I32, (tq, tk), 0)
    col0 = lax.broadcasted_iota(I32, (tq, tk), 1)
    nlb = tk // LANE

    iw = iw_ref[...] * ((IDX_HEADS ** -0.5) * (IDX_DIM ** -0.5))
    for h in range(IDX_HEADS):
        wb_ref[h] = jnp.broadcast_to(iw[:, h:h + 1], (tq, LANE))
    iq_blk = iq_ref[...]
    iq = jnp.concatenate([iq_blk[:, h * IDX_DIM:(h + 1) * IDX_DIM] for h in range(IDX_HEADS)], axis=0)
    sub = 2 * LANE
    nsub = tk // sub

    def score_body(kc, carry):
        top1, top2 = carry
        off = pl.multiple_of(kc * tk, tk)
        parts = []
        for j in range(nsub):
            ikc = ik_ref[pl.ds(pl.multiple_of(off + j * sub, sub), sub), :]
            sc = _dot_nt(iq, ikc)
            part = None
            for h in range(IDX_HEADS):
                term = jnp.maximum(sc[h * tq:(h + 1) * tq, :], 0.0) * jnp.tile(wb_ref[h], (1, sub // LANE))
                part = term if part is None else part + term
            parts.append(part)
        acc = jnp.concatenate(parts, axis=1)
        bits = lax.bitcast_convert_type(acc, I32)
        skey = jnp.where(bits < 0, (bits ^ 0x7FFFFFFF) + 1, bits)
        skey = jnp.where(col0 + off <= row, skey, INT_MIN)
        key_ref[:, pl.ds(off, tk)] = skey
        for j in range(nlb):
            blk = skey[:, j * LANE:(j + 1) * LANE]
            top2 = jnp.maximum(top2, jnp.minimum(top1, blk))
            top1 = jnp.maximum(top1, blk)
        return top1, top2

    def score_group(kp, carry):
        for u in range(DSA_UNROLL):
            carry = score_body(kp * DSA_UNROLL + u, carry)
        return carry

    lowest = jnp.full((tq, LANE), INT_MIN, I32)
    tops = lax.fori_loop(0, nkc // DSA_UNROLL, score_group, (lowest, lowest))
    top1, top2 = lax.fori_loop((nkc // DSA_UNROLL) * DSA_UNROLL, nkc, score_body, tops)

    n_blocks = tq // COUNT_ROWS

    def count_ge(t_rep, wanted):
        parts = []
        for blk in range(n_blocks):
            r0 = blk * COUNT_ROWS
            t_b = jnp.tile(t_rep[r0:r0 + COUNT_ROWS], (1, nlb))
            n_sweep = jnp.where(wanted[blk] > 0, nkc, 0)

            def chunk(kc, cnt, r0=r0, t_b=t_b):
                off = pl.multiple_of(kc * tk, tk)
                hit = key_ref[r0:r0 + COUNT_ROWS, pl.ds(off, tk)] >= t_b
                return cnt + _lane_fold(jnp.where(hit, 1, 0))

            def group(kp, cnt, chunk=chunk):
                for u in range(COUNT_UNROLL):
                    cnt = chunk(kp * COUNT_UNROLL + u, cnt)
                return cnt

            cnt = lax.fori_loop(0, n_sweep // COUNT_UNROLL, group, jnp.zeros((COUNT_ROWS, LANE), I32))
            parts.append(lax.fori_loop((n_sweep // COUNT_UNROLL) * COUNT_UNROLL, n_sweep, chunk, cnt))
        return jnp.sum(jnp.concatenate(parts, axis=0), axis=-1, keepdims=True)

    def count(pred):
        def chunk(kc, cnt):
            off = pl.multiple_of(kc * tk, tk)
            hit = pred(key_ref[:, pl.ds(off, tk)], col0 + off)
            return cnt + _lane_fold(jnp.where(hit, 1, 0))
        cnt = lax.fori_loop(0, nkc, chunk, jnp.zeros((tq, LANE), I32))
        return jnp.sum(cnt, axis=-1, keepdims=True)

    rep = lambda v: jnp.broadcast_to(v, (tq, LANE))
    n_valid = row[:, 0:LANE] + 1
    lo0 = rep(jnp.maximum(jnp.min(top2, axis=-1, keepdims=True), INT_MIN + 1))
    hi0 = jnp.maximum(rep(jnp.max(top2 if topk > LANE + 1 else top1, axis=-1, keepdims=True)), lo0)
    searching0 = n_valid > topk

    def any_row(flag):
        return jnp.max(jnp.where(flag, 1, 0))

    def block_flags(active):
        return tuple(any_row(active[blk * COUNT_ROWS:(blk + 1) * COUNT_ROWS]) for blk in range(n_blocks))

    def bis_cond(st):
        flags = st[0]
        busy = flags[0]
        for f in flags[1:]:
            busy = jnp.maximum(busy, f)
        return busy > 0

    def bis_body(st):
        flags, lo, hi, cnt_lo = st
        for _ in range(2):
            active = searching0 & (lo < hi) & (cnt_lo != topk)
            mid = (lo >> 1) + (hi >> 1) + ((lo | hi) & 1)
            cnt = rep(count_ge(mid, flags))
            up = active & (cnt >= topk)
            dn = active & (cnt < topk)
            lo = jnp.where(up, mid, lo)
            cnt_lo = jnp.where(up, cnt, cnt_lo)
            hi = jnp.where(dn, mid - 1, hi)
        return block_flags(searching0 & (lo < hi) & (cnt_lo != topk)), lo, hi, cnt_lo

    st0 = (block_flags(searching0 & (lo0 < hi0)), lo0, hi0, jnp.full((tq, LANE), topk + 1, I32))
    _, thr, _, cnt_thr = lax.while_loop(bis_cond, bis_body, st0)
    has_ties = any_row(searching0 & (cnt_thr != topk)) > 0
    searching0 = searching0[:, 0:1]
    thr = jnp.where(searching0, thr[:, 0:1], INT_MIN + 1)

    q = q_ref[...]
    qs = jnp.concatenate([q[:, h * DSA_LATENT:(h + 1) * DSA_LATENT] for h in range(DSA_HEADS)], axis=0)
    n_groups = 4
    grows = DSA_HEADS * tq // n_groups

    cut_ref[...] = jnp.full(cut_ref.shape, s_len, I32)

    @pl.when(has_ties)
    def _():
        need = topk - count(lambda k, c: k > thr)

        def cut_body(it, cut):
            cand = cut | jnp.left_shift(jnp.int32(1), nbits_col - 1 - it)
            cnt = count(lambda k, c: (k == thr) & (c < cand))
            return jnp.where(cnt < need, cand, cut)

        cut = lax.fori_loop(0, nbits_col, cut_body, jnp.zeros((tq, 1), I32))
        cut_ref[...] = jnp.broadcast_to(cut + 1, cut_ref.shape)

    tie_cut = cut_ref[:, 0:1]

    @pl.when(qi == 0)
    def _():
        def body(i, mx):
            cf = c_ref[pl.ds(pl.multiple_of(i * tk, tk), tk), :].astype(F32)
            return jnp.maximum(mx, jnp.max(jnp.sum(cf * cf, axis=-1, keepdims=True), axis=0, keepdims=True))
        mx = lax.fori_loop(0, s_len // tk, body, jnp.zeros((1, 1), F32))
        cn_ref[...] = jnp.broadcast_to(jnp.sqrt(mx), cn_ref.shape)

    def selection_bias(kc):
        off = pl.multiple_of(kc * tk, tk)
        keys = key_ref[:, pl.ds(off, tk)]
        sel = (keys > thr) | ((keys == thr) & (col0 + off < tie_cut))
        return off, jnp.where(sel, 0.0, NEG_BIG)

    def group_scores(g, cc, bias):
        s = _dot_nt(qs[g * grows:(g + 1) * grows, :], cc)
        return (s.reshape(grows // tq, tq, tk) + bias[None]).reshape(grows, tk)

    def finalize(l_col):
        z = z_ref[...].astype(F32)
        for h in range(DSA_HEADS):
            rs = slice(h * tq, (h + 1) * tq)
            ob = (acc_ref[rs, :] / l_col[rs, :]).astype(BF16)
            oh = jnp.dot(ob, wuv_ref[h], preferred_element_type=F32)
            sl = slice(h * DSA_V_DIM, (h + 1) * DSA_V_DIM)
            o_ref[:, sl] = (oh * _silu(z[:, sl])).astype(o_ref.dtype)

    qf = qs.astype(F32)
    m_ref[...] = jnp.broadcast_to(jnp.sqrt(jnp.sum(qf * qf, axis=-1, keepdims=True))
                                  * cn_ref[:, 0:1] * BOUND_SLACK, m_ref.shape)
    l_ref[...] = jnp.zeros(l_ref.shape, F32)
    acc_ref[...] = jnp.zeros(acc_ref.shape, F32)

    def bounded_chunk(kc):
        off, bias = selection_bias(kc)
        cc = c_ref[pl.ds(off, tk), :]
        for g in range(n_groups):
            rs = slice(g * grows, (g + 1) * grows)
            p = jnp.exp2(group_scores(g, cc, bias) - jnp.tile(m_ref[rs, :], (1, nlb)))
            l_ref[rs, :] = l_ref[rs, :] + _lane_fold(p)
            acc_ref[rs, :] = acc_ref[rs, :] + jnp.dot(p.astype(BF16), cc, preferred_element_type=F32)

    def bounded_group(kp, carry):
        for u in range(DSA_UNROLL):
            bounded_chunk(kp * DSA_UNROLL + u)
        return carry

    def bounded_tail(kc, carry):
        bounded_chunk(kc)
        return carry

    lax.fori_loop(0, nkc // DSA_UNROLL, bounded_group, 0)
    lax.fori_loop((nkc // DSA_UNROLL) * DSA_UNROLL, nkc, bounded_tail, 0)
    l_tot = jnp.sum(l_ref[...], axis=-1, keepdims=True)
    underflow = jnp.logical_not(jnp.min(l_tot) >= UNDERFLOW_GUARD)

    @pl.when(jnp.logical_not(underflow))
    def _():
        finalize(l_tot)

    @pl.when(underflow)
    def _():
        m_ref[...] = jnp.full(m_ref.shape, NEG_BIG, F32)
        l_ref[...] = jnp.zeros(l_ref.shape, F32)
        acc_ref[...] = jnp.zeros(acc_ref.shape, F32)

        def body(kc, carry):
            off, bias = selection_bias(kc)
            cc = c_ref[pl.ds(off, tk), :]
            for g in range(n_groups):
                rs = slice(g * grows, (g + 1) * grows)
                s = group_scores(g, cc, bias)
                m_prev = m_ref[rs, :]
                m_new = jnp.maximum(m_prev, jnp.max(s, axis=-1, keepdims=True))
                alpha = jnp.exp2(m_prev - m_new)
                p = jnp.exp2(s - jnp.tile(m_new, (1, nlb)))
                l_ref[rs, :] = alpha * l_ref[rs, :] + jnp.sum(p, axis=-1, keepdims=True)
                acc_ref[rs, :] = (jnp.tile(alpha, (1, DSA_LATENT // LANE)) * acc_ref[rs, :]
                                  + jnp.dot(p.astype(BF16), cc, preferred_element_type=F32))
                m_ref[rs, :] = m_new
            return carry

        lax.fori_loop(0, nkc, body, 0)
        finalize(l_ref[:, 0:1])


def _dsa_attention(p_main, iw, c_lat, ik_n, w_uv, b, s, tq, tk):
    m = b * s
    nq = s // tq
    topk = min(IDX_TOPK_MAX, s // 4)
    assert topk <= 2 * LANE, "the per-lane top-two start of the threshold search covers 2 * LANE keys"
    c_q = MAIN_OFF[SEG_DSA_Q] // (DSA_HEADS * DSA_LATENT)
    c_z = MAIN_OFF[SEG_DSA_Z] // BRANCH_WIDTH
    c_iq = MAIN_OFF[SEG_IDX_Q] // (IDX_HEADS * IDX_DIM)
    nbits_col = max(1, int(math.ceil(math.log2(s))) + 1)
    kern = functools.partial(_dsa_kernel, tq=tq, tk=tk, topk=topk, nbits_col=nbits_col, s_len=s)
    return pl.pallas_call(
        kern,
        grid=(b, nq),
        in_specs=[
            pl.BlockSpec((tq, DSA_HEADS * DSA_LATENT), lambda bi, qi: (bi * nq + qi, c_q)),
            pl.BlockSpec((tq, IDX_HEADS * IDX_DIM), lambda bi, qi: (bi * nq + qi, c_iq)),
            pl.BlockSpec((tq, IDX_HEADS), lambda bi, qi: (bi * nq + qi, 0)),
            pl.BlockSpec((tq, BRANCH_WIDTH), lambda bi, qi: (bi * nq + qi, c_z)),
            pl.BlockSpec((s, DSA_LATENT), lambda bi, qi: (bi, 0), pipeline_mode=pl.Buffered(1)),
            pl.BlockSpec((s, IDX_DIM), lambda bi, qi: (bi, 0), pipeline_mode=pl.Buffered(1)),
            pl.BlockSpec((DSA_HEADS, DSA_LATENT, DSA_V_DIM), lambda bi, qi: (0, 0, 0)),
        ],
        out_specs=pl.BlockSpec((tq, DSA_HEADS * DSA_V_DIM), lambda bi, qi: (bi * nq + qi, 0)),
        out_shape=jax.ShapeDtypeStruct((m, DSA_HEADS * DSA_V_DIM), BF16),
        scratch_shapes=[pltpu.VMEM((tq, s), I32),
                        pltpu.VMEM((DSA_HEADS * tq, LANE), F32),
                        pltpu.VMEM((DSA_HEADS * tq, LANE), F32),
                        pltpu.VMEM((DSA_HEADS * tq, DSA_LATENT), F32),
                        pltpu.VMEM((tq, LANE), I32),
                        pltpu.VMEM((1, LANE), F32),
                        pltpu.VMEM((IDX_HEADS, tq, LANE), F32)],
        compiler_params=_cparams(("parallel", "arbitrary")),
        name="dsa_attention",
    )(p_main, p_main, iw, p_main, c_lat, ik_n, w_uv)


def _out_kernel(x_ref, ya_ref, yb_ref, yc_ref, ga_ref, gb_ref, gc_ref, wb_ref, wo_ref, pg_ref, o_ref):
    mixed = None
    for i, (y_ref, g_ref) in enumerate(((ya_ref, ga_ref), (yb_ref, gb_ref), (yc_ref, gc_ref))):
        t = _sigmoid(g_ref[...].astype(F32)) * jnp.dot(y_ref[...], wb_ref[i], preferred_element_type=F32)
        mixed = t if mixed is None else mixed + t
    out = jnp.dot(mixed.astype(BF16), wo_ref[...], preferred_element_type=F32)
    ms = jnp.mean(out * out, axis=-1, keepdims=True)
    o_ref[...] = x_ref[...] + out * lax.rsqrt(ms + NORM_EPS) * pg_ref[...]


def _merge_out(x2, ya, yb, yc, p_main, w_branch, w_out, post_g, tm):
    m, d = x2.shape
    cg = MAIN_OFF[SEG_GATES] // d
    row = lambda i: (i, 0)
    return pl.pallas_call(
        _out_kernel,
        grid=(m // tm,),
        in_specs=[
            pl.BlockSpec((tm, d), row),
            pl.BlockSpec((tm, BRANCH_WIDTH), row),
            pl.BlockSpec((tm, BRANCH_WIDTH), row),
            pl.BlockSpec((tm, BRANCH_WIDTH), row),
            pl.BlockSpec((tm, d), lambda i: (i, cg)),
            pl.BlockSpec((tm, d), lambda i: (i, cg + 1)),
            pl.BlockSpec((tm, d), lambda i: (i, cg + 2)),
            pl.BlockSpec((N_BRANCHES, BRANCH_WIDTH, d), lambda i: (0, 0, 0), pipeline_mode=pl.Buffered(1)),
            pl.BlockSpec((d, d), lambda i: (0, 0), pipeline_mode=pl.Buffered(1)),
            pl.BlockSpec((1, d), lambda i: (0, 0)),
        ],
        out_specs=pl.BlockSpec((tm, d), row),
        out_shape=jax.ShapeDtypeStruct((m, d), F32),
        compiler_params=_cparams(("parallel",)),
        name="merge_out_proj",
    )(x2, ya, yb, yc, p_main, p_main, p_main, w_branch, w_out, post_g.reshape(1, d))


def _pick(total, want):
    t = min(total, want)
    while total % t:
        t //= 2
    return t


def kernel(x, pre_norm_g, w_in, da_lambda, da_norm_g, ml_conv_w, ml_conv_b, ml_gate_b, ml_norm_g,
           dsa_kv_norm_g, dsa_ik_norm_g, dsa_w_uv, w_branch, w_out, post_norm_g):
    b, s, d = x.shape
    m = b * s
    bounds = _seg_bounds()
    x2 = x.reshape(m, d)
    tm_proj = _pick(m, 1024)
    t_da = _pick(s, 512)
    l_ml = _pick(s, 256)
    tq_dsa = _pick(s, 256)
    tk_dsa = _pick(s, 512)
    o_if = SMALL_OFF[SEG_ML_IF]
    o_iw = SMALL_OFF[SEG_IDX_W]
    seg_scale = {SEG_DA_Q: DA_HEAD_DIM ** -0.5 * LOG2E, SEG_DSA_Q: DSA_LATENT ** -0.5 * LOG2E}
    col_scale = jnp.concatenate([jnp.full((1, n), seg_scale.get(sg, 1.0), F32) for sg, n in enumerate(IN_SPLITS)],
                                axis=1)
    for l in range(DEPTH):
        w = (w_in[l] * col_scale).astype(BF16)
        w_main = jnp.concatenate([w[:, bounds[sg]:bounds[sg + 1]] for sg in MAIN_SEGS], axis=1)
        w_small = jnp.concatenate([w[:, bounds[sg]:bounds[sg + 1]] for sg in SMALL_SEGS]
                                  + [jnp.zeros((d, SMALL_WIDTH - SMALL_USED), BF16)], axis=1)
        p_main, p_small = _project(x2, pre_norm_g[l], w_main, w_small, tm_proj, _pick(MAIN_WIDTH, 2048))

        lam_init = 0.8 - 0.6 * math.exp(-0.3 * l)
        y_a = _diff_attention(p_main, da_lambda[l], da_norm_g[l], lam_init, b, s, t_da)

        gif = p_small[:, o_if:o_if + 2 * ML_HEADS]
        gif_t = jnp.transpose(gif.reshape(b, s, 2 * ML_HEADS), (0, 2, 1))
        y_b = _mlstm(p_main, gif, gif_t, ml_conv_w[l], ml_conv_b[l], ml_gate_b[l], ml_norm_g[l], b, s, l_ml)

        c_lat, ik_n = _dsa_norms(p_small, dsa_kv_norm_g[l], dsa_ik_norm_g[l], _pick(m, 2048))
        iw = p_small[:, o_iw:o_iw + IDX_HEADS]
        y_c = _dsa_attention(p_main, iw, c_lat, ik_n, dsa_w_uv[l].astype(BF16), b, s, tq_dsa, tk_dsa)

        x2 = _merge_out(x2, y_a, y_b, y_c, p_main, w_branch[l].astype(BF16), w_out[l].astype(BF16),
                        post_norm_g[l], _pick(m, 256))
    return x2.reshape(b, s, d)
```

```python
import functools
import math

import jax
import jax.numpy as jnp
from jax import lax
from jax.experimental import pallas as pl
from jax.experimental.pallas import tpu as pltpu

F32 = jnp.float32
BF16 = jnp.bfloat16
I32 = jnp.int32

D_MODEL = 2048
DEPTH = 2
BRANCH_WIDTH = 1024
N_BRANCHES = 3
NORM_EPS = 1e-6
DA_HEADS = 8
DA_HEAD_DIM = 64
DA_V_DIM = 128
ML_HEADS = 4
ML_QK_DIM = 128
ML_V_DIM = 256
ML_CONV = 4
ML_QK_WIDTH = 2 * ML_HEADS * ML_QK_DIM
DSA_HEADS = 8
DSA_LATENT = 256
DSA_V_DIM = 128
IDX_HEADS = 16
IDX_DIM = 64
IDX_TOPK_MAX = 256

IN_SPLITS = (
    DA_HEADS * 2 * DA_HEAD_DIM, DA_HEADS * 2 * DA_HEAD_DIM, DA_HEADS * DA_V_DIM, BRANCH_WIDTH,
    ML_QK_WIDTH, ML_HEADS * ML_V_DIM, 2 * ML_HEADS,
    BRANCH_WIDTH, BRANCH_WIDTH,
    DSA_HEADS * DSA_LATENT, DSA_LATENT,
    IDX_HEADS * IDX_DIM, IDX_DIM, IDX_HEADS,
    BRANCH_WIDTH,
    N_BRANCHES * D_MODEL,
)
(SEG_DA_Q, SEG_DA_K, SEG_DA_V, SEG_DA_Z, SEG_ML_QK, SEG_ML_V, SEG_ML_IF, SEG_ML_O, SEG_ML_Z,
 SEG_DSA_Q, SEG_DSA_KV, SEG_IDX_Q, SEG_IDX_K, SEG_IDX_W, SEG_DSA_Z, SEG_GATES) = range(16)

MAIN_SEGS = (SEG_DA_Q, SEG_DA_K, SEG_DA_V, SEG_DA_Z, SEG_ML_QK, SEG_ML_V, SEG_ML_O, SEG_ML_Z,
             SEG_DSA_Q, SEG_IDX_Q, SEG_DSA_Z, SEG_GATES)
SMALL_SEGS = (SEG_DSA_KV, SEG_IDX_K, SEG_IDX_W, SEG_ML_IF)
SMALL_WIDTH = 384

LANE = 128
SUBLANE = 8
V7X_VMEM_BYTES = 64 * 1024 * 1024
VMEM_LIMIT = V7X_VMEM_BYTES * 7 // 8
INT_MIN = -2147483648
NEG_BIG = -1e30
LOG2E = math.log2(math.e)
DA_HEAD_GROUP = 2
DA_UNROLL = 4
DSA_UNROLL = 4
COUNT_ROWS = 64
COUNT_UNROLL = 4
BOUND_SLACK = 1.01
UNDERFLOW_GUARD = 2.0 ** -100


def _seg_bounds():
    offs = [0]
    for n in IN_SPLITS:
        offs.append(offs[-1] + n)
    return offs


def _main_offsets():
    out, off = {}, 0
    for s in MAIN_SEGS:
        out[s] = off
        off += IN_SPLITS[s]
    return out, off


def _small_offsets():
    out, off = {}, 0
    for s in SMALL_SEGS:
        out[s] = off
        off += IN_SPLITS[s]
    return out, off


MAIN_OFF, MAIN_WIDTH = _main_offsets()
SMALL_OFF, SMALL_USED = _small_offsets()


def _cparams(sem):
    return pltpu.CompilerParams(dimension_semantics=sem, vmem_limit_bytes=VMEM_LIMIT)


def _silu(v):
    return v * (1.0 / (1.0 + jnp.exp(-v)))


def _sigmoid(v):
    return 1.0 / (1.0 + jnp.exp(-v))


def _dot_nt(a, b):
    return lax.dot_general(a, b, (((1,), (1,)), ((), ())), preferred_element_type=F32)


def _proj_kernel(x_ref, g_ref, w_ref, ws_ref, o_ref, os_ref, h_ref):
    @pl.when(pl.program_id(1) == 0)
    def _():
        x = x_ref[...]
        ms = jnp.mean(x * x, axis=-1, keepdims=True)
        h_ref[...] = (x * lax.rsqrt(ms + NORM_EPS) * g_ref[...]).astype(BF16)
        os_ref[...] = jnp.dot(h_ref[...], ws_ref[...], preferred_element_type=F32)

    o_ref[...] = jnp.dot(h_ref[...], w_ref[...], preferred_element_type=F32).astype(o_ref.dtype)


def _project(x2, g, w, w_small, tm, tn):
    m, d = x2.shape
    n = w.shape[1]
    ns = w_small.shape[1]
    return pl.pallas_call(
        _proj_kernel,
        grid=(m // tm, n // tn),
        in_specs=[pl.BlockSpec((tm, d), lambda i, j: (i, 0)),
                  pl.BlockSpec((1, d), lambda i, j: (0, 0)),
                  pl.BlockSpec((d, tn), lambda i, j: (0, j)),
                  pl.BlockSpec((d, ns), lambda i, j: (0, 0))],
        out_specs=[pl.BlockSpec((tm, tn), lambda i, j: (i, j)),
                   pl.BlockSpec((tm, ns), lambda i, j: (i, 0))],
        out_shape=[jax.ShapeDtypeStruct((m, n), BF16), jax.ShapeDtypeStruct((m, ns), F32)],
        scratch_shapes=[pltpu.VMEM((tm, d), BF16)],
        compiler_params=_cparams(("parallel", "arbitrary")),
        name="rmsnorm_in_proj",
    )(x2, g.reshape(1, d), w, w_small)


def _lane_fold(x):
    part = x[:, 0:LANE]
    for j in range(1, x.shape[1] // LANE):
        part = part + x[:, j * LANE:(j + 1) * LANE]
    return part


def _da_kernel(lam_ref, q_ref, k_ref, v_ref, z_ref, g_ref, o_ref, m_ref, l_ref, acc_ref, kn_ref,
               *, t, s_len, lam_init):
    qi = pl.program_id(2)
    nrep = t // LANE
    nbr = 2 * DA_HEAD_GROUP
    br = [slice(j * DA_HEAD_DIM, (j + 1) * DA_HEAD_DIM) for j in range(nbr)]
    vcols = [slice((j // 2) * DA_V_DIM, (j // 2 + 1) * DA_V_DIM) for j in range(nbr)]

    @pl.when(qi == 0)
    def _():
        def body(i, mx):
            kk = k_ref[pl.ds(pl.multiple_of(i * t, t), t), :].astype(F32)
            sq = kk * kk
            return tuple(jnp.maximum(mx[j], jnp.max(jnp.sum(sq[:, br[j]], axis=-1, keepdims=True),
                                                    axis=0, keepdims=True)) for j in range(nbr))
        mx = lax.fori_loop(0, s_len // t, body, tuple(jnp.zeros((1, 1), F32) for _ in range(nbr)))
        for j in range(nbr):
            kn_ref[j] = jnp.broadcast_to(jnp.sqrt(mx[j]), (1, LANE))

    q = q_ref[...]
    qsq = q.astype(F32)
    qsq = qsq * qsq
    bound = [jnp.broadcast_to(jnp.sqrt(jnp.sum(qsq[:, br[j]], axis=-1, keepdims=True))
                              * kn_ref[j][:, 0:1] * BOUND_SLACK, (t, LANE)) for j in range(nbr)]

    def causal_keep():
        row = lax.broadcasted_iota(I32, (t, t), 0)
        col = lax.broadcasted_iota(I32, (t, t), 1)
        return col <= row

    def sweep(block):
        def body(kp, carry):
            for u in range(DA_UNROLL):
                block(pl.multiple_of(kp * (DA_UNROLL * t) + u * t, t), False)
            return carry
        lax.fori_loop(0, qi // DA_UNROLL, body, 0)

        def tail(ki, carry):
            block(pl.multiple_of(ki * t, t), False)
            return carry
        lax.fori_loop((qi // DA_UNROLL) * DA_UNROLL, qi, tail, 0)

        block(pl.multiple_of(qi * t, t), True)

    def bounded_block(off, masked):
        k = k_ref[pl.ds(off, t), :]
        v = v_ref[pl.ds(off, t), :]
        for j in range(nbr):
            s = _dot_nt(q[:, br[j]], k[:, br[j]])
            if masked:
                s = jnp.where(causal_keep(), s, -jnp.inf)
            p = jnp.exp2(s - jnp.tile(bound[j], (1, nrep)))
            l_ref[j] = l_ref[j] + _lane_fold(p)
            acc_ref[j] = acc_ref[j] + jnp.dot(p.astype(BF16), v[:, vcols[j]], preferred_element_type=F32)

    def running_max_block(off, masked):
        k = k_ref[pl.ds(off, t), :]
        v = v_ref[pl.ds(off, t), :]
        for j in range(nbr):
            s = _dot_nt(q[:, br[j]], k[:, br[j]])
            if masked:
                s = jnp.where(causal_keep(), s, -jnp.inf)
            m_prev = m_ref[j]
            m_new = jnp.maximum(m_prev, jnp.max(s, axis=-1, keepdims=True))
            alpha = jnp.exp2(m_prev - m_new)
            p = jnp.exp2(s - jnp.tile(m_new, (1, nrep)))
            l_ref[j] = alpha * l_ref[j] + jnp.sum(p, axis=-1, keepdims=True)
            acc_ref[j] = alpha * acc_ref[j] + jnp.dot(p.astype(BF16), v[:, vcols[j]],
                                                      preferred_element_type=F32)
            m_ref[j] = m_new

    l_ref[...] = jnp.zeros(l_ref.shape, F32)
    acc_ref[...] = jnp.zeros(acc_ref.shape, F32)
    sweep(bounded_block)
    l_tot = [jnp.sum(l_ref[j], axis=-1, keepdims=True) for j in range(nbr)]
    l_min = l_tot[0]
    for j in range(1, nbr):
        l_min = jnp.minimum(l_min, l_tot[j])
    underflow = jnp.logical_not(jnp.min(l_min) >= UNDERFLOW_GUARD)

    @pl.when(jnp.logical_not(underflow))
    def _():
        for j in range(nbr):
            acc_ref[j] = acc_ref[j] / l_tot[j]

    @pl.when(underflow)
    def _():
        m_ref[...] = jnp.full(m_ref.shape, -jnp.inf, F32)
        l_ref[...] = jnp.zeros(l_ref.shape, F32)
        acc_ref[...] = jnp.zeros(acc_ref.shape, F32)
        sweep(running_max_block)
        for j in range(nbr):
            acc_ref[j] = acc_ref[j] / l_ref[j]

    lp = lam_ref[...]
    lam = (jnp.exp(jnp.sum(lp[0:1] * lp[1:2], axis=-1, keepdims=True))
           - jnp.exp(jnp.sum(lp[2:3] * lp[3:4], axis=-1, keepdims=True)) + lam_init)
    for hd in range(DA_HEAD_GROUP):
        cols = slice(hd * DA_V_DIM, (hd + 1) * DA_V_DIM)
        o = acc_ref[2 * hd] - lam * acc_ref[2 * hd + 1]
        ms = jnp.mean(o * o, axis=-1, keepdims=True)
        o = o * lax.rsqrt(ms + NORM_EPS) * g_ref[...] * (1.0 - lam_init)
        o_ref[:, cols] = (o * _silu(z_ref[:, cols].astype(F32))).astype(o_ref.dtype)


def _diff_attention(p_main, lam_params, norm_g, lam_init, b, s, t):
    m = b * s
    nq = s // t
    wg = DA_HEAD_GROUP * LANE
    cq = MAIN_OFF[SEG_DA_Q] // wg
    ck = MAIN_OFF[SEG_DA_K] // wg
    cv = MAIN_OFF[SEG_DA_V] // wg
    cz = MAIN_OFF[SEG_DA_Z] // wg
    nbr = 2 * DA_HEAD_GROUP
    kern = functools.partial(_da_kernel, t=t, s_len=s, lam_init=lam_init)
    return pl.pallas_call(
        kern,
        grid=(b, DA_HEADS // DA_HEAD_GROUP, nq),
        in_specs=[
            pl.BlockSpec((4, DA_HEAD_DIM), lambda bi, h, qi: (0, 0)),
            pl.BlockSpec((t, wg), lambda bi, h, qi: (bi * nq + qi, cq + h)),
            pl.BlockSpec((s, wg), lambda bi, h, qi: (bi, ck + h)),
            pl.BlockSpec((s, wg), lambda bi, h, qi: (bi, cv + h)),
            pl.BlockSpec((t, wg), lambda bi, h, qi: (bi * nq + qi, cz + h)),
            pl.BlockSpec((1, DA_V_DIM), lambda bi, h, qi: (0, 0)),
        ],
        out_specs=pl.BlockSpec((t, wg), lambda bi, h, qi: (bi * nq + qi, h)),
        out_shape=jax.ShapeDtypeStruct((m, DA_HEADS * DA_V_DIM), BF16),
        scratch_shapes=[pltpu.VMEM((nbr, t, LANE), F32), pltpu.VMEM((nbr, t, LANE), F32),
                        pltpu.VMEM((nbr, t, DA_V_DIM), F32), pltpu.VMEM((nbr, 1, LANE), F32)],
        compiler_params=_cparams(("parallel", "parallel", "arbitrary")),
        name="diff_attention",
    )(lam_params, p_main, p_main, p_main, p_main, norm_g.reshape(1, DA_V_DIM))


def _split3(v):
    hi = v.astype(BF16)
    r1 = v - hi.astype(F32)
    mid = r1.astype(BF16)
    lo = (r1 - mid.astype(F32)).astype(BF16)
    return hi, mid, lo


def _log_sigmoid(v):
    return jnp.minimum(v, 0.0) - jnp.log(1.0 + jnp.exp(-jnp.abs(v)))


def _ml_kernel(qk_ref, v_ref, o_ref, z_ref, gif_ref, gift_ref, cw_ref, cb_ref, gb_ref, gbt_ref, ng_ref,
               y_ref, xbuf, c_st, n_st, m_st, *, L):
    ci = pl.program_id(1)
    pad = SUBLANE

    @pl.when(ci == 0)
    def _():
        xbuf[0:pad, :] = jnp.zeros((pad, ML_QK_WIDTH), F32)
        c_st[...] = jnp.zeros(c_st.shape, F32)
        n_st[...] = jnp.zeros(n_st.shape, F32)
        m_st[...] = jnp.zeros(m_st.shape, F32)

    xbuf[pad:pad + L, :] = qk_ref[...].astype(F32)
    cw = cw_ref[...]
    y = xbuf[pad:pad + L, :] * cw[ML_CONV - 1:ML_CONV, :]
    for j in range(ML_CONV - 1):
        sh = ML_CONV - 1 - j
        y = y + xbuf[pad - sh:pad - sh + L, :] * cw[j:j + 1, :]
    y = _silu(y + cb_ref[...])
    xbuf[0:pad, :] = xbuf[L:L + pad, :]

    g_c = gif_ref[...] + gb_ref[...]
    g_r = gift_ref[0] + gbt_ref[...]
    lf_c = _log_sigmoid(g_c)
    lf_r = _log_sigmoid(g_r)
    ti = lax.broadcasted_iota(I32, (L, L), 0)
    si = lax.broadcasted_iota(I32, (L, L), 1)
    causal = si <= ti
    tri = jnp.where(causal, 1.0, 0.0).astype(BF16)
    triu = jnp.where(ti <= si, 1.0, 0.0).astype(BF16)
    bc_all = sum(jnp.dot(tri, piece, preferred_element_type=F32) for piece in _split3(lf_c))
    br_all = sum(jnp.dot(piece, triu, preferred_element_type=F32) for piece in _split3(lf_r))

    for h in range(ML_HEADS):
        q = y[:, h * ML_QK_DIM:(h + 1) * ML_QK_DIM].astype(BF16)
        kf = y[:, ML_HEADS * ML_QK_DIM + h * ML_QK_DIM:ML_HEADS * ML_QK_DIM + (h + 1) * ML_QK_DIM] * (ML_QK_DIM ** -0.5)
        k = kf.astype(BF16)
        v = v_ref[:, h * ML_V_DIM:(h + 1) * ML_V_DIM]
        bc = bc_all[:, ML_HEADS + h:ML_HEADS + h + 1]
        br = br_all[ML_HEADS + h:ML_HEADS + h + 1, :]
        ig_c = g_c[:, h:h + 1]
        ig_r = g_r[h:h + 1, :]
        m_prev = m_st[h][:, 0:1]
        c_prev = c_st[h]
        n_prev = n_st[h]

        dmat = jnp.where(causal, bc - br + ig_r, -jnp.inf)
        inter = bc + m_prev
        m_t = jnp.maximum(jnp.max(dmat, axis=-1, keepdims=True), inter)
        w = jnp.exp(dmat - m_t) * _dot_nt(q, k)
        decay = jnp.exp(inter - m_t)
        num = (jnp.dot(w.astype(BF16), v, preferred_element_type=F32)
               + decay * jnp.dot(q, c_prev.astype(BF16), preferred_element_type=F32))
        qn = jnp.sum(q.astype(F32) * n_prev.astype(BF16).astype(F32), axis=-1, keepdims=True)
        den = jnp.sum(w, axis=-1, keepdims=True) + decay * qn
        hh = num / jnp.maximum(jnp.abs(den), jnp.exp(-m_t))

        b_last = br[:, L - 1:L]
        g_row = b_last - br + ig_r
        m_new = jnp.maximum(b_last + m_prev, jnp.max(g_row, axis=-1, keepdims=True))
        wk_c = jnp.exp(b_last - bc + ig_c - m_new)
        cd = jnp.exp(b_last + m_prev - m_new)
        kw = (kf * wk_c)
        c_st[h] = cd * c_prev + lax.dot_general(kw.astype(BF16), v, (((0,), (0,)), ((), ())),
                                                preferred_element_type=F32)
        n_st[h] = cd * n_prev + jnp.sum(kw, axis=0, keepdims=True)
        m_st[h] = jnp.broadcast_to(m_new, (1, LANE))

        ms = jnp.mean(hh * hh, axis=-1, keepdims=True)
        hn = hh * lax.rsqrt(ms + NORM_EPS) * ng_ref[...]
        sl = slice(h * ML_V_DIM, (h + 1) * ML_V_DIM)
        out = hn * _sigmoid(o_ref[:, sl].astype(F32)) * _silu(z_ref[:, sl].astype(F32))
        y_ref[:, sl] = out.astype(y_ref.dtype)


def _mlstm(p_main, gif, gif_t, conv_w, conv_b, gate_b, norm_g, b, s, L):
    m = b * s
    nc = s // L
    w1k = BRANCH_WIDTH
    c_qk = MAIN_OFF[SEG_ML_QK] // w1k
    c_v = MAIN_OFF[SEG_ML_V] // w1k
    c_o = MAIN_OFF[SEG_ML_O] // w1k
    c_z = MAIN_OFF[SEG_ML_Z] // w1k
    gb = gate_b.reshape(1, 2 * ML_HEADS)
    kern = functools.partial(_ml_kernel, L=L)
    full = lambda shp: pl.BlockSpec(shp, lambda bi, ci: (0,) * len(shp))
    return pl.pallas_call(
        kern,
        grid=(b, nc),
        in_specs=[
            pl.BlockSpec((L, w1k), lambda bi, ci: (bi * nc + ci, c_qk)),
            pl.BlockSpec((L, w1k), lambda bi, ci: (bi * nc + ci, c_v)),
            pl.BlockSpec((L, w1k), lambda bi, ci: (bi * nc + ci, c_o)),
            pl.BlockSpec((L, w1k), lambda bi, ci: (bi * nc + ci, c_z)),
            pl.BlockSpec((L, 2 * ML_HEADS), lambda bi, ci: (bi * nc + ci, 0)),
            pl.BlockSpec((1, 2 * ML_HEADS, L), lambda bi, ci: (bi, 0, ci)),
            full((ML_CONV, ML_QK_WIDTH)),
            full((1, ML_QK_WIDTH)),
            full((1, 2 * ML_HEADS)),
            full((2 * ML_HEADS, 1)),
            full((1, ML_V_DIM)),
        ],
        out_specs=pl.BlockSpec((L, w1k), lambda bi, ci: (bi * nc + ci, 0)),
        out_shape=jax.ShapeDtypeStruct((m, ML_HEADS * ML_V_DIM), BF16),
        scratch_shapes=[pltpu.VMEM((L + SUBLANE, ML_QK_WIDTH), F32),
                        pltpu.VMEM((ML_HEADS, ML_QK_DIM, ML_V_DIM), F32),
                        pltpu.VMEM((ML_HEADS, 1, ML_QK_DIM), F32),
                        pltpu.VMEM((ML_HEADS, 1, LANE), F32)],
        compiler_params=_cparams(("parallel", "arbitrary")),
        name="mlstm",
    )(p_main, p_main, p_main, p_main, gif, gif_t, conv_w, conv_b.reshape(1, ML_QK_WIDTH),
      gb, gb.reshape(2 * ML_HEADS, 1), norm_g.reshape(1, ML_V_DIM))


def _dsa_norm_kernel(p_ref, gkv_ref, gik_ref, c_ref, ik_ref):
    o_kv = SMALL_OFF[SEG_DSA_KV]
    o_ik = SMALL_OFF[SEG_IDX_K]
    ckv = p_ref[:, o_kv:o_kv + DSA_LATENT]
    ms = jnp.mean(ckv * ckv, axis=-1, keepdims=True)
    c_ref[...] = (ckv * lax.rsqrt(ms + NORM_EPS) * gkv_ref[...]).astype(c_ref.dtype)
    ik = p_ref[:, o_ik:o_ik + IDX_DIM]
    ms = jnp.mean(ik * ik, axis=-1, keepdims=True)
    ik_ref[...] = (ik * lax.rsqrt(ms + NORM_EPS) * gik_ref[...]).astype(ik_ref.dtype)


def _dsa_norms(p_small, g_kv, g_ik, tm):
    m = p_small.shape[0]
    return pl.pallas_call(
        _dsa_norm_kernel,
        grid=(m // tm,),
        in_specs=[pl.BlockSpec((tm, SMALL_WIDTH), lambda i: (i, 0)),
                  pl.BlockSpec((1, DSA_LATENT), lambda i: (0, 0)),
                  pl.BlockSpec((1, IDX_DIM), lambda i: (0, 0))],
        out_specs=[pl.BlockSpec((tm, DSA_LATENT), lambda i: (i, 0)),
                   pl.BlockSpec((tm, IDX_DIM), lambda i: (i, 0))],
        out_shape=[jax.ShapeDtypeStruct((m, DSA_LATENT), BF16),
                   jax.ShapeDtypeStruct((m, IDX_DIM), BF16)],
        compiler_params=_cparams(("parallel",)),
        name="dsa_norms",
    )(p_small, g_kv.reshape(1, DSA_LATENT), g_ik.reshape(1, IDX_DIM))


def _dsa_kernel(q_ref, iq_ref, iw_ref, z_ref, c_ref, ik_ref, wuv_ref, o_ref,
                key_ref, m_ref, l_ref, acc_ref, cut_ref, cn_ref, wb_ref, *, tq, tk, topk, nbits_col, s_len):
    qi = pl.program_id(1)
    nkc = (qi * tq + tq + tk - 1) // tk
    row = qi * tq + lax.broadcasted_iota(I32, (tq, tk), 0)
    col0 = lax.broadcasted_iota(I32, (tq, tk), 1)
    nlb = tk // LANE

    iw = iw_ref[...] * ((IDX_HEADS ** -0.5) * (IDX_DIM ** -0.5))
    for h in range(IDX_HEADS):
        wb_ref[h] = jnp.broadcast_to(iw[:, h:h + 1], (tq, LANE))
    iq_blk = iq_ref[...]
    iq = jnp.concatenate([iq_blk[:, h * IDX_DIM:(h + 1) * IDX_DIM] for h in range(IDX_HEADS)], axis=0)
    sub = 2 * LANE
    nsub = tk // sub

    def score_body(kc, carry):
        top1, top2 = carry
        off = pl.multiple_of(kc * tk, tk)
        parts = []
        for j in range(nsub):
            ikc = ik_ref[pl.ds(pl.multiple_of(off + j * sub, sub), sub), :]
            sc = _dot_nt(iq, ikc)
            part = None
            for h in range(IDX_HEADS):
                term = jnp.maximum(sc[h * tq:(h + 1) * tq, :], 0.0) * jnp.tile(wb_ref[h], (1, sub // LANE))
                part = term if part is None else part + term
            parts.append(part)
        acc = jnp.concatenate(parts, axis=1)
        bits = lax.bitcast_convert_type(acc, I32)
        skey = jnp.where(bits < 0, (bits ^ 0x7FFFFFFF) + 1, bits)
        skey = jnp.where(col0 + off <= row, skey, INT_MIN)
        key_ref[:, pl.ds(off, tk)] = skey
        for j in range(nlb):
            blk = skey[:, j * LANE:(j + 1) * LANE]
            top2 = jnp.maximum(top2, jnp.minimum(top1, blk))
            top1 = jnp.maximum(top1, blk)
        return top1, top2

    def score_group(kp, carry):
        for u in range(DSA_UNROLL):
            carry = score_body(kp * DSA_UNROLL + u, carry)
        return carry

    lowest = jnp.full((tq, LANE), INT_MIN, I32)
    tops = lax.fori_loop(0, nkc // DSA_UNROLL, score_group, (lowest, lowest))
    top1, top2 = lax.fori_loop((nkc // DSA_UNROLL) * DSA_UNROLL, nkc, score_body, tops)

    n_blocks = tq // COUNT_ROWS

    def count_ge(t_rep, wanted):
        parts = []
        for blk in range(n_blocks):
            r0 = blk * COUNT_ROWS
            t_b = jnp.tile(t_rep[r0:r0 + COUNT_ROWS], (1, nlb))
            n_sweep = jnp.where(wanted[blk] > 0, nkc, 0)

            def chunk(kc, cnt, r0=r0, t_b=t_b):
                off = pl.multiple_of(kc * tk, tk)
                hit = key_ref[r0:r0 + COUNT_ROWS, pl.ds(off, tk)] >= t_b
                return cnt + _lane_fold(jnp.where(hit, 1, 0))

            def group(kp, cnt, chunk=chunk):
                for u in range(COUNT_UNROLL):
                    cnt = chunk(kp * COUNT_UNROLL + u, cnt)
                return cnt

            cnt = lax.fori_loop(0, n_sweep // COUNT_UNROLL, group, jnp.zeros((COUNT_ROWS, LANE), I32))
            parts.append(lax.fori_loop((n_sweep // COUNT_UNROLL) * COUNT_UNROLL, n_sweep, chunk, cnt))
        return jnp.sum(jnp.concatenate(parts, axis=0), axis=-1, keepdims=True)

    def count(pred):
        def chunk(kc, cnt):
            off = pl.multiple_of(kc * tk, tk)
            hit = pred(key_ref[:, pl.ds(off, tk)], col0 + off)
            return cnt + _lane_fold(jnp.where(hit, 1, 0))
        cnt = lax.fori_loop(0, nkc, chunk, jnp.zeros((tq, LANE), I32))
        return jnp.sum(cnt, axis=-1, keepdims=True)

    rep = lambda v: jnp.broadcast_to(v, (tq, LANE))
    n_valid = row[:, 0:LANE] + 1
    lo0 = rep(jnp.maximum(jnp.min(top2, axis=-1, keepdims=True), INT_MIN + 1))
    hi0 = jnp.maximum(rep(jnp.max(top2 if topk > LANE + 1 else top1, axis=-1, keepdims=True)), lo0)
    searching0 = n_valid > topk

    def any_row(flag):
        return jnp.max(jnp.where(flag, 1, 0))

    def block_flags(active):
        return tuple(any_row(active[blk * COUNT_ROWS:(blk + 1) * COUNT_ROWS]) for blk in range(n_blocks))

    def bis_cond(st):
        flags = st[0]
        busy = flags[0]
        for f in flags[1:]:
            busy = jnp.maximum(busy, f)
        return busy > 0

    def bis_body(st):
        flags, lo, hi, cnt_lo = st
        for _ in range(2):
            active = searching0 & (lo < hi) & (cnt_lo != topk)
            mid = (lo >> 1) + (hi >> 1) + ((lo | hi) & 1)
            cnt = rep(count_ge(mid, flags))
            up = active & (cnt >= topk)
            dn = active & (cnt < topk)
            lo = jnp.where(up, mid, lo)
            cnt_lo = jnp.where(up, cnt, cnt_lo)
            hi = jnp.where(dn, mid - 1, hi)
        return block_flags(searching0 & (lo < hi) & (cnt_lo != topk)), lo, hi, cnt_lo

    st0 = (block_flags(searching0 & (lo0 < hi0)), lo0, hi0, jnp.full((tq, LANE), topk + 1, I32))
    _, thr, _, cnt_thr = lax.while_loop(bis_cond, bis_body, st0)
    has_ties = any_row(searching0 & (cnt_thr != topk)) > 0
    searching0 = searching0[:, 0:1]
    thr = jnp.where(searching0, thr[:, 0:1], INT_MIN + 1)

    q = q_ref[...]
    qs = jnp.concatenate([q[:, h * DSA_LATENT:(h + 1) * DSA_LATENT] for h in range(DSA_HEADS)], axis=0)
    n_groups = 4
    grows = DSA_HEADS * tq // n_groups

    cut_ref[...] = jnp.full(cut_ref.shape, s_len, I32)

    @pl.when(has_ties)
    def _():
        need = topk - count(lambda k, c: k > thr)

        def cut_body(it, cut):
            cand = cut | jnp.left_shift(jnp.int32(1), nbits_col - 1 - it)
            cnt = count(lambda k, c: (k == thr) & (c < cand))
            return jnp.where(cnt < need, cand, cut)

        cut = lax.fori_loop(0, nbits_col, cut_body, jnp.zeros((tq, 1), I32))
        cut_ref[...] = jnp.broadcast_to(cut + 1, cut_ref.shape)

    tie_cut = cut_ref[:, 0:1]

    @pl.when(qi == 0)
    def _():
        def body(i, mx):
            cf = c_ref[pl.ds(pl.multiple_of(i * tk, tk), tk), :].astype(F32)
            return jnp.maximum(mx, jnp.max(jnp.sum(cf * cf, axis=-1, keepdims=True), axis=0, keepdims=True))
        mx = lax.fori_loop(0, s_len // tk, body, jnp.zeros((1, 1), F32))
        cn_ref[...] = jnp.broadcast_to(jnp.sqrt(mx), cn_ref.shape)

    def selection_bias(kc):
        off = pl.multiple_of(kc * tk, tk)
        keys = key_ref[:, pl.ds(off, tk)]
        sel = (keys > thr) | ((keys == thr) & (col0 + off < tie_cut))
        return off, jnp.where(sel, 0.0, NEG_BIG)

    def group_scores(g, cc, bias):
        s = _dot_nt(qs[g * grows:(g + 1) * grows, :], cc)
        return (s.reshape(grows // tq, tq, tk) + bias[None]).reshape(grows, tk)

    def finalize(l_col):
        z = z_ref[...].astype(F32)
        for h in range(DSA_HEADS):
            rs = slice(h * tq, (h + 1) * tq)
            ob = (acc_ref[rs, :] / l_col[rs, :]).astype(BF16)
            oh = jnp.dot(ob, wuv_ref[h], preferred_element_type=F32)
            sl = slice(h * DSA_V_DIM, (h + 1) * DSA_V_DIM)
            o_ref[:, sl] = (oh * _silu(z[:, sl])).astype(o_ref.dtype)

    qf = qs.astype(F32)
    m_ref[...] = jnp.broadcast_to(jnp.sqrt(jnp.sum(qf * qf, axis=-1, keepdims=True))
                                  * cn_ref[:, 0:1] * BOUND_SLACK, m_ref.shape)
    l_ref[...] = jnp.zeros(l_ref.shape, F32)
    acc_ref[...] = jnp.zeros(acc_ref.shape, F32)

    def bounded_chunk(kc):
        off, bias = selection_bias(kc)
        cc = c_ref[pl.ds(off, tk), :]
        for g in range(n_groups):
            rs = slice(g * grows, (g + 1) * grows)
            p = jnp.exp2(group_scores(g, cc, bias) - jnp.tile(m_ref[rs, :], (1, nlb)))
            l_ref[rs, :] = l_ref[rs, :] + _lane_fold(p)
            acc_ref[rs, :] = acc_ref[rs, :] + jnp.dot(p.astype(BF16), cc, preferred_element_type=F32)

    def bounded_group(kp, carry):
        for u in range(DSA_UNROLL):
            bounded_chunk(kp * DSA_UNROLL + u)
        return carry

    def bounded_tail(kc, carry):
        bounded_chunk(kc)
        return carry

    lax.fori_loop(0, nkc // DSA_UNROLL, bounded_group, 0)
    lax.fori_loop((nkc // DSA_UNROLL) * DSA_UNROLL, nkc, bounded_tail, 0)
    l_tot = jnp.sum(l_ref[...], axis=-1, keepdims=True)
    underflow = jnp.logical_not(jnp.min(l_tot) >= UNDERFLOW_GUARD)

    @pl.when(jnp.logical_not(underflow))
    def _():
        finalize(l_tot)

    @pl.when(underflow)
    def _():
        m_ref[...] = jnp.full(m_ref.shape, NEG_BIG, F32)
        l_ref[...] = jnp.zeros(l_ref.shape, F32)
        acc_ref[...] = jnp.zeros(acc_ref.shape, F32)

        def body(kc, carry):
            off, bias = selection_bias(kc)
            cc = c_ref[pl.ds(off, tk), :]
            for g in range(n_groups):
                rs = slice(g * grows, (g + 1) * grows)
                s = group_scores(g, cc, bias)
                m_prev = m_ref[rs, :]
                m_new = jnp.maximum(m_prev, jnp.max(s, axis=-1, keepdims=True))
                alpha = jnp.exp2(m_prev - m_new)
                p = jnp.exp2(s - jnp.tile(m_new, (1, nlb)))
                l_ref[rs, :] = alpha * l_ref[rs, :] + jnp.sum(p, axis=-1, keepdims=True)
                acc_ref[rs, :] = (jnp.tile(alpha, (1, DSA_LATENT // LANE)) * acc_ref[rs, :]
                                  + jnp.dot(p.astype(BF16), cc, preferred_element_type=F32))
                m_ref[rs, :] = m_new
            return carry

        lax.fori_loop(0, nkc, body, 0)
        finalize(l_ref[:, 0:1])


def _dsa_attention(p_main, iw, c_lat, ik_n, w_uv, b, s, tq, tk):
    m = b * s
    nq = s // tq
    topk = min(IDX_TOPK_MAX, s // 4)
    assert topk <= 2 * LANE, "the per-lane top-two start of the threshold search covers 2 * LANE keys"
    c_q = MAIN_OFF[SEG_DSA_Q] // (DSA_HEADS * DSA_LATENT)
    c_z = MAIN_OFF[SEG_DSA_Z] // BRANCH_WIDTH
    c_iq = MAIN_OFF[SEG_IDX_Q] // (IDX_HEADS * IDX_DIM)
    nbits_col = max(1, int(math.ceil(math.log2(s))) + 1)
    kern = functools.partial(_dsa_kernel, tq=tq, tk=tk, topk=topk, nbits_col=nbits_col, s_len=s)
    return pl.pallas_call(
        kern,
        grid=(b, nq),
        in_specs=[
            pl.BlockSpec((tq, DSA_HEADS * DSA_LATENT), lambda bi, qi: (bi * nq + qi, c_q)),
            pl.BlockSpec((tq, IDX_HEADS * IDX_DIM), lambda bi, qi: (bi * nq + qi, c_iq)),
            pl.BlockSpec((tq, IDX_HEADS), lambda bi, qi: (bi * nq + qi, 0)),
            pl.BlockSpec((tq, BRANCH_WIDTH), lambda bi, qi: (bi * nq + qi, c_z)),
            pl.BlockSpec((s, DSA_LATENT), lambda bi, qi: (bi, 0), pipeline_mode=pl.Buffered(1)),
            pl.BlockSpec((s, IDX_DIM), lambda bi, qi: (bi, 0), pipeline_mode=pl.Buffered(1)),
            pl.BlockSpec((DSA_HEADS, DSA_LATENT, DSA_V_DIM), lambda bi, qi: (0, 0, 0)),
        ],
        out_specs=pl.BlockSpec((tq, DSA_HEADS * DSA_V_DIM), lambda bi, qi: (bi * nq + qi, 0)),
        out_shape=jax.ShapeDtypeStruct((m, DSA_HEADS * DSA_V_DIM), BF16),
        scratch_shapes=[pltpu.VMEM((tq, s), I32),
                        pltpu.VMEM((DSA_HEADS * tq, LANE), F32),
                        pltpu.VMEM((DSA_HEADS * tq, LANE), F32),
                        pltpu.VMEM((DSA_HEADS * tq, DSA_LATENT), F32),
                        pltpu.VMEM((tq, LANE), I32),
                        pltpu.VMEM((1, LANE), F32),
                        pltpu.VMEM((IDX_HEADS, tq, LANE), F32)],
        compiler_params=_cparams(("parallel", "arbitrary")),
        name="dsa_attention",
    )(p_main, p_main, iw, p_main, c_lat, ik_n, w_uv)


def _out_kernel(x_ref, ya_ref, yb_ref, yc_ref, ga_ref, gb_ref, gc_ref, wb_ref, wo_ref, pg_ref, o_ref):
    mixed = None
    for i, (y_ref, g_ref) in enumerate(((ya_ref, ga_ref), (yb_ref, gb_ref), (yc_ref, gc_ref))):
        t = _sigmoid(g_ref[...].astype(F32)) * jnp.dot(y_ref[...], wb_ref[i], preferred_element_type=F32)
        mixed = t if mixed is None else mixed + t
    out = jnp.dot(mixed.astype(BF16), wo_ref[...], preferred_element_type=F32)
    ms = jnp.mean(out * out, axis=-1, keepdims=True)
    o_ref[...] = x_ref[...] + out * lax.rsqrt(ms + NORM_EPS) * pg_ref[...]


def _merge_out(x2, ya, yb, yc, p_main, w_branch, w_out, post_g, tm):
    m, d = x2.shape
    cg = MAIN_OFF[SEG_GATES] // d
    row = lambda i: (i, 0)
    return pl.pallas_call(
        _out_kernel,
        grid=(m // tm,),
        in_specs=[
            pl.BlockSpec((tm, d), row),
            pl.BlockSpec((tm, BRANCH_WIDTH), row),
            pl.BlockSpec((tm, BRANCH_WIDTH), row),
            pl.BlockSpec((tm, BRANCH_WIDTH), row),
            pl.BlockSpec((tm, d), lambda i: (i, cg)),
            pl.BlockSpec((tm, d), lambda i: (i, cg + 1)),
            pl.BlockSpec((tm, d), lambda i: (i, cg + 2)),
            pl.BlockSpec((N_BRANCHES, BRANCH_WIDTH, d), lambda i: (0, 0, 0), pipeline_mode=pl.Buffered(1)),
            pl.BlockSpec((d, d), lambda i: (0, 0), pipeline_mode=pl.Buffered(1)),
            pl.BlockSpec((1, d), lambda i: (0, 0)),
        ],
        out_specs=pl.BlockSpec((tm, d), row),
        out_shape=jax.ShapeDtypeStruct((m, d), F32),
        compiler_params=_cparams(("parallel",)),
        name="merge_out_proj",
    )(x2, ya, yb, yc, p_main, p_main, p_main, w_branch, w_out, post_g.reshape(1, d))


def _pick(total, want):
    t = min(total, want)
    while total % t:
        t //= 2
    return t


def kernel(x, pre_norm_g, w_in, da_lambda, da_norm_g, ml_conv_w, ml_conv_b, ml_gate_b, ml_norm_g,
           dsa_kv_norm_g, dsa_ik_norm_g, dsa_w_uv, w_branch, w_out, post_norm_g):
    b, s, d = x.shape
    m = b * s
    bounds = _seg_bounds()
    x2 = x.reshape(m, d)
    tm_proj = _pick(m, 1024)
    t_da = _pick(s, 512)
    l_ml = _pick(s, 256)
    tq_dsa = _pick(s, 256)
    tk_dsa = _pick(s, 512)
    o_if = SMALL_OFF[SEG_ML_IF]
    o_iw = SMALL_OFF[SEG_IDX_W]
    for l in range(DEPTH):
        w = w_in[l]
        col_scale = {SEG_DA_Q: DA_HEAD_DIM ** -0.5 * LOG2E, SEG_DSA_Q: DSA_LATENT ** -0.5 * LOG2E}
        w_main = jnp.concatenate([w[:, bounds[sg]:bounds[sg + 1]] * col_scale.get(sg, 1.0) for sg in MAIN_SEGS],
                                 axis=1).astype(BF16)
        w_small = jnp.concatenate([w[:, bounds[sg]:bounds[sg + 1]] for sg in SMALL_SEGS]
                                  + [jnp.zeros((d, SMALL_WIDTH - SMALL_USED), w.dtype)], axis=1).astype(BF16)
        p_main, p_small = _project(x2, pre_norm_g[l], w_main, w_small, tm_proj, _pick(MAIN_WIDTH, 2048))

        lam_init = 0.8 - 0.6 * math.exp(-0.3 * l)
        y_a = _diff_attention(p_main, da_lambda[l], da_norm_g[l], lam_init, b, s, t_da)

        gif = p_small[:, o_if:o_if + 2 * ML_HEADS]
        gif_t = jnp.transpose(gif.reshape(b, s, 2 * ML_HEADS), (0, 2, 1))
        y_b = _mlstm(p_main, gif, gif_t, ml_conv_w[l], ml_conv_b[l], ml_gate_b[l], ml_norm_g[l], b, s, l_ml)

        c_lat, ik_n = _dsa_norms(p_small, dsa_kv_norm_g[l], dsa_ik_norm_g[l], _pick(m, 2048))
        iw = p_small[:, o_iw:o_iw + IDX_HEADS]
        y_c = _dsa_attention(p_main, iw, c_lat, ik_n, dsa_w_uv[l].astype(BF16), b, s, tq_dsa, tk_dsa)

        x2 = _merge_out(x2, y_a, y_b, y_c, p_main, w_branch[l].astype(BF16), w_out[l].astype(BF16),
                        post_norm_g[l], _pick(m, 256))
    return x2.reshape(b, s, d)
```
